```python
import math
import jax
import jax.numpy as jnp
from jax import lax
import numpy as np

D_MODEL = 1024
BATCH = 8
SEQ = 4096
DEPTH = 2

N_BRANCH = 4
BRANCH_W = 512
SSD_HEADS = 8
SSD_HEAD_DIM = 64
SSD_GROUPS = 2
SSD_STATE = 128
SSD_CHUNK = 128
SSD_CONV = 4
SSD_XBC = BRANCH_W + 2 * SSD_GROUPS * SSD_STATE
DN_HEADS = 4
DN_HEAD_DIM = 128
DN_CHUNK = 64
DN_CONV = 4
SG_GROUPS = 4
SG_GROUP_DIM = BRANCH_W // SG_GROUPS
SG_CHUNK = 128
FOX_HEADS = 8
FOX_HEAD_DIM = 64
FOX_BLOCK = 128
D_FF = 4 * D_MODEL
IN_SIZES = (BRANCH_W, SSD_XBC, SSD_HEADS, 3 * BRANCH_W, DN_HEADS, DN_HEADS, BRANCH_W,
            2 * BRANCH_W, 3 * BRANCH_W, FOX_HEADS, N_BRANCH * D_MODEL)
D_IN = sum(IN_SIZES)
LN_EPS = 1e-5
NORM_EPS = 1e-6
DEEPNORM_ALPHA = (2 * DEPTH) ** 0.25
DEEPNORM_BETA = (8 * DEPTH) ** -0.25

kernel_name = "hybrid_ssd_deltanet_sgmlp_fox"


def layer_norm(x, g, b):
    xf = x.astype(jnp.float32)
    mu = jnp.mean(xf, -1, keepdims=True)
    var = jnp.mean(jnp.square(xf - mu), -1, keepdims=True)
    y = (xf - mu) * lax.rsqrt(var + LN_EPS) * g.astype(jnp.float32) + b.astype(jnp.float32)
    return y.astype(x.dtype)


def rms_norm(x, w):
    xf = x.astype(jnp.float32)
    return xf * lax.rsqrt(jnp.mean(xf * xf, -1, keepdims=True) + NORM_EPS) * w.astype(jnp.float32)


def l2_normalize(t):
    return t * lax.rsqrt(jnp.sum(t * t, -1, keepdims=True) + NORM_EPS)


def causal_dwconv(x, w):
    k = w.shape[0]
    return lax.conv_general_dilated(x, w[:, None, :], window_strides=(1,), padding=[(k - 1, 0)],
                                    dimension_numbers=('NWC', 'WIO', 'NWC'),
                                    feature_group_count=x.shape[-1])


def ssd_mixer(z, xbc, dt_raw, conv_w, conv_b, dt_bias, a_log, d_skip, norm_w):
    f32 = jnp.float32
    b, L, _ = z.shape
    nc, Q, G = L // SSD_CHUNK, SSD_CHUNK, SSD_GROUPS
    e = SSD_HEADS // G
    xbc = jax.nn.silu(causal_dwconv(xbc, conv_w) + conv_b)
    xs, bm, cm = jnp.split(xbc.astype(f32), [BRANCH_W, BRANCH_W + G * SSD_STATE], axis=-1)
    dt = jax.nn.softplus(dt_raw.astype(f32) + dt_bias.astype(f32))
    a = -jnp.exp(a_log.astype(f32))
    xh = xs.reshape(b, nc, Q, G, e, SSD_HEAD_DIM)
    xdt = xh * dt.reshape(b, nc, Q, G, e)[..., None]
    bm = bm.reshape(b, nc, Q, G, SSD_STATE)
    cm = cm.reshape(b, nc, Q, G, SSD_STATE)
    a_cs = jnp.cumsum((dt * a).reshape(b, nc, Q, G, e).transpose(0, 3, 4, 1, 2), axis=-1)
    tri = jnp.tril(jnp.ones((Q, Q), bool))
    seg = jnp.exp(jnp.where(tri, a_cs[..., :, None] - a_cs[..., None, :], -jnp.inf))
    cb = jnp.einsum('bclgn,bcsgn->bgcls', cm, bm)
    y_diag = jnp.einsum('bgcls,bgecls,bcsgep->bclgep', cb, seg, xdt)
    decay_to_end = jnp.exp(a_cs[..., -1:] - a_cs)
    states = jnp.einsum('bcsgn,bgecs,bcsgep->cbgepn', bm, decay_to_end, xdt)
    chunk_decay = jnp.exp(a_cs[..., -1]).transpose(3, 0, 1, 2)

    def step(h, inp):
        st, dec = inp
        return h * dec[..., None, None] + st, h

    _, prev = lax.scan(step, jnp.zeros(states.shape[1:], f32), (states, chunk_decay))
    y_off = jnp.einsum('bclgn,cbgepn,bgecl->bclgep', cm, prev, jnp.exp(a_cs))
    y = y_diag + y_off + d_skip.astype(f32).reshape(G, e)[:, :, None] * xh
    y = y.reshape(b, L, BRANCH_W) * jax.nn.silu(z.astype(f32))
    return rms_norm(y, norm_w).astype(z.dtype)


def gated_deltanet_mixer(qkv, beta_raw, a_raw, gate, conv_w, a_log, dt_bias, norm_w):
    f32 = jnp.float32
    b, L, _ = qkv.shape
    H, Dk, C = DN_HEADS, DN_HEAD_DIM, DN_CHUNK
    nc = L // C
    qkv = jax.nn.silu(causal_dwconv(qkv, conv_w)).astype(f32)
    q, k, v = jnp.split(qkv, 3, axis=-1)
    q = l2_normalize(q.reshape(b, L, H, Dk)) * (Dk ** -0.5)
    k = l2_normalize(k.reshape(b, L, H, Dk))
    v = v.reshape(b, L, H, Dk)

    def chunk4(t):
        return t.reshape(b, nc, C, H, Dk).transpose(1, 0, 3, 2, 4)

    def chunk3(t):
        return t.reshape(b, nc, C, H).transpose(1, 0, 3, 2)

    q, k, v = chunk4(q), chunk4(k), chunk4(v)
    beta = chunk3(jax.nn.sigmoid(beta_raw.astype(f32)))
    g = -jnp.exp(a_log.astype(f32)) * jax.nn.softplus(a_raw.astype(f32) + dt_bias.astype(f32))
    g_cs = jnp.cumsum(chunk3(g), axis=-1)
    tri = jnp.tril(jnp.ones((C, C), bool))
    strict = jnp.tril(jnp.ones((C, C), f32), -1)
    gamma = jnp.exp(jnp.where(tri, g_cs[..., :, None] - g_cs[..., None, :], -jnp.inf))
    kb = k * beta[..., None]
    a_mat = jnp.einsum('nbhid,nbhjd->nbhij', kb, k) * gamma * strict
    m = a_mat + jnp.eye(C, dtype=f32)
    rhs = jnp.concatenate([kb * jnp.exp(g_cs)[..., None], v * beta[..., None]], axis=-1)
    sol = lax.linalg.triangular_solve(m, rhs, left_side=True, lower=True, unit_diagonal=True)
    w_c, u_c = jnp.split(sol, 2, axis=-1)
    qg = q * jnp.exp(g_cs)[..., None]
    qk = jnp.einsum('nbhid,nbhjd->nbhij', q, k) * gamma
    k_dec = k * jnp.exp(g_cs[..., -1:] - g_cs)[..., None]
    last = jnp.exp(g_cs[..., -1])

    def step(S, inp):
        qg_c, qk_c, w_i, u_i, kd_c, last_c = inp
        v_new = u_i - jnp.einsum('bhcd,bhde->bhce', w_i, S)
        o = jnp.einsum('bhcd,bhde->bhce', qg_c, S) + jnp.einsum('bhij,bhje->bhie', qk_c, v_new)
        S = S * last_c[..., None, None] + jnp.einsum('bhcd,bhce->bhde', kd_c, v_new)
        return S, o

    S0 = jnp.zeros((b, H, Dk, Dk), f32)
    _, o = lax.scan(step, S0, (qg, qk, w_c, u_c, k_dec, last))
    o = o.transpose(1, 0, 3, 2, 4).reshape(b, L, H, Dk)
    o = rms_norm(o, norm_w) * jax.nn.silu(gate.astype(f32).reshape(b, L, H, Dk))
    return o.reshape(b, L, BRANCH_W).astype(gate.dtype)


def spatial_gating_mixer(uv, ln_g, ln_b, w_s, b_s):
    b, L, _ = uv.shape
    nc = L // SG_CHUNK
    u, v = jnp.split(jax.nn.gelu(uv), 2, axis=-1)
    v = layer_norm(v, ln_g, ln_b).reshape(b, nc, SG_CHUNK, SG_GROUPS, SG_GROUP_DIM)
    w = w_s * jnp.tril(jnp.ones((SG_CHUNK, SG_CHUNK), w_s.dtype))
    v = jnp.einsum('gts,bcsgd->bctgd', w, v) + b_s.T[None, None, :, :, None]
    return u * v.reshape(b, L, BRANCH_W)


def forgetting_attention_mixer(qkv, f_raw, f_bias):
    f32 = jnp.float32
    b, L, _ = qkv.shape
    H, Dh, BLK = FOX_HEADS, FOX_HEAD_DIM, FOX_BLOCK
    nb = L // BLK
    q, k, v = (t.reshape(b, L, H, Dh) for t in jnp.split(qkv, 3, axis=-1))
    c = jnp.cumsum(jax.nn.log_sigmoid(f_raw.astype(f32) + f_bias.astype(f32)), axis=1)
    c_keys = c.transpose(0, 2, 1)
    qb = q.reshape(b, nb, BLK, H, Dh).transpose(1, 0, 2, 3, 4)
    cb = c.reshape(b, nb, BLK, H).transpose(1, 0, 3, 2)
    starts = jnp.arange(nb, dtype=jnp.int32) * BLK
    k_pos = jnp.arange(L, dtype=jnp.int32)
    scale = Dh ** -0.5

    def block(args):
        q_blk, c_blk, start = args
        s = jnp.einsum('bqhd,bkhd->bhqk', q_blk, k).astype(f32) * scale
        s = s + c_blk[..., None] - c_keys[:, :, None, :]
        q_pos = start + jnp.arange(BLK, dtype=jnp.int32)
        s = jnp.where(k_pos[None, :] <= q_pos[:, None], s, -jnp.inf)
        p = jax.nn.softmax(s, axis=-1)
        return jnp.einsum('bhqk,bkhd->bqhd', p.astype(v.dtype), v)

    o = lax.map(block, (qb, cb, starts))
    return o.transpose(1, 0, 2, 3, 4).reshape(b, L, BRANCH_W)


def setup_inputs(seed: int = 0) -> dict:
    key = jax.random.key(seed)
    ks = iter(jax.random.split(key, 40))
    f32 = jnp.float32

    def nrm(shape, scale):
        return scale * jax.random.normal(next(ks), shape, f32)

    def gain(shape):
        return 1.0 + nrm(shape, 0.02)

    def dt_bias_init(shape):
        u = jax.random.uniform(next(ks), shape, f32)
        dt = jnp.exp(u * (math.log(0.1) - math.log(0.001)) + math.log(0.001))
        return dt + jnp.log(-jnp.expm1(-dt))

    def a_log_init(shape):
        return jnp.log(jax.random.uniform(next(ks), shape, f32, minval=1.0, maxval=16.0))

    x = nrm((BATCH, SEQ, D_MODEL), 1.0)
    return {
        "x": x,
        "ln_in_g": gain((D_MODEL,)),
        "ln_in_b": nrm((D_MODEL,), 0.02),
        "w_in": nrm((DEPTH, D_MODEL, D_IN), D_MODEL ** -0.5),
        "ssd_conv_w": nrm((DEPTH, SSD_CONV, SSD_XBC), SSD_CONV ** -0.5),
        "ssd_conv_b": nrm((DEPTH, SSD_XBC), 0.02),
        "ssd_dt_bias": dt_bias_init((DEPTH, SSD_HEADS)),
        "ssd_a_log": a_log_init((DEPTH, SSD_HEADS)),
        "ssd_d": gain((DEPTH, SSD_HEADS)),
        "ssd_norm_w": gain((DEPTH, BRANCH_W)),
        "dn_conv_w": nrm((DEPTH, DN_CONV, 3 * BRANCH_W), DN_CONV ** -0.5),
        "dn_a_log": a_log_init((DEPTH, DN_HEADS)),
        "dn_dt_bias": dt_bias_init((DEPTH, DN_HEADS)),
        "dn_norm_w": gain((DEPTH, DN_HEAD_DIM)),
        "sg_ln_g": gain((DEPTH, BRANCH_W)),
        "sg_ln_b": nrm((DEPTH, BRANCH_W), 0.02),
        "sg_w": nrm((DEPTH, SG_GROUPS, SG_CHUNK, SG_CHUNK), SG_CHUNK ** -0.5),
        "sg_b": 1.0 + nrm((DEPTH, SG_GROUPS, SG_CHUNK), 0.1),
        "fox_f_bias": 2.0 + nrm((DEPTH, FOX_HEADS), 0.5),
        "gate_b": nrm((DEPTH, N_BRANCH, D_MODEL), 0.02),
        "w_branch": nrm((DEPTH, N_BRANCH, BRANCH_W, D_MODEL), BRANCH_W ** -0.5),
        "w_out": nrm((DEPTH, D_MODEL, D_MODEL), DEEPNORM_BETA * D_MODEL ** -0.5),
        "ln1_g": gain((DEPTH, D_MODEL)),
        "ln1_b": nrm((DEPTH, D_MODEL), 0.02),
        "w_up": nrm((DEPTH, D_MODEL, D_FF), D_MODEL ** -0.5),
        "w_down": nrm((DEPTH, D_FF, D_MODEL), DEEPNORM_BETA * D_FF ** -0.5),
        "ln2_g": gain((DEPTH, D_MODEL)),
        "ln2_b": nrm((DEPTH, D_MODEL), 0.02),
    }


def reference(x, ln_in_g, ln_in_b, w_in, ssd_conv_w, ssd_conv_b, ssd_dt_bias, ssd_a_log, ssd_d,
              ssd_norm_w, dn_conv_w, dn_a_log, dn_dt_bias, dn_norm_w, sg_ln_g, sg_ln_b, sg_w, sg_b,
              fox_f_bias, gate_b, w_branch, w_out, ln1_g, ln1_b, w_up, w_down, ln2_g, ln2_b):
    b, L, _ = x.shape
    splits = np.cumsum(IN_SIZES)[:-1].tolist()
    h = layer_norm(x, ln_in_g, ln_in_b)
    for l in range(DEPTH):
        proj = jnp.einsum('bld,dc->blc', h, w_in[l])
        (ssd_z, ssd_xbc, ssd_dt, dn_qkv, dn_beta, dn_a, dn_gate,
         sg_uv, fox_qkv, fox_f, gate_logits) = jnp.split(proj, splits, axis=-1)
        y_a = ssd_mixer(ssd_z, ssd_xbc, ssd_dt, ssd_conv_w[l], ssd_conv_b[l], ssd_dt_bias[l],
                        ssd_a_log[l], ssd_d[l], ssd_norm_w[l])
        y_b = gated_deltanet_mixer(dn_qkv, dn_beta, dn_a, dn_gate, dn_conv_w[l], dn_a_log[l],
                                   dn_dt_bias[l], dn_norm_w[l])
        y_c = spatial_gating_mixer(sg_uv, sg_ln_g[l], sg_ln_b[l], sg_w[l], sg_b[l])
        y_d = forgetting_attention_mixer(fox_qkv, fox_f, fox_f_bias[l])
        gates = jax.nn.sigmoid(gate_logits.reshape(b, L, N_BRANCH, D_MODEL) + gate_b[l])
        merged = gates[:, :, 0] * jnp.einsum('blc,cd->bld', y_a, w_branch[l, 0])
        merged = merged + gates[:, :, 1] * jnp.einsum('blc,cd->bld', y_b, w_branch[l, 1])
        merged = merged + gates[:, :, 2] * jnp.einsum('blc,cd->bld', y_c, w_branch[l, 2])
        merged = merged + gates[:, :, 3] * jnp.einsum('blc,cd->bld', y_d, w_branch[l, 3])
        mix = jnp.einsum('bld,de->ble', merged, w_out[l])
        h = layer_norm(DEEPNORM_ALPHA * h + mix, ln1_g[l], ln1_b[l])
        ff = jnp.einsum('blf,fd->bld', jnp.square(jax.nn.relu(jnp.einsum('bld,df->blf', h, w_up[l]))), w_down[l])
        h = layer_norm(DEEPNORM_ALPHA * h + ff, ln2_g[l], ln2_b[l])
    return h
```

```python
import functools
import math

import numpy as np
import jax
import jax.numpy as jnp
from jax import lax
from jax.experimental import pallas as pl
from jax.experimental.pallas import tpu as pltpu

F32 = jnp.float32
BF16 = jnp.bfloat16

BRANCH_W = 512
N_BRANCH = 4
SSD_HEADS, SSD_HEAD_DIM, SSD_GROUPS, SSD_STATE, SSD_CONV = 8, 64, 2, 128, 4
SSD_XBC = BRANCH_W + 2 * SSD_GROUPS * SSD_STATE
DN_HEADS, DN_HEAD_DIM, DN_CONV = 4, 128, 4
SG_GROUPS, SG_CHUNK = 4, 128
FOX_HEADS, FOX_HEAD_DIM = 8, 64
LN_EPS = 1e-5
NORM_EPS = 1e-6
NEG_BIG = -1e30

LANES = 128
SUBLANES = 8
VMEM_LIMIT_BYTES = 56 * 1024 * 1024

SM_DT, SM_BETA, SM_A, SM_F = 0, 8, 12, 16

SSD_CHUNK = 128
DN_CHUNK = 64


def _dot(a, b):
    return jnp.dot(a, b, preferred_element_type=F32)


def _dot_nt(a, b):
    return lax.dot_general(a, b, (((1,), (1,)), ((), ())), preferred_element_type=F32)


def _dot_tn(a, b):
    return lax.dot_general(a, b, (((0,), (0,)), ((), ())), preferred_element_type=F32)


def _bdot(a, b):
    return _dot(a.astype(BF16), b.astype(BF16))


def _split_terms(x, n):
    terms, r = [], x
    for i in range(n):
        p = r.astype(BF16)
        terms.append(p)
        if i + 1 < n:
            r = r - p.astype(F32)
    return terms


def _sel_right(x, m, n):
    return sum(_dot(p, m) for p in _split_terms(x, n))


def _sel_left(m, x, n):
    return sum(_dot(m, p) for p in _split_terms(x, n))


def _silu(x):
    return x * jax.nn.sigmoid(x)


def _softplus(x):
    return jnp.maximum(x, 0.0) + jnp.log1p(jnp.exp(-jnp.abs(x)))


def _layer_norm(x, g, b):
    mu = jnp.mean(x, axis=-1, keepdims=True)
    xc = x - mu
    var = jnp.mean(xc * xc, axis=-1, keepdims=True)
    return xc * lax.rsqrt(var + LN_EPS) * g + b


def _causal_conv(x, tail_ref, w):
    n = x.shape[0]
    k = w.shape[0]
    tail = tail_ref[...]
    row8 = lax.broadcasted_iota(jnp.int32, (SUBLANES, x.shape[1]), 0)
    acc = x * w[k - 1:k, :]
    for s in range(1, k):
        xr = pltpu.roll(x, s, axis=0)
        pr = pltpu.roll(tail, s, axis=0)
        head = jnp.where(row8 < s, pr, xr[:SUBLANES])
        xs = jnp.concatenate([head, xr[SUBLANES:]], axis=0)
        acc = acc + xs * w[k - 1 - s:k - s, :]
    tail_ref[...] = x[n - SUBLANES:]
    return acc


def _cparams(sem):
    return pltpu.CompilerParams(dimension_semantics=sem, vmem_limit_bytes=VMEM_LIMIT_BYTES)


def _const_spec(shape):
    nd = len(shape)
    return pl.BlockSpec(shape, lambda *_: (0,) * nd)


def _ln_kernel(x_ref, g_ref, b_ref, o32_ref, o16_ref):
    y = _layer_norm(x_ref[...], g_ref[...], b_ref[...])
    o32_ref[...] = y
    o16_ref[...] = y.astype(BF16)


def _entry_norm(x2, g, b, tm):
    t, d = x2.shape
    row = pl.BlockSpec((tm, d), lambda i: (i, 0))
    return pl.pallas_call(
        _ln_kernel,
        grid=(t // tm,),
        in_specs=[row, _const_spec((1, d)), _const_spec((1, d))],
        out_specs=[row, row],
        out_shape=[jax.ShapeDtypeStruct((t, d), F32), jax.ShapeDtypeStruct((t, d), BF16)],
        compiler_params=_cparams(("parallel",)),
        name="entry_norm",
    )(x2, g.reshape(1, d), b.reshape(1, d))


def _proj_kernel(h_ref, w_ref, ws_ref, o_ref, os_ref):
    h = h_ref[...]
    o_ref[...] = _dot(h, w_ref[...]).astype(BF16)

    @pl.when(pl.program_id(1) == 0)
    def _():
        os_ref[...] = _dot(h, ws_ref[...])


def _in_proj(h16, w_big, w_small, tm, tn):
    t, d = h16.shape
    n = w_big.shape[1]
    return pl.pallas_call(
        _proj_kernel,
        grid=(t // tm, n // tn),
        in_specs=[pl.BlockSpec((tm, d), lambda i, j: (i, 0)),
                  pl.BlockSpec((d, tn), lambda i, j: (0, j)),
                  pl.BlockSpec((d, LANES), lambda i, j: (0, 0))],
        out_specs=[pl.BlockSpec((tm, tn), lambda i, j: (i, j)),
                   pl.BlockSpec((tm, LANES), lambda i, j: (i, 0))],
        out_shape=[jax.ShapeDtypeStruct((t, n), BF16), jax.ShapeDtypeStruct((t, LANES), F32)],
        compiler_params=_cparams(("parallel", "arbitrary")),
        name="in_proj",
    )(h16, w_big, w_small)


def _ssd_kernel(z_ref, x_ref, bc_ref, sm_ref, cwx_ref, cwbc_ref, cbx_ref, cbbc_ref, dtb_ref,
                alog_ref, dskip_ref, nw_ref, e8_ref, tri_ref, o_ref, state_ref, xtail_ref, bctail_ref):
    first = pl.program_id(1) == 0
    ts = x_ref.shape[0]
    q = SSD_CHUNK
    hp = SSD_HEADS // SSD_GROUPS * SSD_HEAD_DIM
    e8 = e8_ref[...]
    tri = tri_ref[...]

    @pl.when(first)
    def _():
        state_ref[...] = jnp.zeros_like(state_ref)
        xtail_ref[...] = jnp.zeros_like(xtail_ref)
        bctail_ref[...] = jnp.zeros_like(bctail_ref)

    xs = _silu(_causal_conv(x_ref[...].astype(F32), xtail_ref, cwx_ref[...]) + cbx_ref[...])
    bcm = _silu(_causal_conv(bc_ref[...].astype(F32), bctail_ref, cwbc_ref[...]) + cbbc_ref[...])
    lane = lax.broadcasted_iota(jnp.int32, (1, LANES), 1)
    head_lane = (lane >= SM_DT) & (lane < SM_DT + SSD_HEADS)
    a_neg = jnp.where(head_lane, -jnp.exp(alog_ref[...]), 0.0)
    dt = _softplus(sm_ref[...] + dtb_ref[...])
    dta = dt * a_neg
    xdt = xs * _sel_right(dt, e8, 2)
    ri = lax.broadcasted_iota(jnp.int32, (q, q), 0)
    ci = lax.broadcasted_iota(jnp.int32, (q, q), 1)
    lower = ci <= ri
    lane_g = lax.broadcasted_iota(jnp.int32, (1, hp), 1) // SSD_HEAD_DIM

    for c in range(ts // q):
        sl = slice(c * q, (c + 1) * q)
        acs = _sel_left(tri, dta[sl], 3)
        acs_t = acs.T
        eacs = _sel_right(jnp.exp(acs), e8, 2)
        dec_end = _sel_right(jnp.exp(acs[q - 1:q, :] - acs), e8, 2)
        ys = []
        for g in range(SSD_GROUPS):
            gl = slice(g * hp, (g + 1) * hp)
            bm = bcm[sl, g * SSD_STATE:(g + 1) * SSD_STATE].astype(BF16)
            cm = bcm[sl, (SSD_GROUPS + g) * SSD_STATE:(SSD_GROUPS + g + 1) * SSD_STATE].astype(BF16)
            cb = _dot_nt(cm, bm)
            xg = xdt[sl, gl]
            s_prev = state_ref[g]
            y = _dot(cm, s_prev.astype(BF16)) * eacs[:, gl]
            for e in range(SSD_HEADS // SSD_GROUPS):
                h = SM_DT + g * (SSD_HEADS // SSD_GROUPS) + e
                seg = jnp.exp(jnp.where(lower, acs[:, h:h + 1] - acs_t[h:h + 1, :], NEG_BIG))
                xe = jnp.where(lane_g == e, xg, 0.0).astype(BF16)
                y = y + _dot((cb * seg).astype(BF16), xe)
            state_ref[g] = s_prev * eacs[q - 1:q, gl] + _dot_tn(bm, (xg * dec_end[:, gl]).astype(BF16))
            ys.append(y)
        y = jnp.concatenate(ys, axis=1) + dskip_ref[...] * xs[sl]
        y = y * _silu(z_ref[sl, :].astype(F32))
        y = y * lax.rsqrt(jnp.mean(y * y, axis=-1, keepdims=True) + NORM_EPS) * nw_ref[...]
        o_ref[sl, :] = y.astype(BF16)


def _ssd_mixer(proj, small, p, batch, seq, ts):
    nb = seq // ts
    w = BRANCH_W
    rows = lambda col: pl.BlockSpec((ts, w), lambda b, s, col=col: (b * nb + s, col))
    hp = SSD_HEADS // SSD_GROUPS * SSD_HEAD_DIM
    return pl.pallas_call(
        _ssd_kernel,
        grid=(batch, nb),
        in_specs=[rows(0), rows(1), rows(2),
                  pl.BlockSpec((ts, LANES), lambda b, s: (b * nb + s, 0)),
                  _const_spec((SSD_CONV, w)), _const_spec((SSD_CONV, w)),
                  _const_spec((1, w)), _const_spec((1, w)),
                  _const_spec((1, LANES)), _const_spec((1, LANES)),
                  _const_spec((1, w)), _const_spec((1, w)),
                  _const_spec((LANES, w)), _const_spec((SSD_CHUNK, SSD_CHUNK))],
        out_specs=pl.BlockSpec((ts, w), lambda b, s: (b * nb + s, 0)),
        out_shape=jax.ShapeDtypeStruct((batch * seq, w), BF16),
        scratch_shapes=[pltpu.VMEM((SSD_GROUPS, SSD_STATE, hp), F32),
                        pltpu.VMEM((SUBLANES, w), F32), pltpu.VMEM((SUBLANES, w), F32)],
        compiler_params=_cparams(("parallel", "arbitrary")),
        name="ssd_mixer",
    )(proj, proj, proj, small, p["ssd_cwx"], p["ssd_cwbc"], p["ssd_cbx"], p["ssd_cbbc"], p["ssd_dtb"],
      p["ssd_alog"], p["ssd_dskip"], p["ssd_nw"], p["e8"], p["tri_ssd"])


def _unit_lower_inverse(a):
    c = a.shape[0]
    ri = lax.broadcasted_iota(jnp.int32, (c, c), 0)
    ci = lax.broadcasted_iota(jnp.int32, (c, c), 1)
    eye = jnp.where(ri == ci, 1.0, 0.0).astype(F32)
    pw = -a
    inv = eye + pw
    for _ in range(int(math.log2(c)) - 1):
        pw = _bdot(pw, pw)
        inv = inv + _bdot(inv, pw)
    return inv


def _dn_kernel(q_ref, k_ref, v_ref, gate_ref, sm_ref, cwq_ref, cwk_ref, cwv_ref, alog_ref, dtb_ref,
               nw_ref, eb_ref, eg_ref, tri_ref, o_ref, state_ref, qtail_ref, ktail_ref, vtail_ref):
    first = pl.program_id(1) == 0
    ts = q_ref.shape[0]
    c = DN_CHUNK
    dk = DN_HEAD_DIM

    @pl.when(first)
    def _():
        state_ref[...] = jnp.zeros_like(state_ref)
        qtail_ref[...] = jnp.zeros_like(qtail_ref)
        ktail_ref[...] = jnp.zeros_like(ktail_ref)
        vtail_ref[...] = jnp.zeros_like(vtail_ref)

    qf = _silu(_causal_conv(q_ref[...].astype(F32), qtail_ref, cwq_ref[...]))
    kf = _silu(_causal_conv(k_ref[...].astype(F32), ktail_ref, cwk_ref[...]))
    vf = _silu(_causal_conv(v_ref[...].astype(F32), vtail_ref, cwv_ref[...]))
    sm = sm_ref[...]
    lane = lax.broadcasted_iota(jnp.int32, (1, LANES), 1)
    a_lane = (lane >= SM_A) & (lane < SM_A + DN_HEADS)
    neg_a = jnp.where(a_lane, -jnp.exp(alog_ref[...]), 0.0)
    gdec = neg_a * _softplus(sm + dtb_ref[...])
    beta = _sel_right(jax.nn.sigmoid(sm), eb_ref[...], 2)
    tri = tri_ref[...]
    ri = lax.broadcasted_iota(jnp.int32, (c, c), 0)
    ci = lax.broadcasted_iota(jnp.int32, (c, c), 1)
    lower = ci <= ri
    strict = ci < ri

    qn, kn = [], []
    for h in range(DN_HEADS):
        hl = slice(h * dk, (h + 1) * dk)
        qh, kh = qf[:, hl], kf[:, hl]
        qn.append(qh * lax.rsqrt(jnp.sum(qh * qh, axis=-1, keepdims=True) + NORM_EPS) * (dk ** -0.5))
        kn.append(kh * lax.rsqrt(jnp.sum(kh * kh, axis=-1, keepdims=True) + NORM_EPS))

    for ch in range(ts // c):
        sl = slice(ch * c, (ch + 1) * c)
        gcs = _sel_left(tri, gdec[sl], 3)
        gcs_t = gcs.T
        egcs = _sel_right(jnp.exp(gcs), eg_ref[...], 2)
        edec = _sel_right(jnp.exp(gcs[c - 1:c, :] - gcs), eg_ref[...], 2)
        for h in range(DN_HEADS):
            hl = slice(h * dk, (h + 1) * dk)
            gl = SM_A + h
            q, k, v = qn[h][sl], kn[h][sl], vf[sl, hl]
            bh = beta[sl, hl]
            diff = gcs[:, gl:gl + 1] - gcs_t[gl:gl + 1, :]
            gamma = jnp.exp(jnp.where(lower, diff, NEG_BIG))
            kb = k * bh
            k16 = k.astype(BF16)
            a_mat = jnp.where(strict, _dot_nt(kb.astype(BF16), k16) * gamma, 0.0)
            t_inv = _unit_lower_inverse(a_mat).astype(BF16)
            w_c = _dot(t_inv, (kb * egcs[:, hl]).astype(BF16))
            u_c = _dot(t_inv, (v * bh).astype(BF16))
            qk = (_dot_nt(q.astype(BF16), k16) * gamma).astype(BF16)
            s_prev = state_ref[h]
            s16 = s_prev.astype(BF16)
            v_new = u_c - _dot(w_c.astype(BF16), s16)
            vn16 = v_new.astype(BF16)
            o = _dot((q * egcs[:, hl]).astype(BF16), s16) + _dot(qk, vn16)
            state_ref[h] = s_prev * egcs[c - 1:c, hl] + _dot_tn((k * edec[:, hl]).astype(BF16), vn16)
            o = o * lax.rsqrt(jnp.mean(o * o, axis=-1, keepdims=True) + NORM_EPS) * nw_ref[...]
            o_ref[sl, hl] = (o * _silu(gate_ref[sl, hl].astype(F32))).astype(BF16)


def _dn_mixer(proj, small, p, batch, seq, ts):
    nb = seq // ts
    w = BRANCH_W
    rows = lambda col: pl.BlockSpec((ts, w), lambda b, s, col=col: (b * nb + s, col))
    return pl.pallas_call(
        _dn_kernel,
        grid=(batch, nb),
        in_specs=[rows(3), rows(4), rows(5), rows(6),
                  pl.BlockSpec((ts, LANES), lambda b, s: (b * nb + s, 0)),
                  _const_spec((DN_CONV, w)), _const_spec((DN_CONV, w)), _const_spec((DN_CONV, w)),
                  _const_spec((1, LANES)), _const_spec((1, LANES)), _const_spec((1, DN_HEAD_DIM)),
                  _const_spec((LANES, w)), _const_spec((LANES, w)), _const_spec((DN_CHUNK, DN_CHUNK))],
        out_specs=pl.BlockSpec((ts, w), lambda b, s: (b * nb + s, 0)),
        out_shape=jax.ShapeDtypeStruct((batch * seq, w), BF16),
        scratch_shapes=[pltpu.VMEM((DN_HEADS, DN_HEAD_DIM, DN_HEAD_DIM), F32),
                        pltpu.VMEM((SUBLANES, w), F32), pltpu.VMEM((SUBLANES, w), F32),
                        pltpu.VMEM((SUBLANES, w), F32)],
        compiler_params=_cparams(("parallel", "arbitrary")),
        name="dn_mixer",
    )(proj, proj, proj, proj, small, p["dn_cwq"], p["dn_cwk"], p["dn_cwv"], p["dn_alog"], p["dn_dtb"],
      p["dn_nw"], p["eb"], p["eg"], p["tri_dn"])


def _sg_kernel(u_ref, v_ref, g_ref, b_ref, w_ref, bs_ref, o_ref):
    ts = u_ref.shape[0]
    q = SG_CHUNK
    gd = BRANCH_W // SG_GROUPS
    u = jax.nn.gelu(u_ref[...].astype(F32))
    v = _layer_norm(jax.nn.gelu(v_ref[...].astype(F32)), g_ref[...], b_ref[...])
    ri = lax.broadcasted_iota(jnp.int32, (q, q), 0)
    ci = lax.broadcasted_iota(jnp.int32, (q, q), 1)
    lower = ci <= ri
    for g in range(SG_GROUPS):
        gl = slice(g * gd, (g + 1) * gd)
        wg = jnp.where(lower, w_ref[g], 0.0).astype(BF16)
        for c in range(ts // q):
            sl = slice(c * q, (c + 1) * q)
            mixed = _dot(wg, v[sl, gl].astype(BF16)) + bs_ref[:, gl]
            o_ref[sl, gl] = (u[sl, gl] * mixed).astype(BF16)


def _sg_mixer(proj, p, t, ts):
    w = BRANCH_W
    rows = lambda col: pl.BlockSpec((ts, w), lambda i, col=col: (i, col))
    return pl.pallas_call(
        _sg_kernel,
        grid=(t // ts,),
        in_specs=[rows(7), rows(8), _const_spec((1, w)), _const_spec((1, w)),
                  _const_spec((SG_GROUPS, SG_CHUNK, SG_CHUNK)), _const_spec((SG_CHUNK, w))],
        out_specs=pl.BlockSpec((ts, w), lambda i: (i, 0)),
        out_shape=jax.ShapeDtypeStruct((t, w), BF16),
        compiler_params=_cparams(("parallel",)),
        name="sg_mixer",
    )(proj, proj, p["sg_g"], p["sg_b"], p["sg_w"], p["sg_bs"])


def _fcum_kernel(sm_ref, fb_ref, tri_ref, col_ref, row_ref, carry_ref):
    @pl.when(pl.program_id(1) == 0)
    def _():
        carry_ref[...] = jnp.zeros_like(carry_ref)

    n = sm_ref.shape[0]
    ls = jax.nn.log_sigmoid(sm_ref[...] + fb_ref[...])
    c = _sel_left(tri_ref[...], ls, 3) + carry_ref[...]
    carry_ref[...] = c[n - 1:n, :]
    col_ref[...] = c
    row_ref[...] = c.T[SM_F:SM_F + FOX_HEADS, :]


def _forget_cumsum(small, p, batch, seq, tc):
    nb = seq // tc
    return pl.pallas_call(
        _fcum_kernel,
        grid=(batch, nb),
        in_specs=[pl.BlockSpec((tc, LANES), lambda b, s: (b * nb + s, 0)),
                  _const_spec((1, LANES)), _const_spec((tc, tc))],
        out_specs=[pl.BlockSpec((tc, LANES), lambda b, s: (b * nb + s, 0)),
                   pl.BlockSpec((None, FOX_HEADS, tc), lambda b, s: (b, 0, s))],
        out_shape=[jax.ShapeDtypeStruct((batch * seq, LANES), F32),
                   jax.ShapeDtypeStruct((batch, FOX_HEADS, seq), F32)],
        scratch_shapes=[pltpu.VMEM((1, LANES), F32)],
        compiler_params=_cparams(("parallel", "arbitrary")),
        name="forget_cumsum",
    )(small, p["fox_fb"], p["tri_fox"])


def _fox_kernel(q_ref, k_ref, v_ref, ccol_ref, crow_ref, o_ref, m_ref, l_ref, acc_ref):
    pair = pl.program_id(1)
    i = pl.program_id(2)
    j = pl.program_id(3)
    tq, tk = q_ref.shape[0], k_ref.shape[0]
    scale = FOX_HEAD_DIM ** -0.5
    per = LANES // FOX_HEAD_DIM

    @pl.when(j == 0)
    def _():
        m_ref[...] = jnp.full_like(m_ref, NEG_BIG)
        l_ref[...] = jnp.zeros_like(l_ref)
        acc_ref[...] = jnp.zeros_like(acc_ref)

    @pl.when(j <= i)
    def _():
        q2 = q_ref[...]
        k2 = k_ref[...]
        v2 = v_ref[...]
        lane = lax.broadcasted_iota(jnp.int32, (1, LANES), 1)
        qpos = i * tq + lax.broadcasted_iota(jnp.int32, (tq, tk), 0)
        kpos = j * tk + lax.broadcasted_iota(jnp.int32, (tq, tk), 1)
        causal = kpos <= qpos
        ccol = ccol_ref[...]
        for e in range(per):
            qe = jnp.where(lane // FOX_HEAD_DIM == e, q2, jnp.zeros_like(q2))
            s = _dot_nt(qe, k2) * scale
            hsel = lane == SM_F + pair * per + e
            cq = jnp.sum(jnp.where(hsel, ccol, 0.0), axis=-1, keepdims=True)
            ck = crow_ref[pl.ds(pair * per + e, 1), :]
            s = jnp.where(causal, s + (cq - ck), NEG_BIG)
            m_old = m_ref[e]
            m_new = jnp.maximum(m_old, jnp.max(s, axis=-1, keepdims=True))
            alpha = jnp.exp(m_old - m_new)
            pr = jnp.exp(s - m_new[:, :1])
            l_ref[e] = alpha * l_ref[e] + jnp.sum(pr, axis=-1, keepdims=True)
            acc_ref[e] = alpha * acc_ref[e] + _dot(pr.astype(BF16), v2)
            m_ref[e] = m_new

    @pl.when(j == pl.num_programs(3) - 1)
    def _():
        lane = lax.broadcasted_iota(jnp.int32, (1, LANES), 1)
        out = jnp.zeros((tq, LANES), F32)
        for e in range(per):
            out = jnp.where(lane // FOX_HEAD_DIM == e, acc_ref[e] / l_ref[e], out)
        o_ref[...] = out.astype(BF16)


def _fox_mixer(proj, ccol, crow, batch, seq, tq):
    nq = seq // tq
    pairs = FOX_HEADS * FOX_HEAD_DIM // LANES
    qcol, kcol, vcol = (9 * BRANCH_W // LANES, 10 * BRANCH_W // LANES, 11 * BRANCH_W // LANES)
    return pl.pallas_call(
        _fox_kernel,
        grid=(batch, pairs, nq, nq),
        in_specs=[pl.BlockSpec((tq, LANES), lambda b, p, i, j: (b * nq + i, qcol + p)),
                  pl.BlockSpec((tq, LANES), lambda b, p, i, j: (b * nq + jnp.minimum(j, i), kcol + p)),
                  pl.BlockSpec((tq, LANES), lambda b, p, i, j: (b * nq + jnp.minimum(j, i), vcol + p)),
                  pl.BlockSpec((tq, LANES), lambda b, p, i, j: (b * nq + i, 0)),
                  pl.BlockSpec((None, FOX_HEADS, tq), lambda b, p, i, j: (b, 0, jnp.minimum(j, i)))],
        out_specs=pl.BlockSpec((tq, LANES), lambda b, p, i, j: (b * nq + i, p)),
        out_shape=jax.ShapeDtypeStruct((batch * seq, BRANCH_W), BF16),
        scratch_shapes=[pltpu.VMEM((LANES // FOX_HEAD_DIM, tq, LANES), F32),
                        pltpu.VMEM((LANES // FOX_HEAD_DIM, tq, LANES), F32),
                        pltpu.VMEM((LANES // FOX_HEAD_DIM, tq, LANES), F32)],
        compiler_params=_cparams(("parallel", "parallel", "parallel", "arbitrary")),
        name="fox_mixer",
    )(proj, proj, proj, ccol, crow)


def _merge_kernel(alpha, ya_ref, yb_ref, yc_ref, yd_ref, g0_ref, g1_ref, g2_ref, g3_ref, gb_ref,
                  wb_ref, wo_ref, h_ref, lg_ref, lb_ref, o32_ref, o16_ref):
    merged = None
    for i, (y_ref, gl_ref) in enumerate(((ya_ref, g0_ref), (yb_ref, g1_ref), (yc_ref, g2_ref), (yd_ref, g3_ref))):
        gate = jax.nn.sigmoid(gl_ref[...].astype(F32) + gb_ref[i:i + 1, :])
        term = gate * _dot(y_ref[...], wb_ref[i])
        merged = term if merged is None else merged + term
    mix = _dot(merged.astype(BF16), wo_ref[...])
    y = _layer_norm(alpha * h_ref[...] + mix, lg_ref[...], lb_ref[...])
    o32_ref[...] = y
    o16_ref[...] = y.astype(BF16)


def _merge(ys, proj, h32, p, alpha, tm):
    t, d = h32.shape
    w = BRANCH_W
    gate0 = (12 * w) // d
    yspec = pl.BlockSpec((tm, w), lambda i: (i, 0))
    gspec = lambda n: pl.BlockSpec((tm, d), lambda i, n=n: (i, gate0 + n))
    row = pl.BlockSpec((tm, d), lambda i: (i, 0))
    return pl.pallas_call(
        functools.partial(_merge_kernel, alpha),
        grid=(t // tm,),
        in_specs=[yspec] * 4 + [gspec(n) for n in range(N_BRANCH)] +
                 [_const_spec((N_BRANCH, d)), _const_spec((N_BRANCH, w, d)), _const_spec((d, d)), row,
                  _const_spec((1, d)), _const_spec((1, d))],
        out_specs=[row, row],
        out_shape=[jax.ShapeDtypeStruct((t, d), F32), jax.ShapeDtypeStruct((t, d), BF16)],
        compiler_params=_cparams(("parallel",)),
        name="merge_norm",
    )(*ys, proj, proj, proj, proj, p["gate_b"], p["w_branch"], p["w_out"], h32, p["ln1_g"], p["ln1_b"])


def _ffn_kernel(alpha, h16_ref, h32_ref, wu_ref, wd_ref, lg_ref, lb_ref, o32_ref, o16_ref, acc_ref):
    f = pl.program_id(1)

    @pl.when(f == 0)
    def _():
        acc_ref[...] = jnp.zeros_like(acc_ref)

    up = jnp.maximum(_dot(h16_ref[...], wu_ref[...]), 0.0)
    acc_ref[...] += _dot((up * up).astype(BF16), wd_ref[...])

    @pl.when(f == pl.num_programs(1) - 1)
    def _():
        y = _layer_norm(alpha * h32_ref[...] + acc_ref[...], lg_ref[...], lb_ref[...])
        o32_ref[...] = y
        o16_ref[...] = y.astype(BF16)


def _ffn(h16, h32, p, alpha, tm, tf):
    t, d = h32.shape
    dff = p["w_up"].shape[1]
    row = pl.BlockSpec((tm, d), lambda i, f: (i, 0))
    return pl.pallas_call(
        functools.partial(_ffn_kernel, alpha),
        grid=(t // tm, dff // tf),
        in_specs=[row, row, pl.BlockSpec((d, tf), lambda i, f: (0, f)), pl.BlockSpec((tf, d), lambda i, f: (f, 0)),
                  _const_spec((1, d)), _const_spec((1, d))],
        out_specs=[row, row],
        out_shape=[jax.ShapeDtypeStruct((t, d), F32), jax.ShapeDtypeStruct((t, d), BF16)],
        scratch_shapes=[pltpu.VMEM((tm, d), F32)],
        compiler_params=_cparams(("parallel", "arbitrary")),
        name="ffn_norm",
    )(h16, h32, p["w_up"], p["w_down"], p["ln2_g"], p["ln2_b"])


def _pad_lanes(v, start):
    return jnp.zeros((1, LANES), F32).at[0, start:start + v.shape[0]].set(v.astype(F32))


def _expander(start, heads, width):
    m = np.zeros((LANES, heads * width), np.float32)
    for h in range(heads):
        m[start + h, h * width:(h + 1) * width] = 1.0
    return jnp.asarray(m, BF16)


def _tri(n):
    return jnp.asarray(np.tril(np.ones((n, n), np.float32)), BF16)


def _layer_params(l, a, tc):
    w = BRANCH_W
    w_in = a["w_in"][l]
    o_dt = w + SSD_XBC
    o_qkv = o_dt + SSD_HEADS
    o_beta = o_qkv + 3 * w
    o_a = o_beta + DN_HEADS
    o_gate = o_a + DN_HEADS
    o_f = o_gate + w + 2 * w + 3 * w
    o_gates = o_f + FOX_HEADS
    w_big = jnp.concatenate([w_in[:, :o_dt], w_in[:, o_qkv:o_beta], w_in[:, o_gate:o_f], w_in[:, o_gates:]],
                            axis=1).astype(BF16)
    w_small = jnp.zeros((w_in.shape[0], LANES), F32)
    w_small = w_small.at[:, SM_DT:SM_DT + SSD_HEADS].set(w_in[:, o_dt:o_qkv])
    w_small = w_small.at[:, SM_BETA:SM_BETA + DN_HEADS].set(w_in[:, o_beta:o_a])
    w_small = w_small.at[:, SM_A:SM_A + DN_HEADS].set(w_in[:, o_a:o_gate])
    w_small = w_small.at[:, SM_F:SM_F + FOX_HEADS].set(w_in[:, o_f:o_gates])
    scw, dcw = a["ssd_conv_w"][l], a["dn_conv_w"][l]
    scb = a["ssd_conv_b"][l]
    return {
        "w_big": w_big, "w_small": w_small.astype(BF16),
        "ssd_cwx": scw[:, :w], "ssd_cwbc": scw[:, w:], "ssd_cbx": scb[None, :w], "ssd_cbbc": scb[None, w:],
        "ssd_dtb": _pad_lanes(a["ssd_dt_bias"][l], SM_DT), "ssd_alog": _pad_lanes(a["ssd_a_log"][l], SM_DT),
        "ssd_dskip": jnp.repeat(a["ssd_d"][l], SSD_HEAD_DIM)[None, :], "ssd_nw": a["ssd_norm_w"][l][None, :],
        "e8": _expander(SM_DT, SSD_HEADS, SSD_HEAD_DIM), "tri_ssd": _tri(SSD_CHUNK),
        "dn_cwq": dcw[:, :w], "dn_cwk": dcw[:, w:2 * w], "dn_cwv": dcw[:, 2 * w:],
        "dn_alog": _pad_lanes(a["dn_a_log"][l], SM_A), "dn_dtb": _pad_lanes(a["dn_dt_bias"][l], SM_A),
        "dn_nw": a["dn_norm_w"][l][None, :],
        "eb": _expander(SM_BETA, DN_HEADS, DN_HEAD_DIM), "eg": _expander(SM_A, DN_HEADS, DN_HEAD_DIM),
        "tri_dn": _tri(DN_CHUNK),
        "sg_g": a["sg_ln_g"][l][None, :], "sg_b": a["sg_ln_b"][l][None, :], "sg_w": a["sg_w"][l],
        "sg_bs": jnp.repeat(a["sg_b"][l].T, BRANCH_W // SG_GROUPS, axis=1),
        "fox_fb": _pad_lanes(a["fox_f_bias"][l], SM_F), "tri_fox": _tri(tc),
        "gate_b": a["gate_b"][l], "w_branch": a["w_branch"][l].astype(BF16), "w_out": a["w_out"][l].astype(BF16),
        "ln1_g": a["ln1_g"][l][None, :], "ln1_b": a["ln1_b"][l][None, :],
        "w_up": a["w_up"][l].astype(BF16), "w_down": a["w_down"][l].astype(BF16),
        "ln2_g": a["ln2_g"][l][None, :], "ln2_b": a["ln2_b"][l][None, :],
    }


def _tiles(batch, seq):
    t = batch * seq
    pick = lambda n, cap: math.gcd(n, cap)
    return {
        "ln": pick(t, 512), "proj_m": pick(t, 1024), "proj_n": 1024, "ssd": pick(seq, 256), "dn": pick(seq, 256),
        "sg": pick(t, 512), "fcum": pick(seq, 512), "fox": pick(seq, 512), "merge": pick(t, 512),
        "ffn_m": pick(t, 1024), "ffn_f": 1024,
    }


def kernel(x, ln_in_g, ln_in_b, w_in, ssd_conv_w, ssd_conv_b, ssd_dt_bias, ssd_a_log, ssd_d, ssd_norm_w, dn_conv_w,
           dn_a_log, dn_dt_bias, dn_norm_w, sg_ln_g, sg_ln_b, sg_w, sg_b, fox_f_bias, gate_b, w_branch, w_out,
           ln1_g, ln1_b, w_up, w_down, ln2_g, ln2_b):
    batch, seq, d = x.shape
    depth = w_in.shape[0]
    alpha = (2 * depth) ** 0.25
    a = dict(w_in=w_in, ssd_conv_w=ssd_conv_w, ssd_conv_b=ssd_conv_b, ssd_dt_bias=ssd_dt_bias, ssd_a_log=ssd_a_log,
             ssd_d=ssd_d, ssd_norm_w=ssd_norm_w, dn_conv_w=dn_conv_w, dn_a_log=dn_a_log, dn_dt_bias=dn_dt_bias,
             dn_norm_w=dn_norm_w, sg_ln_g=sg_ln_g, sg_ln_b=sg_ln_b, sg_w=sg_w, sg_b=sg_b, fox_f_bias=fox_f_bias,
             gate_b=gate_b, w_branch=w_branch, w_out=w_out, ln1_g=ln1_g, ln1_b=ln1_b, w_up=w_up, w_down=w_down,
             ln2_g=ln2_g, ln2_b=ln2_b)
    tl = _tiles(batch, seq)
    t = batch * seq
    h32, h16 = _entry_norm(x.reshape(t, d), ln_in_g, ln_in_b, tl["ln"])
    for l in range(depth):
        p = _layer_params(l, a, tl["fcum"])
        proj, small = _in_proj(h16, p["w_big"], p["w_small"], tl["proj_m"], tl["proj_n"])
        y_a = _ssd_mixer(proj, small, p, batch, seq, tl["ssd"])
        y_b = _dn_mixer(proj, small, p, batch, seq, tl["dn"])
        y_c = _sg_mixer(proj, p, t, tl["sg"])
        ccol, crow = _forget_cumsum(small, p, batch, seq, tl["fcum"])
        y_d = _fox_mixer(proj, ccol, crow, batch, seq, tl["fox"])
        h32, h16 = _merge((y_a, y_b, y_c, y_d), proj, h32, p, alpha, tl["merge"])
        h32, h16 = _ffn(h16, h32, p, alpha, tl["ffn_m"], tl["ffn_f"])
    return h32.reshape(batch, seq, d)
```

```python
import functools
import math

import numpy as np
import jax
import jax.numpy as jnp
from jax import lax
from jax.experimental import pallas as pl
from jax.experimental.pallas import tpu as pltpu

F32 = jnp.float32
BF16 = jnp.bfloat16

BRANCH_W = 512
N_BRANCH = 4
SSD_HEADS, SSD_HEAD_DIM, SSD_GROUPS, SSD_STATE, SSD_CONV = 8, 64, 2, 128, 4
SSD_XBC = BRANCH_W + 2 * SSD_GROUPS * SSD_STATE
DN_HEADS, DN_HEAD_DIM, DN_CONV = 4, 128, 4
SG_GROUPS, SG_CHUNK = 4, 128
FOX_HEADS, FOX_HEAD_DIM = 8, 64
LN_EPS = 1e-5
NORM_EPS = 1e-6
NEG_BIG = -1e30

LANES = 128
SUBLANES = 8
VMEM_LIMIT_BYTES = 56 * 1024 * 1024

SM_DT, SM_BETA, SM_A, SM_F = 0, 8, 12, 16

SSD_CHUNK = 128
DN_CHUNK = 64


def _dot(a, b):
    return jnp.dot(a, b, preferred_element_type=F32)


def _dot_nt(a, b):
    return lax.dot_general(a, b, (((1,), (1,)), ((), ())), preferred_element_type=F32)


def _dot_tn(a, b):
    return lax.dot_general(a, b, (((0,), (0,)), ((), ())), preferred_element_type=F32)


def _bdot(a, b):
    return _dot(a.astype(BF16), b.astype(BF16))


def _split_terms(x, n):
    terms, r = [], x
    for i in range(n):
        p = r.astype(BF16)
        terms.append(p)
        if i + 1 < n:
            r = r - p.astype(F32)
    return terms


def _sel_right(x, m, n):
    return sum(_dot(p, m) for p in _split_terms(x, n))


def _sel_left(m, x, n):
    return sum(_dot(m, p) for p in _split_terms(x, n))


def _silu(x):
    return x * jax.nn.sigmoid(x)


def _softplus(x):
    return jnp.maximum(x, 0.0) + jnp.log1p(jnp.exp(-jnp.abs(x)))


def _layer_norm(x, g, b):
    mu = jnp.mean(x, axis=-1, keepdims=True)
    xc = x - mu
    var = jnp.mean(xc * xc, axis=-1, keepdims=True)
    return xc * lax.rsqrt(var + LN_EPS) * g + b


def _causal_conv(x, tail_ref, w):
    n = x.shape[0]
    k = w.shape[0]
    tail = tail_ref[...]
    row8 = lax.broadcasted_iota(jnp.int32, (SUBLANES, x.shape[1]), 0)
    acc = x * w[k - 1:k, :]
    for s in range(1, k):
        xr = pltpu.roll(x, s, axis=0)
        pr = pltpu.roll(tail, s, axis=0)
        head = jnp.where(row8 < s, pr, xr[:SUBLANES])
        xs = jnp.concatenate([head, xr[SUBLANES:]], axis=0)
        acc = acc + xs * w[k - 1 - s:k - s, :]
    tail_ref[...] = x[n - SUBLANES:]
    return acc


def _cparams(sem):
    return pltpu.CompilerParams(dimension_semantics=sem, vmem_limit_bytes=VMEM_LIMIT_BYTES)


def _const_spec(shape):
    nd = len(shape)
    return pl.BlockSpec(shape, lambda *_: (0,) * nd)


def _ln_kernel(x_ref, g_ref, b_ref, o32_ref, o16_ref):
    y = _layer_norm(x_ref[...], g_ref[...], b_ref[...])
    o32_ref[...] = y
    o16_ref[...] = y.astype(BF16)


def _entry_norm(x2, g, b, tm):
    t, d = x2.shape
    row = pl.BlockSpec((tm, d), lambda i: (i, 0))
    return pl.pallas_call(
        _ln_kernel,
        grid=(t // tm,),
        in_specs=[row, _const_spec((1, d)), _const_spec((1, d))],
        out_specs=[row, row],
        out_shape=[jax.ShapeDtypeStruct((t, d), F32), jax.ShapeDtypeStruct((t, d), BF16)],
        compiler_params=_cparams(("parallel",)),
        name="entry_norm",
    )(x2, g.reshape(1, d), b.reshape(1, d))


def _proj_kernel(h_ref, w_ref, ws_ref, o_ref, os_ref):
    h = h_ref[...]
    o_ref[...] = _dot(h, w_ref[...]).astype(BF16)

    @pl.when(pl.program_id(1) == 0)
    def _():
        os_ref[...] = _dot(h, ws_ref[...])


def _in_proj(h16, w_big, w_small, tm, tn):
    t, d = h16.shape
    n = w_big.shape[1]
    return pl.pallas_call(
        _proj_kernel,
        grid=(t // tm, n // tn),
        in_specs=[pl.BlockSpec((tm, d), lambda i, j: (i, 0)),
                  pl.BlockSpec((d, tn), lambda i, j: (0, j)),
                  pl.BlockSpec((d, LANES), lambda i, j: (0, 0))],
        out_specs=[pl.BlockSpec((tm, tn), lambda i, j: (i, j)),
                   pl.BlockSpec((tm, LANES), lambda i, j: (i, 0))],
        out_shape=[jax.ShapeDtypeStruct((t, n), BF16), jax.ShapeDtypeStruct((t, LANES), F32)],
        compiler_params=_cparams(("parallel", "arbitrary")),
        name="in_proj",
    )(h16, w_big, w_small)


def _ssd_kernel(z_ref, x_ref, bc_ref, sm_ref, cwx_ref, cwbc_ref, cbx_ref, cbbc_ref, dtb_ref,
                alog_ref, dskip_ref, nw_ref, e8_ref, tri_ref, o_ref, state_ref, xtail_ref, bctail_ref):
    first = pl.program_id(1) == 0
    ts = x_ref.shape[0]
    q = SSD_CHUNK
    hp = SSD_HEADS // SSD_GROUPS * SSD_HEAD_DIM
    e8 = e8_ref[...]
    tri = tri_ref[...]

    @pl.when(first)
    def _():
        state_ref[...] = jnp.zeros_like(state_ref)
        xtail_ref[...] = jnp.zeros_like(xtail_ref)
        bctail_ref[...] = jnp.zeros_like(bctail_ref)

    xs = _silu(_causal_conv(x_ref[...].astype(F32), xtail_ref, cwx_ref[...]) + cbx_ref[...])
    bcm = _silu(_causal_conv(bc_ref[...].astype(F32), bctail_ref, cwbc_ref[...]) + cbbc_ref[...])
    lane = lax.broadcasted_iota(jnp.int32, (1, LANES), 1)
    head_lane = (lane >= SM_DT) & (lane < SM_DT + SSD_HEADS)
    a_neg = jnp.where(head_lane, -jnp.exp(alog_ref[...]), 0.0)
    dt = _softplus(sm_ref[...] + dtb_ref[...])
    dta = dt * a_neg
    xdt = xs * _sel_right(dt, e8, 2)
    ri = lax.broadcasted_iota(jnp.int32, (q, q), 0)
    ci = lax.broadcasted_iota(jnp.int32, (q, q), 1)
    lower = ci <= ri
    lane_g = lax.broadcasted_iota(jnp.int32, (1, hp), 1) // SSD_HEAD_DIM

    for c in range(ts // q):
        sl = slice(c * q, (c + 1) * q)
        acs = _sel_left(tri, dta[sl], 3)
        acs_t = acs.T
        eacs = _sel_right(jnp.exp(acs), e8, 2)
        dec_end = _sel_right(jnp.exp(acs[q - 1:q, :] - acs), e8, 2)
        ys = []
        for g in range(SSD_GROUPS):
            gl = slice(g * hp, (g + 1) * hp)
            bm = bcm[sl, g * SSD_STATE:(g + 1) * SSD_STATE].astype(BF16)
            cm = bcm[sl, (SSD_GROUPS + g) * SSD_STATE:(SSD_GROUPS + g + 1) * SSD_STATE].astype(BF16)
            cb = _dot_nt(cm, bm)
            xg = xdt[sl, gl]
            s_prev = state_ref[g]
            y = _dot(cm, s_prev.astype(BF16)) * eacs[:, gl]
            for e in range(SSD_HEADS // SSD_GROUPS):
                h = SM_DT + g * (SSD_HEADS // SSD_GROUPS) + e
                seg = jnp.exp(jnp.where(lower, acs[:, h:h + 1] - acs_t[h:h + 1, :], NEG_BIG))
                xe = jnp.where(lane_g == e, xg, 0.0).astype(BF16)
                y = y + _dot((cb * seg).astype(BF16), xe)
            state_ref[g] = s_prev * eacs[q - 1:q, gl] + _dot_tn(bm, (xg * dec_end[:, gl]).astype(BF16))
            ys.append(y)
        y = jnp.concatenate(ys, axis=1) + dskip_ref[...] * xs[sl]
        y = y * _silu(z_ref[sl, :].astype(F32))
        y = y * lax.rsqrt(jnp.mean(y * y, axis=-1, keepdims=True) + NORM_EPS) * nw_ref[...]
        o_ref[sl, :] = y.astype(BF16)


def _ssd_mixer(proj, small, p, batch, seq, ts):
    nb = seq // ts
    w = BRANCH_W
    rows = lambda col: pl.BlockSpec((ts, w), lambda b, s, col=col: (b * nb + s, col))
    hp = SSD_HEADS // SSD_GROUPS * SSD_HEAD_DIM
    return pl.pallas_call(
        _ssd_kernel,
        grid=(batch, nb),
        in_specs=[rows(0), rows(1), rows(2),
                  pl.BlockSpec((ts, LANES), lambda b, s: (b * nb + s, 0)),
                  _const_spec((SSD_CONV, w)), _const_spec((SSD_CONV, w)),
                  _const_spec((1, w)), _const_spec((1, w)),
                  _const_spec((1, LANES)), _const_spec((1, LANES)),
                  _const_spec((1, w)), _const_spec((1, w)),
                  _const_spec((LANES, w)), _const_spec((SSD_CHUNK, SSD_CHUNK))],
        out_specs=pl.BlockSpec((ts, w), lambda b, s: (b * nb + s, 0)),
        out_shape=jax.ShapeDtypeStruct((batch * seq, w), BF16),
        scratch_shapes=[pltpu.VMEM((SSD_GROUPS, SSD_STATE, hp), F32),
                        pltpu.VMEM((SUBLANES, w), F32), pltpu.VMEM((SUBLANES, w), F32)],
        compiler_params=_cparams(("parallel", "arbitrary")),
        name="ssd_mixer",
    )(proj, proj, proj, small, p["ssd_cwx"], p["ssd_cwbc"], p["ssd_cbx"], p["ssd_cbbc"], p["ssd_dtb"],
      p["ssd_alog"], p["ssd_dskip"], p["ssd_nw"], p["e8"], p["tri_ssd"])


def _unit_lower_inverses(mats, c):
    n = mats[0].shape[0]
    ri = lax.broadcasted_iota(jnp.int32, (n, n), 0)
    ci = lax.broadcasted_iota(jnp.int32, (n, n), 1)
    eye = jnp.where(ri == ci, 1.0, 0.0).astype(F32)
    pws = [(-a).astype(BF16) for a in mats]
    invs = [eye - a for a in mats]
    for _ in range(int(math.log2(c)) - 1):
        sq = [_dot(pw, pw) for pw in pws]
        pws = [s.astype(BF16) for s in sq]
        invs = [inv + _dot(inv.astype(BF16), pw) for inv, pw in zip(invs, pws)]
    return invs


def _dn_kernel(q_ref, k_ref, v_ref, gate_ref, sm_ref, cwq_ref, cwk_ref, cwv_ref, alog_ref, dtb_ref,
               nw_ref, eb_ref, eg_ref, tri_ref, ones_ref, o_ref, state_ref, qtail_ref, ktail_ref, vtail_ref):
    first = pl.program_id(1) == 0
    ts = q_ref.shape[0]
    c = DN_CHUNK
    dk = DN_HEAD_DIM

    @pl.when(first)
    def _():
        state_ref[...] = jnp.zeros_like(state_ref)
        qtail_ref[...] = jnp.zeros_like(qtail_ref)
        ktail_ref[...] = jnp.zeros_like(ktail_ref)
        vtail_ref[...] = jnp.zeros_like(vtail_ref)

    qf = _silu(_causal_conv(q_ref[...].astype(F32), qtail_ref, cwq_ref[...]))
    kf = _silu(_causal_conv(k_ref[...].astype(F32), ktail_ref, cwk_ref[...]))
    vf = _silu(_causal_conv(v_ref[...].astype(F32), vtail_ref, cwv_ref[...]))
    sm = sm_ref[...]
    lane = lax.broadcasted_iota(jnp.int32, (1, LANES), 1)
    a_lane = (lane >= SM_A) & (lane < SM_A + DN_HEADS)
    neg_a = jnp.where(a_lane, -jnp.exp(alog_ref[...]), 0.0)
    gdec = neg_a * _softplus(sm + dtb_ref[...])
    beta = _sel_right(jax.nn.sigmoid(sm), eb_ref[...], 2)
    eg = eg_ref[...]
    gcs = _sel_left(tri_ref[...], gdec, 3)
    gtot = _sel_left(ones_ref[...], gdec, 3)
    gcs_t = gcs.T
    egcs = _sel_right(jnp.exp(gcs), eg, 2)
    edec = _sel_right(jnp.exp(gtot - gcs), eg, 2)
    elast = _sel_right(jnp.exp(gtot), eg, 2)
    ri = lax.broadcasted_iota(jnp.int32, (ts, ts), 0)
    ci = lax.broadcasted_iota(jnp.int32, (ts, ts), 1)
    lower = (ci >= ri // c * c) & (ci <= ri)
    offdiag = ci != ri

    a_mats, rhs, qk, qg, kd = [], [], [], [], []
    for h in range(DN_HEADS):
        hl = slice(h * dk, (h + 1) * dk)
        gl = SM_A + h
        qh, kh = qf[:, hl], kf[:, hl]
        q = qh * lax.rsqrt(jnp.sum(qh * qh, axis=-1, keepdims=True) + NORM_EPS) * (dk ** -0.5)
        k = kh * lax.rsqrt(jnp.sum(kh * kh, axis=-1, keepdims=True) + NORM_EPS)
        bh = beta[:, hl]
        gamma = jnp.exp(jnp.where(lower, gcs[:, gl:gl + 1] - gcs_t[gl:gl + 1, :], NEG_BIG))
        kb = k * bh
        k16 = k.astype(BF16)
        a_mats.append(jnp.where(offdiag, _dot_nt(kb.astype(BF16), k16) * gamma, 0.0))
        rhs.append(jnp.concatenate([kb * egcs[:, hl], vf[:, hl] * bh], axis=1).astype(BF16))
        qk.append((_dot_nt(q.astype(BF16), k16) * gamma).astype(BF16))
        qg.append((q * egcs[:, hl]).astype(BF16))
        kd.append((k * edec[:, hl]).astype(BF16))
    t_invs = _unit_lower_inverses(a_mats, c)
    wu = [_dot(t.astype(BF16), r) for t, r in zip(t_invs, rhs)]

    states = [state_ref[h] for h in range(DN_HEADS)]
    v_new = [[] for _ in range(DN_HEADS)]
    o_state = [[] for _ in range(DN_HEADS)]
    for ch in range(ts // c):
        sl = slice(ch * c, (ch + 1) * c)
        for h in range(DN_HEADS):
            hl = slice(h * dk, (h + 1) * dk)
            lhs = jnp.concatenate([wu[h][sl, :dk].astype(BF16), qg[h][sl]], axis=0)
            ws = _dot(lhs, states[h].astype(BF16))
            vn = wu[h][sl, dk:] - ws[:c]
            v_new[h].append(vn)
            o_state[h].append(ws[c:])
            states[h] = states[h] * elast[ch * c:ch * c + 1, hl] + _dot_tn(kd[h][sl], vn.astype(BF16))

    for h in range(DN_HEADS):
        hl = slice(h * dk, (h + 1) * dk)
        state_ref[h] = states[h]
        vn_all = jnp.concatenate(v_new[h], axis=0).astype(BF16)
        o = jnp.concatenate(o_state[h], axis=0) + _dot(qk[h], vn_all)
        o = o * lax.rsqrt(jnp.mean(o * o, axis=-1, keepdims=True) + NORM_EPS) * nw_ref[...]
        o_ref[:, hl] = (o * _silu(gate_ref[:, hl].astype(F32))).astype(BF16)


def _dn_mixer(proj, small, p, batch, seq, ts):
    nb = seq // ts
    w = BRANCH_W
    rows = lambda col: pl.BlockSpec((ts, w), lambda b, s, col=col: (b * nb + s, col))
    return pl.pallas_call(
        _dn_kernel,
        grid=(batch, nb),
        in_specs=[rows(3), rows(4), rows(5), rows(6),
                  pl.BlockSpec((ts, LANES), lambda b, s: (b * nb + s, 0)),
                  _const_spec((DN_CONV, w)), _const_spec((DN_CONV, w)), _const_spec((DN_CONV, w)),
                  _const_spec((1, LANES)), _const_spec((1, LANES)), _const_spec((1, DN_HEAD_DIM)),
                  _const_spec((LANES, w)), _const_spec((LANES, w)), _const_spec((ts, ts)), _const_spec((ts, ts))],
        out_specs=pl.BlockSpec((ts, w), lambda b, s: (b * nb + s, 0)),
        out_shape=jax.ShapeDtypeStruct((batch * seq, w), BF16),
        scratch_shapes=[pltpu.VMEM((DN_HEADS, DN_HEAD_DIM, DN_HEAD_DIM), F32),
                        pltpu.VMEM((SUBLANES, w), F32), pltpu.VMEM((SUBLANES, w), F32),
                        pltpu.VMEM((SUBLANES, w), F32)],
        compiler_params=_cparams(("parallel", "arbitrary")),
        name="dn_mixer",
    )(proj, proj, proj, proj, small, p["dn_cwq"], p["dn_cwk"], p["dn_cwv"], p["dn_alog"], p["dn_dtb"],
      p["dn_nw"], p["eb"], p["eg"], _block_diag(ts, DN_CHUNK, True), _block_diag(ts, DN_CHUNK, False))


def _sg_kernel(u_ref, v_ref, g_ref, b_ref, w_ref, bs_ref, o_ref):
    ts = u_ref.shape[0]
    q = SG_CHUNK
    gd = BRANCH_W // SG_GROUPS
    u = jax.nn.gelu(u_ref[...].astype(F32))
    v = _layer_norm(jax.nn.gelu(v_ref[...].astype(F32)), g_ref[...], b_ref[...])
    ri = lax.broadcasted_iota(jnp.int32, (q, q), 0)
    ci = lax.broadcasted_iota(jnp.int32, (q, q), 1)
    lower = ci <= ri
    for g in range(SG_GROUPS):
        gl = slice(g * gd, (g + 1) * gd)
        wg = jnp.where(lower, w_ref[g], 0.0).astype(BF16)
        for c in range(ts // q):
            sl = slice(c * q, (c + 1) * q)
            mixed = _dot(wg, v[sl, gl].astype(BF16)) + bs_ref[:, gl]
            o_ref[sl, gl] = (u[sl, gl] * mixed).astype(BF16)


def _sg_mixer(proj, p, t, ts):
    w = BRANCH_W
    rows = lambda col: pl.BlockSpec((ts, w), lambda i, col=col: (i, col))
    return pl.pallas_call(
        _sg_kernel,
        grid=(t // ts,),
        in_specs=[rows(7), rows(8), _const_spec((1, w)), _const_spec((1, w)),
                  _const_spec((SG_GROUPS, SG_CHUNK, SG_CHUNK)), _const_spec((SG_CHUNK, w))],
        out_specs=pl.BlockSpec((ts, w), lambda i: (i, 0)),
        out_shape=jax.ShapeDtypeStruct((t, w), BF16),
        compiler_params=_cparams(("parallel",)),
        name="sg_mixer",
    )(proj, proj, p["sg_g"], p["sg_b"], p["sg_w"], p["sg_bs"])


FOX_SPLIT = 3
LOG2E = math.log2(math.e)


def _fcum_kernel(sm_ref, fb_ref, tri_ref, eq_ref, ek_ref, oq_ref, ok_ref, qa_ref, ka_ref, carry_ref):
    @pl.when(pl.program_id(1) == 0)
    def _():
        carry_ref[...] = jnp.zeros_like(carry_ref)

    n = sm_ref.shape[0]
    ls = jax.nn.log_sigmoid(sm_ref[...] + fb_ref[...])
    c = _sel_left(tri_ref[...], ls, 3) + carry_ref[...]
    carry_ref[...] = c[n - 1:n, :]
    terms = _split_terms(c * LOG2E, FOX_SPLIT)
    cq = sum(_dot(t, eq_ref[i]) for i, t in enumerate(terms))
    ck = sum(_dot(t, ek_ref[i]) for i, t in enumerate(terms))
    qa_ref[...] = (oq_ref[...] + cq).astype(BF16)
    ka_ref[...] = (ok_ref[...] - ck).astype(BF16)


def _forget_cumsum(small, p, batch, seq, tc):
    nb = seq // tc
    wa = FOX_HEADS * LANES
    rows = pl.BlockSpec((tc, wa), lambda b, s: (b * nb + s, 0))
    return pl.pallas_call(
        _fcum_kernel,
        grid=(batch, nb),
        in_specs=[pl.BlockSpec((tc, LANES), lambda b, s: (b * nb + s, 0)),
                  _const_spec((1, LANES)), _const_spec((tc, tc)),
                  _const_spec((FOX_SPLIT, LANES, wa)), _const_spec((FOX_SPLIT, LANES, wa)),
                  _const_spec((1, wa)), _const_spec((1, wa))],
        out_specs=[rows, rows],
        out_shape=[jax.ShapeDtypeStruct((batch * seq, wa), BF16), jax.ShapeDtypeStruct((batch * seq, wa), BF16)],
        scratch_shapes=[pltpu.VMEM((1, LANES), F32)],
        compiler_params=_cparams(("parallel", "arbitrary")),
        name="forget_cumsum",
    )(small, p["fox_fb"], p["tri_fox"], p["fox_eq"], p["fox_ek"], p["fox_oq"], p["fox_ok"])


def _fox_kernel(ii_ref, jj_ref, q_ref, k_ref, v_ref, qa_ref, ka_ref, o_ref, qaug_ref, m_ref, acc_ref):
    t = pl.program_id(1)
    i = ii_ref[t]
    j = jj_ref[t]
    tq, tk = q_ref.shape[0], k_ref.shape[0]
    per = LANES // FOX_HEAD_DIM
    lane = lax.broadcasted_iota(jnp.int32, (1, LANES), 1)

    @pl.when(j == 0)
    def _():
        m_ref[...] = jnp.full_like(m_ref, NEG_BIG)
        acc_ref[...] = jnp.zeros_like(acc_ref)
        for h in range(FOX_HEADS):
            bl = slice(h // per * LANES, (h // per + 1) * LANES)
            q2 = (q_ref[:, bl].astype(F32) * (LOG2E * FOX_HEAD_DIM ** -0.5)).astype(BF16)
            qaug_ref[h, :, :LANES] = jnp.where(lane // FOX_HEAD_DIM == h % per, q2, jnp.zeros_like(q2))
            qaug_ref[h, :, LANES:] = qa_ref[:, h * LANES:(h + 1) * LANES]

    def step(diagonal):
        ones = jnp.ones((tk, LANES), BF16)
        if diagonal:
            causal = (lax.broadcasted_iota(jnp.int32, (tq, tk), 1) <= lax.broadcasted_iota(jnp.int32, (tq, tk), 0))
        for h in range(FOX_HEADS):
            bl = slice(h // per * LANES, (h // per + 1) * LANES)
            kaug = jnp.concatenate([k_ref[:, bl], ka_ref[:, h * LANES:(h + 1) * LANES]], axis=1)
            vaug = jnp.concatenate([v_ref[:, bl], ones], axis=1)
            s = _dot_nt(qaug_ref[h], kaug)
            if diagonal:
                s = jnp.where(causal, s, NEG_BIG)
            m_old = m_ref[h]
            m_new = jnp.maximum(m_old, jnp.max(s, axis=-1, keepdims=True))
            alpha = jnp.exp2(m_old - m_new)
            p = jnp.exp2(s - m_new[:, :1]).astype(BF16)
            acc_ref[h] = jnp.concatenate([alpha, alpha], axis=1) * acc_ref[h] + _dot(p, vaug)
            m_ref[h] = m_new

    @pl.when(j < i)
    def _():
        step(False)

    @pl.when(j == i)
    def _():
        step(True)
        for b in range(FOX_HEADS // per):
            out = jnp.zeros((tq, LANES), F32)
            for e in range(per):
                a = acc_ref[b * per + e]
                out = jnp.where(lane // FOX_HEAD_DIM == e, a[:, :LANES] / a[:, LANES:], out)
            o_ref[:, b * LANES:(b + 1) * LANES] = out.astype(BF16)


def _fox_mixer(proj, qa, ka, batch, seq, tq):
    nq = seq // tq
    w = BRANCH_W
    wa = FOX_HEADS * LANES
    pairs = [(i, j) for i in range(nq) for j in range(i + 1)]
    ii = jnp.asarray([p[0] for p in pairs], jnp.int32)
    jj = jnp.asarray([p[1] for p in pairs], jnp.int32)
    qrow = lambda b, t, ii, jj: b * nq + ii[t]
    krow = lambda b, t, ii, jj: b * nq + jj[t]
    grid_spec = pltpu.PrefetchScalarGridSpec(
        num_scalar_prefetch=2,
        grid=(batch, len(pairs)),
        in_specs=[pl.BlockSpec((tq, w), lambda b, t, ii, jj: (qrow(b, t, ii, jj), 9)),
                  pl.BlockSpec((tq, w), lambda b, t, ii, jj: (krow(b, t, ii, jj), 10)),
                  pl.BlockSpec((tq, w), lambda b, t, ii, jj: (krow(b, t, ii, jj), 11)),
                  pl.BlockSpec((tq, wa), lambda b, t, ii, jj: (qrow(b, t, ii, jj), 0)),
                  pl.BlockSpec((tq, wa), lambda b, t, ii, jj: (krow(b, t, ii, jj), 0))],
        out_specs=pl.BlockSpec((tq, w), lambda b, t, ii, jj: (qrow(b, t, ii, jj), 0)),
        scratch_shapes=[pltpu.VMEM((FOX_HEADS, tq, 2 * LANES), BF16),
                        pltpu.VMEM((FOX_HEADS, tq, LANES), F32),
                        pltpu.VMEM((FOX_HEADS, tq, 2 * LANES), F32)])
    return pl.pallas_call(
        _fox_kernel,
        grid_spec=grid_spec,
        out_shape=jax.ShapeDtypeStruct((batch * seq, w), BF16),
        compiler_params=_cparams(("parallel", "arbitrary")),
        name="fox_mixer",
    )(ii, jj, proj, proj, proj, qa, ka)


def _merge_kernel(alpha, ya_ref, yb_ref, yc_ref, yd_ref, g0_ref, g1_ref, g2_ref, g3_ref, gb_ref,
                  wb_ref, wo_ref, h_ref, lg_ref, lb_ref, o32_ref, o16_ref):
    merged = None
    for i, (y_ref, gl_ref) in enumerate(((ya_ref, g0_ref), (yb_ref, g1_ref), (yc_ref, g2_ref), (yd_ref, g3_ref))):
        gate = jax.nn.sigmoid(gl_ref[...].astype(F32) + gb_ref[i:i + 1, :])
        term = gate * _dot(y_ref[...], wb_ref[i])
        merged = term if merged is None else merged + term
    mix = _dot(merged.astype(BF16), wo_ref[...])
    y = _layer_norm(alpha * h_ref[...] + mix, lg_ref[...], lb_ref[...])
    o32_ref[...] = y
    o16_ref[...] = y.astype(BF16)


def _merge(ys, proj, h32, p, alpha, tm):
    t, d = h32.shape
    w = BRANCH_W
    gate0 = (12 * w) // d
    yspec = pl.BlockSpec((tm, w), lambda i: (i, 0))
    gspec = lambda n: pl.BlockSpec((tm, d), lambda i, n=n: (i, gate0 + n))
    row = pl.BlockSpec((tm, d), lambda i: (i, 0))
    return pl.pallas_call(
        functools.partial(_merge_kernel, alpha),
        grid=(t // tm,),
        in_specs=[yspec] * 4 + [gspec(n) for n in range(N_BRANCH)] +
                 [_const_spec((N_BRANCH, d)), _const_spec((N_BRANCH, w, d)), _const_spec((d, d)), row,
                  _const_spec((1, d)), _const_spec((1, d))],
        out_specs=[row, row],
        out_shape=[jax.ShapeDtypeStruct((t, d), F32), jax.ShapeDtypeStruct((t, d), BF16)],
        compiler_params=_cparams(("parallel",)),
        name="merge_norm",
    )(*ys, proj, proj, proj, proj, p["gate_b"], p["w_branch"], p["w_out"], h32, p["ln1_g"], p["ln1_b"])


def _ffn_kernel(alpha, h16_ref, h32_ref, wu_ref, wd_ref, lg_ref, lb_ref, o32_ref, o16_ref, acc_ref):
    f = pl.program_id(1)

    @pl.when(f == 0)
    def _():
        acc_ref[...] = jnp.zeros_like(acc_ref)

    up = jnp.maximum(_dot(h16_ref[...], wu_ref[...]), 0.0)
    acc_ref[...] += _dot((up * up).astype(BF16), wd_ref[...])

    @pl.when(f == pl.num_programs(1) - 1)
    def _():
        y = _layer_norm(alpha * h32_ref[...] + acc_ref[...], lg_ref[...], lb_ref[...])
        o32_ref[...] = y
        o16_ref[...] = y.astype(BF16)


def _ffn(h16, h32, p, alpha, tm, tf):
    t, d = h32.shape
    dff = p["w_up"].shape[1]
    row = pl.BlockSpec((tm, d), lambda i, f: (i, 0))
    return pl.pallas_call(
        functools.partial(_ffn_kernel, alpha),
        grid=(t // tm, dff // tf),
        in_specs=[row, row, pl.BlockSpec((d, tf), lambda i, f: (0, f)), pl.BlockSpec((tf, d), lambda i, f: (f, 0)),
                  _const_spec((1, d)), _const_spec((1, d))],
        out_specs=[row, row],
        out_shape=[jax.ShapeDtypeStruct((t, d), F32), jax.ShapeDtypeStruct((t, d), BF16)],
        scratch_shapes=[pltpu.VMEM((tm, d), F32)],
        compiler_params=_cparams(("parallel", "arbitrary")),
        name="ffn_norm",
    )(h16, h32, p["w_up"], p["w_down"], p["ln2_g"], p["ln2_b"])


def _pad_lanes(v, start):
    return jnp.zeros((1, LANES), F32).at[0, start:start + v.shape[0]].set(v.astype(F32))


def _expander(start, heads, width):
    m = np.zeros((LANES, heads * width), np.float32)
    for h in range(heads):
        m[start + h, h * width:(h + 1) * width] = 1.0
    return jnp.asarray(m, BF16)


def _tri(n):
    return jnp.asarray(np.tril(np.ones((n, n), np.float32)), BF16)


def _block_diag(n, c, lower):
    blk = np.tril(np.ones((c, c), np.float32)) if lower else np.ones((c, c), np.float32)
    return jnp.asarray(np.kron(np.eye(n // c, dtype=np.float32), blk), BF16)


def _fox_aug_constants():
    wa = FOX_HEADS * LANES
    eq = np.zeros((FOX_SPLIT, LANES, wa), np.float32)
    ek = np.zeros((FOX_SPLIT, LANES, wa), np.float32)
    oq = np.zeros((1, wa), np.float32)
    ok = np.zeros((1, wa), np.float32)
    for h in range(FOX_HEADS):
        for i in range(FOX_SPLIT):
            ek[i, SM_F + h, h * LANES + i] = 1.0
            eq[i, SM_F + h, h * LANES + FOX_SPLIT + i] = 1.0
            oq[0, h * LANES + i] = 1.0
            ok[0, h * LANES + FOX_SPLIT + i] = 1.0
    return jnp.asarray(eq, BF16), jnp.asarray(ek, BF16), jnp.asarray(oq), jnp.asarray(ok)


def _layer_params(l, a, tc):
    w = BRANCH_W
    w_in = a["w_in"][l]
    o_dt = w + SSD_XBC
    o_qkv = o_dt + SSD_HEADS
    o_beta = o_qkv + 3 * w
    o_a = o_beta + DN_HEADS
    o_gate = o_a + DN_HEADS
    o_f = o_gate + w + 2 * w + 3 * w
    o_gates = o_f + FOX_HEADS
    w_big = jnp.concatenate([w_in[:, :o_dt], w_in[:, o_qkv:o_beta], w_in[:, o_gate:o_f], w_in[:, o_gates:]],
                            axis=1).astype(BF16)
    w_small = jnp.zeros((w_in.shape[0], LANES), F32)
    w_small = w_small.at[:, SM_DT:SM_DT + SSD_HEADS].set(w_in[:, o_dt:o_qkv])
    w_small = w_small.at[:, SM_BETA:SM_BETA + DN_HEADS].set(w_in[:, o_beta:o_a])
    w_small = w_small.at[:, SM_A:SM_A + DN_HEADS].set(w_in[:, o_a:o_gate])
    w_small = w_small.at[:, SM_F:SM_F + FOX_HEADS].set(w_in[:, o_f:o_gates])
    scw, dcw = a["ssd_conv_w"][l], a["dn_conv_w"][l]
    fox_eq, fox_ek, fox_oq, fox_ok = _fox_aug_constants()
    scb = a["ssd_conv_b"][l]
    return {
        "w_big": w_big, "w_small": w_small.astype(BF16),
        "ssd_cwx": scw[:, :w], "ssd_cwbc": scw[:, w:], "ssd_cbx": scb[None, :w], "ssd_cbbc": scb[None, w:],
        "ssd_dtb": _pad_lanes(a["ssd_dt_bias"][l], SM_DT), "ssd_alog": _pad_lanes(a["ssd_a_log"][l], SM_DT),
        "ssd_dskip": jnp.repeat(a["ssd_d"][l], SSD_HEAD_DIM)[None, :], "ssd_nw": a["ssd_norm_w"][l][None, :],
        "e8": _expander(SM_DT, SSD_HEADS, SSD_HEAD_DIM), "tri_ssd": _tri(SSD_CHUNK),
        "dn_cwq": dcw[:, :w], "dn_cwk": dcw[:, w:2 * w], "dn_cwv": dcw[:, 2 * w:],
        "dn_alog": _pad_lanes(a["dn_a_log"][l], SM_A), "dn_dtb": _pad_lanes(a["dn_dt_bias"][l], SM_A),
        "dn_nw": a["dn_norm_w"][l][None, :],
        "eb": _expander(SM_BETA, DN_HEADS, DN_HEAD_DIM), "eg": _expander(SM_A, DN_HEADS, DN_HEAD_DIM),
        "sg_g": a["sg_ln_g"][l][None, :], "sg_b": a["sg_ln_b"][l][None, :], "sg_w": a["sg_w"][l],
        "sg_bs": jnp.repeat(a["sg_b"][l].T, BRANCH_W // SG_GROUPS, axis=1),
        "fox_fb": _pad_lanes(a["fox_f_bias"][l], SM_F), "tri_fox": _tri(tc),
        "fox_eq": fox_eq, "fox_ek": fox_ek, "fox_oq": fox_oq, "fox_ok": fox_ok,
        "gate_b": a["gate_b"][l], "w_branch": a["w_branch"][l].astype(BF16), "w_out": a["w_out"][l].astype(BF16),
        "ln1_g": a["ln1_g"][l][None, :], "ln1_b": a["ln1_b"][l][None, :],
        "w_up": a["w_up"][l].astype(BF16), "w_down": a["w_down"][l].astype(BF16),
        "ln2_g": a["ln2_g"][l][None, :], "ln2_b": a["ln2_b"][l][None, :],
    }


def _tiles(batch, seq):
    t = batch * seq
    pick = lambda n, cap: math.gcd(n, cap)
    return {
        "ln": pick(t, 512), "proj_m": pick(t, 2048), "proj_n": 1024, "ssd": pick(seq, 256), "dn": pick(seq, 256),
        "sg": pick(t, 512), "fcum": pick(seq, 512), "fox": pick(seq, 512), "merge": pick(t, 512),
        "ffn_m": pick(t, 1024), "ffn_f": 1024,
    }


def kernel(x, ln_in_g, ln_in_b, w_in, ssd_conv_w, ssd_conv_b, ssd_dt_bias, ssd_a_log, ssd_d, ssd_norm_w, dn_conv_w,
           dn_a_log, dn_dt_bias, dn_norm_w, sg_ln_g, sg_ln_b, sg_w, sg_b, fox_f_bias, gate_b, w_branch, w_out,
           ln1_g, ln1_b, w_up, w_down, ln2_g, ln2_b):
    batch, seq, d = x.shape
    depth = w_in.shape[0]
    alpha = (2 * depth) ** 0.25
    a = dict(w_in=w_in, ssd_conv_w=ssd_conv_w, ssd_conv_b=ssd_conv_b, ssd_dt_bias=ssd_dt_bias, ssd_a_log=ssd_a_log,
             ssd_d=ssd_d, ssd_norm_w=ssd_norm_w, dn_conv_w=dn_conv_w, dn_a_log=dn_a_log, dn_dt_bias=dn_dt_bias,
             dn_norm_w=dn_norm_w, sg_ln_g=sg_ln_g, sg_ln_b=sg_ln_b, sg_w=sg_w, sg_b=sg_b, fox_f_bias=fox_f_bias,
             gate_b=gate_b, w_branch=w_branch, w_out=w_out, ln1_g=ln1_g, ln1_b=ln1_b, w_up=w_up, w_down=w_down,
             ln2_g=ln2_g, ln2_b=ln2_b)
    tl = _tiles(batch, seq)
    t = batch * seq
    h32, h16 = _entry_norm(x.reshape(t, d), ln_in_g, ln_in_b, tl["ln"])
    for l in range(depth):
        p = _layer_params(l, a, tl["fcum"])
        proj, small = _in_proj(h16, p["w_big"], p["w_small"], tl["proj_m"], tl["proj_n"])
        y_a = _ssd_mixer(proj, small, p, batch, seq, tl["ssd"])
        y_b = _dn_mixer(proj, small, p, batch, seq, tl["dn"])
        y_c = _sg_mixer(proj, p, t, tl["sg"])
        qa, ka = _forget_cumsum(small, p, batch, seq, tl["fcum"])
        y_d = _fox_mixer(proj, qa, ka, batch, seq, tl["fox"])
        h32, h16 = _merge((y_a, y_b, y_c, y_d), proj, h32, p, alpha, tl["merge"])
        h32, h16 = _ffn(h16, h32, p, alpha, tl["ffn_m"], tl["ffn_f"])
    return h32.reshape(batch, seq, d)
```

```python
import functools
import math

import numpy as np
import jax
import jax.numpy as jnp
from jax import lax
from jax.experimental import pallas as pl
from jax.experimental.pallas import tpu as pltpu

F32 = jnp.float32
BF16 = jnp.bfloat16

BRANCH_W = 512
N_BRANCH = 4
SSD_HEADS, SSD_HEAD_DIM, SSD_GROUPS, SSD_STATE, SSD_CONV = 8, 64, 2, 128, 4
SSD_XBC = BRANCH_W + 2 * SSD_GROUPS * SSD_STATE
DN_HEADS, DN_HEAD_DIM, DN_CONV = 4, 128, 4
SG_GROUPS, SG_CHUNK = 4, 128
FOX_HEADS, FOX_HEAD_DIM = 8, 64
LN_EPS = 1e-5
NORM_EPS = 1e-6
NEG_BIG = -1e30

LANES = 128
SUBLANES = 8
VMEM_LIMIT_BYTES = 56 * 1024 * 1024

SM_DT, SM_BETA, SM_A, SM_F = 0, 8, 12, 16

SSD_CHUNK = 128
DN_CHUNK = 64


def _dot(a, b):
    return jnp.dot(a, b, preferred_element_type=F32)


def _dot_nt(a, b):
    return lax.dot_general(a, b, (((1,), (1,)), ((), ())), preferred_element_type=F32)


def _dot_tn(a, b):
    return lax.dot_general(a, b, (((0,), (0,)), ((), ())), preferred_element_type=F32)


def _bdot(a, b):
    return _dot(a.astype(BF16), b.astype(BF16))


def _split_terms(x, n):
    terms, r = [], x
    for i in range(n):
        p = r.astype(BF16)
        terms.append(p)
        if i + 1 < n:
            r = r - p.astype(F32)
    return terms


def _sel_right(x, m, n):
    return sum(_dot(p, m) for p in _split_terms(x, n))


def _sel_left(m, x, n):
    return sum(_dot(m, p) for p in _split_terms(x, n))


def _sigmoid(x):
    return 0.5 * jnp.tanh(0.5 * x) + 0.5


def _silu(x):
    hx = 0.5 * x
    return hx + hx * jnp.tanh(hx)


def _softplus(x):
    return jnp.maximum(x, 0.0) + jnp.log1p(jnp.exp(-jnp.abs(x)))


def _layer_norm(x, g, b):
    mu = jnp.mean(x, axis=-1, keepdims=True)
    xc = x - mu
    var = jnp.mean(xc * xc, axis=-1, keepdims=True)
    return xc * lax.rsqrt(var + LN_EPS) * g + b


def _causal_conv(x, tail, w):
    n = x.shape[0]
    k = w.shape[0]
    row8 = lax.broadcasted_iota(jnp.int32, (SUBLANES, x.shape[1]), 0)
    acc = x * w[k - 1:k, :]
    for s in range(1, k):
        xr = pltpu.roll(x, s, axis=0)
        pr = pltpu.roll(tail, s, axis=0)
        head = jnp.where(row8 < s, pr, xr[:SUBLANES])
        xs = jnp.concatenate([head, xr[SUBLANES:]], axis=0)
        acc = acc + xs * w[k - 1 - s:k - s, :]
    return acc, x[n - SUBLANES:]


def _cparams(sem):
    return pltpu.CompilerParams(dimension_semantics=sem, vmem_limit_bytes=VMEM_LIMIT_BYTES)


def _const_spec(shape):
    nd = len(shape)
    return pl.BlockSpec(shape, lambda *_: (0,) * nd)


def _ln_kernel(x_ref, g_ref, b_ref, o32_ref, o16_ref):
    y = _layer_norm(x_ref[...], g_ref[...], b_ref[...])
    o32_ref[...] = y
    o16_ref[...] = y.astype(BF16)


def _entry_norm(x2, g, b, tm):
    t, d = x2.shape
    row = pl.BlockSpec((tm, d), lambda i: (i, 0))
    return pl.pallas_call(
        _ln_kernel,
        grid=(t // tm,),
        in_specs=[row, _const_spec((1, d)), _const_spec((1, d))],
        out_specs=[row, row],
        out_shape=[jax.ShapeDtypeStruct((t, d), F32), jax.ShapeDtypeStruct((t, d), BF16)],
        compiler_params=_cparams(("parallel",)),
        name="entry_norm",
    )(x2, g.reshape(1, d), b.reshape(1, d))


COL_Z, COL_X, COL_BC, COL_DQ, COL_DK, COL_DV, COL_DG, COL_SU, COL_SV, COL_FQ, COL_FK, COL_FV, COL_GATE = range(13)
CONV_TAPS = 4
PROJ_ROW_BLOCK = 256


def _proj_kernel(blocks_per_seq, h_ref, w_ref, ws_ref, cw_ref, bias_ref, o_ref, os_ref, tail_ref):
    i = pl.program_id(0)
    j = pl.program_id(1)
    tm = h_ref.shape[0]
    rb = math.gcd(tm, PROJ_ROW_BLOCK)

    def emit(fn, carry=None):
        n = tm // rb
        y_next = _dot(h_ref[0:rb, :], w_ref[...])
        for r in range(n):
            rows = slice(r * rb, (r + 1) * rb)
            y = y_next
            if r + 1 < n:
                y_next = _dot(h_ref[(r + 1) * rb:(r + 2) * rb, :], w_ref[...])
            if carry is None:
                y = fn(y)
            else:
                y, carry = fn(y, carry)
            o_ref[rows, :] = y.astype(BF16)
        return carry

    @pl.when(j == COL_Z)
    def _():
        emit(_silu)
        os_ref[...] = _dot(h_ref[...], ws_ref[...])

    @pl.when((j >= COL_X) & (j <= COL_DV))
    def _():
        tail = tail_ref.at[j]

        @pl.when(i % blocks_per_seq == 0)
        def _():
            tail[...] = jnp.zeros_like(tail)

        def conv_silu(x, prev):
            y, last = _causal_conv(x, prev, cw_ref[...])
            return _silu(y + bias_ref[...]), last

        tail[...] = emit(conv_silu, tail[...])

    @pl.when(j == COL_DG)
    def _():
        emit(_silu)

    @pl.when((j >= COL_SU) & (j <= COL_SV))
    def _():
        emit(jax.nn.gelu)

    @pl.when((j >= COL_FQ) & (j <= COL_FV))
    def _():
        emit(lambda x: x)

    @pl.when(j >= COL_GATE)
    def _():
        emit(lambda x: _sigmoid(x + bias_ref[...]))


def _in_proj(h16, w_big, w_small, conv_w, bias, seq, tm):
    t, d = h16.shape
    n = w_big.shape[1]
    tn = BRANCH_W
    assert seq % tm == 0
    return pl.pallas_call(
        functools.partial(_proj_kernel, seq // tm),
        grid=(t // tm, n // tn),
        in_specs=[pl.BlockSpec((tm, d), lambda i, j: (i, 0)),
                  pl.BlockSpec((d, tn), lambda i, j: (0, j)),
                  pl.BlockSpec((d, LANES), lambda i, j: (0, 0)),
                  pl.BlockSpec((CONV_TAPS, tn), lambda i, j: (0, j)),
                  pl.BlockSpec((1, tn), lambda i, j: (0, j))],
        out_specs=[pl.BlockSpec((tm, tn), lambda i, j: (i, j)),
                   pl.BlockSpec((tm, LANES), lambda i, j: (i, 0))],
        out_shape=[jax.ShapeDtypeStruct((t, n), BF16), jax.ShapeDtypeStruct((t, LANES), F32)],
        scratch_shapes=[pltpu.VMEM((n // tn, SUBLANES, tn), F32)],
        compiler_params=_cparams(("arbitrary", "arbitrary")),
        name="in_proj",
    )(h16, w_big, w_small, conv_w, bias)


def _ssd_kernel(z_ref, x_ref, bc_ref, sm_ref, dtb_ref, alog_ref, dskip_ref, nw_ref, e8_ref, tri_ref,
                o_ref, state_ref):
    first = pl.program_id(1) == 0
    ts = x_ref.shape[0]
    q = SSD_CHUNK
    hp = SSD_HEADS // SSD_GROUPS * SSD_HEAD_DIM
    e8 = e8_ref[...]
    tri = tri_ref[...]

    @pl.when(first)
    def _():
        state_ref[...] = jnp.zeros_like(state_ref)

    xs = x_ref[...].astype(F32)
    bcm = bc_ref[...]
    lane = lax.broadcasted_iota(jnp.int32, (1, LANES), 1)
    head_lane = (lane >= SM_DT) & (lane < SM_DT + SSD_HEADS)
    a_neg = jnp.where(head_lane, -jnp.exp(alog_ref[...]), 0.0)
    dt = _softplus(sm_ref[...] + dtb_ref[...])
    dta = dt * a_neg
    xdt = xs * _sel_right(dt, e8, 2)
    ri = lax.broadcasted_iota(jnp.int32, (q, q), 0)
    ci = lax.broadcasted_iota(jnp.int32, (q, q), 1)
    lower = ci <= ri
    lane_g = lax.broadcasted_iota(jnp.int32, (1, hp), 1) // SSD_HEAD_DIM

    for c in range(ts // q):
        sl = slice(c * q, (c + 1) * q)
        acs = _sel_left(tri, dta[sl], 3)
        acs_t = acs.T
        eacs = _sel_right(jnp.exp(acs), e8, 2)
        dec_end = _sel_right(jnp.exp(acs[q - 1:q, :] - acs), e8, 2)
        ys = []
        for g in range(SSD_GROUPS):
            gl = slice(g * hp, (g + 1) * hp)
            bm = bcm[sl, g * SSD_STATE:(g + 1) * SSD_STATE]
            cm = bcm[sl, (SSD_GROUPS + g) * SSD_STATE:(SSD_GROUPS + g + 1) * SSD_STATE]
            cb = _dot_nt(cm, bm)
            xg = xdt[sl, gl]
            s_prev = state_ref[g]
            y = _dot(cm, s_prev.astype(BF16)) * eacs[:, gl]
            for e in range(SSD_HEADS // SSD_GROUPS):
                h = SM_DT + g * (SSD_HEADS // SSD_GROUPS) + e
                seg = jnp.exp(jnp.where(lower, acs[:, h:h + 1] - acs_t[h:h + 1, :], NEG_BIG))
                xe = jnp.where(lane_g == e, xg, 0.0).astype(BF16)
                y = y + _dot((cb * seg).astype(BF16), xe)
            state_ref[g] = s_prev * eacs[q - 1:q, gl] + _dot_tn(bm, (xg * dec_end[:, gl]).astype(BF16))
            ys.append(y)
        y = jnp.concatenate(ys, axis=1) + dskip_ref[...] * xs[sl]
        y = y * z_ref[sl, :].astype(F32)
        y = y * lax.rsqrt(jnp.mean(y * y, axis=-1, keepdims=True) + NORM_EPS) * nw_ref[...]
        o_ref[sl, :] = y.astype(BF16)


def _ssd_mixer(proj, small, p, batch, seq, ts):
    nb = seq // ts
    w = BRANCH_W
    rows = lambda col: pl.BlockSpec((ts, w), lambda b, s, col=col: (b * nb + s, col))
    hp = SSD_HEADS // SSD_GROUPS * SSD_HEAD_DIM
    return pl.pallas_call(
        _ssd_kernel,
        grid=(batch, nb),
        in_specs=[rows(COL_Z), rows(COL_X), rows(COL_BC),
                  pl.BlockSpec((ts, LANES), lambda b, s: (b * nb + s, 0)),
                  _const_spec((1, LANES)), _const_spec((1, LANES)),
                  _const_spec((1, w)), _const_spec((1, w)),
                  _const_spec((LANES, w)), _const_spec((SSD_CHUNK, SSD_CHUNK))],
        out_specs=pl.BlockSpec((ts, w), lambda b, s: (b * nb + s, 0)),
        out_shape=jax.ShapeDtypeStruct((batch * seq, w), BF16),
        scratch_shapes=[pltpu.VMEM((SSD_GROUPS, SSD_STATE, hp), F32)],
        compiler_params=_cparams(("parallel", "arbitrary")),
        name="ssd_mixer",
    )(proj, proj, proj, small, p["ssd_dtb"], p["ssd_alog"], p["ssd_dskip"], p["ssd_nw"], p["e8"], p["tri_ssd"])


def _unit_lower_inverses(mats, c):
    n = mats[0].shape[0]
    ri = lax.broadcasted_iota(jnp.int32, (n, n), 0)
    ci = lax.broadcasted_iota(jnp.int32, (n, n), 1)
    eye = jnp.where(ri == ci, 1.0, 0.0).astype(F32)
    pws = [(-a).astype(BF16) for a in mats]
    invs = [eye - a for a in mats]
    for _ in range(int(math.log2(c)) - 1):
        sq = [_dot(pw, pw) for pw in pws]
        pws = [s.astype(BF16) for s in sq]
        invs = [inv + _dot(inv.astype(BF16), pw) for inv, pw in zip(invs, pws)]
    return invs


def _dn_kernel(q_ref, k_ref, v_ref, gate_ref, sm_ref, alog_ref, dtb_ref, nw_ref, tri_ref, hsum_ref,
               o_ref, state_ref):
    first = pl.program_id(1) == 0
    ts = q_ref.shape[0]
    c = DN_CHUNK
    dk = DN_HEAD_DIM

    @pl.when(first)
    def _():
        state_ref[...] = jnp.zeros_like(state_ref)

    qf = q_ref[...].astype(F32)
    kf = k_ref[...].astype(F32)
    vf = v_ref[...].astype(F32)
    q_ss = _sel_right(qf * qf, hsum_ref[...], 1)
    k_ss = _sel_right(kf * kf, hsum_ref[...], 1)
    qf = qf * (lax.rsqrt(q_ss + NORM_EPS) * (dk ** -0.5))
    kf = kf * lax.rsqrt(k_ss + NORM_EPS)
    sm = sm_ref[...]
    lane = lax.broadcasted_iota(jnp.int32, (1, LANES), 1)
    a_lane = (lane >= SM_A) & (lane < SM_A + DN_HEADS)
    neg_a = jnp.where(a_lane, -jnp.exp(alog_ref[...]), 0.0)
    gdec = neg_a * _softplus(sm + dtb_ref[...])
    beta_s = _sigmoid(sm)
    gcs = _sel_left(tri_ref[...], gdec, 3)
    gtot = jnp.concatenate([jnp.broadcast_to(gcs[(n + 1) * c - 1:(n + 1) * c], (c, LANES))
                            for n in range(ts // c)], axis=0)
    gcs_t = gcs.T
    egcs_s = jnp.exp(gcs)
    edec_s = jnp.exp(gtot - gcs)
    elast_s = jnp.exp(gtot)

    def head_lanes(x, lane_idx):
        return jnp.broadcast_to(x[:, lane_idx:lane_idx + 1], (x.shape[0], dk))

    ri = lax.broadcasted_iota(jnp.int32, (ts, ts), 0)
    ci = lax.broadcasted_iota(jnp.int32, (ts, ts), 1)
    lower = (ci >= ri // c * c) & (ci <= ri)
    offdiag = ci != ri

    a_mats, rhs, qk, qg, kd, elast = [], [], [], [], [], []
    for h in range(DN_HEADS):
        hl = slice(h * dk, (h + 1) * dk)
        gl = SM_A + h
        q, k = qf[:, hl], kf[:, hl]
        bh = head_lanes(beta_s, SM_BETA + h)
        egcs = head_lanes(egcs_s, gl)
        gamma = jnp.exp(jnp.where(lower, gcs[:, gl:gl + 1] - gcs_t[gl:gl + 1, :], NEG_BIG))
        kb = k * bh
        k16 = k.astype(BF16)
        a_mats.append(jnp.where(offdiag, _dot_nt(kb.astype(BF16), k16) * gamma, 0.0))
        rhs.append(jnp.concatenate([kb * egcs, vf[:, hl] * bh], axis=1).astype(BF16))
        qk.append((_dot_nt(q.astype(BF16), k16) * gamma).astype(BF16))
        qg.append((q * egcs).astype(BF16))
        kd.append((k * head_lanes(edec_s, gl)).astype(BF16))
        elast.append([head_lanes(elast_s[n * c:n * c + 1], gl) for n in range(ts // c)])
    t_invs = _unit_lower_inverses(a_mats, c)
    wu = [_dot(t.astype(BF16), r) for t, r in zip(t_invs, rhs)]

    states = [state_ref[h] for h in range(DN_HEADS)]
    v_new = [[] for _ in range(DN_HEADS)]
    o_state = [[] for _ in range(DN_HEADS)]
    for ch in range(ts // c):
        sl = slice(ch * c, (ch + 1) * c)
        for h in range(DN_HEADS):
            hl = slice(h * dk, (h + 1) * dk)
            lhs = jnp.concatenate([wu[h][sl, :dk].astype(BF16), qg[h][sl]], axis=0)
            ws = _dot(lhs, states[h].astype(BF16))
            vn = wu[h][sl, dk:] - ws[:c]
            v_new[h].append(vn)
            o_state[h].append(ws[c:])
            states[h] = states[h] * elast[h][ch] + _dot_tn(kd[h][sl], vn.astype(BF16))

    for h in range(DN_HEADS):
        hl = slice(h * dk, (h + 1) * dk)
        state_ref[h] = states[h]
        vn_all = jnp.concatenate(v_new[h], axis=0).astype(BF16)
        o = jnp.concatenate(o_state[h], axis=0) + _dot(qk[h], vn_all)
        o = o * lax.rsqrt(jnp.mean(o * o, axis=-1, keepdims=True) + NORM_EPS) * nw_ref[...]
        o_ref[:, hl] = (o * gate_ref[:, hl].astype(F32)).astype(BF16)


def _dn_mixer(proj, small, p, batch, seq, ts):
    nb = seq // ts
    w = BRANCH_W
    rows = lambda col: pl.BlockSpec((ts, w), lambda b, s, col=col: (b * nb + s, col))
    return pl.pallas_call(
        _dn_kernel,
        grid=(batch, nb),
        in_specs=[rows(COL_DQ), rows(COL_DK), rows(COL_DV), rows(COL_DG),
                  pl.BlockSpec((ts, LANES), lambda b, s: (b * nb + s, 0)),
                  _const_spec((1, LANES)), _const_spec((1, LANES)), _const_spec((1, DN_HEAD_DIM)),
                  _const_spec((ts, ts)), _const_spec((w, w))],
        out_specs=pl.BlockSpec((ts, w), lambda b, s: (b * nb + s, 0)),
        out_shape=jax.ShapeDtypeStruct((batch * seq, w), BF16),
        scratch_shapes=[pltpu.VMEM((DN_HEADS, DN_HEAD_DIM, DN_HEAD_DIM), F32)],
        compiler_params=_cparams(("parallel", "arbitrary")),
        name="dn_mixer",
    )(proj, proj, proj, proj, small, p["dn_alog"], p["dn_dtb"], p["dn_nw"],
      _block_diag(ts, DN_CHUNK, True), _block_diag(w, DN_HEAD_DIM, False))


def _sg_kernel(u_ref, v_ref, g_ref, b_ref, w_ref, bs_ref, o_ref):
    ts = u_ref.shape[0]
    q = SG_CHUNK
    gd = BRANCH_W // SG_GROUPS
    u = u_ref[...].astype(F32)
    v = _layer_norm(v_ref[...].astype(F32), g_ref[...], b_ref[...])
    ri = lax.broadcasted_iota(jnp.int32, (q, q), 0)
    ci = lax.broadcasted_iota(jnp.int32, (q, q), 1)
    lower = ci <= ri
    for g in range(SG_GROUPS):
        gl = slice(g * gd, (g + 1) * gd)
        wg = jnp.where(lower, w_ref[g], 0.0).astype(BF16)
        for c in range(ts // q):
            sl = slice(c * q, (c + 1) * q)
            mixed = _dot(wg, v[sl, gl].astype(BF16)) + bs_ref[:, gl]
            o_ref[sl, gl] = (u[sl, gl] * mixed).astype(BF16)


def _sg_mixer(proj, p, t, ts):
    w = BRANCH_W
    rows = lambda col: pl.BlockSpec((ts, w), lambda i, col=col: (i, col))
    return pl.pallas_call(
        _sg_kernel,
        grid=(t // ts,),
        in_specs=[rows(COL_SU), rows(COL_SV), _const_spec((1, w)), _const_spec((1, w)),
                  _const_spec((SG_GROUPS, SG_CHUNK, SG_CHUNK)), _const_spec((SG_CHUNK, w))],
        out_specs=pl.BlockSpec((ts, w), lambda i: (i, 0)),
        out_shape=jax.ShapeDtypeStruct((t, w), BF16),
        compiler_params=_cparams(("parallel",)),
        name="sg_mixer",
    )(proj, proj, p["sg_g"], p["sg_b"], p["sg_w"], p["sg_bs"])


FOX_SPLIT = 3
FOX_AUG = LANES // (2 * FOX_HEADS)
assert FOX_AUG >= 2 * FOX_SPLIT
LOG2E = math.log2(math.e)


def _fcum_kernel(sm_ref, fb_ref, tri_ref, eq_ref, ek_ref, oq_ref, ok_ref, qa_ref, ka_ref, carry_ref):
    @pl.when(pl.program_id(1) == 0)
    def _():
        carry_ref[...] = jnp.zeros_like(carry_ref)

    n = sm_ref.shape[0]
    ls = jax.nn.log_sigmoid(sm_ref[...] + fb_ref[...])
    c = _sel_left(tri_ref[...], ls, 3) + carry_ref[...]
    carry_ref[...] = c[n - 1:n, :]
    terms = _split_terms(c * LOG2E, FOX_SPLIT)
    cq = sum(_dot(t, eq_ref[i]) for i, t in enumerate(terms))
    ck = sum(_dot(t, ek_ref[i]) for i, t in enumerate(terms))
    qa_ref[...] = (oq_ref[...] + cq).astype(BF16)
    ka_ref[...] = (ok_ref[...] - ck).astype(BF16)


def _forget_cumsum(small, p, batch, seq, tc):
    nb = seq // tc
    wa = LANES
    rows = pl.BlockSpec((tc, wa), lambda b, s: (b * nb + s, 0))
    return pl.pallas_call(
        _fcum_kernel,
        grid=(batch, nb),
        in_specs=[pl.BlockSpec((tc, LANES), lambda b, s: (b * nb + s, 0)),
                  _const_spec((1, LANES)), _const_spec((tc, tc)),
                  _const_spec((FOX_SPLIT, LANES, wa)), _const_spec((FOX_SPLIT, LANES, wa)),
                  _const_spec((1, wa)), _const_spec((1, wa))],
        out_specs=[rows, rows],
        out_shape=[jax.ShapeDtypeStruct((batch * seq, wa), BF16), jax.ShapeDtypeStruct((batch * seq, wa), BF16)],
        scratch_shapes=[pltpu.VMEM((1, LANES), F32)],
        compiler_params=_cparams(("parallel", "arbitrary")),
        name="forget_cumsum",
    )(small, p["fox_fb"], p["tri_fox"], p["fox_eq"], p["fox_ek"], p["fox_oq"], p["fox_ok"])


def _fox_kernel(ii_ref, jj_ref, q_ref, k_ref, v_ref, qa_ref, ka_ref, o_ref, qaug_ref, m_ref, acc_ref):
    t = pl.program_id(1)
    i = ii_ref[t]
    j = jj_ref[t]
    tq, tk = q_ref.shape[0], k_ref.shape[0]
    per = LANES // FOX_HEAD_DIM
    lane = lax.broadcasted_iota(jnp.int32, (1, LANES), 1)

    @pl.when(j == 0)
    def _():
        m_ref[...] = jnp.full_like(m_ref, NEG_BIG)
        acc_ref[...] = jnp.zeros_like(acc_ref)
        for h in range(FOX_HEADS):
            bl = slice(h // per * LANES, (h // per + 1) * LANES)
            q2 = (q_ref[:, bl].astype(F32) * (LOG2E * FOX_HEAD_DIM ** -0.5)).astype(BF16)
            qaug_ref[h, :, :LANES] = jnp.where(lane // FOX_HEAD_DIM == h % per, q2, jnp.zeros_like(q2))
            qa = qa_ref[...]
            qaug_ref[h, :, LANES:] = jnp.where(lane // FOX_AUG == h, qa, jnp.zeros_like(qa))

    def step(diagonal):
        ones = jnp.ones((tk, LANES), BF16)
        if diagonal:
            causal = (lax.broadcasted_iota(jnp.int32, (tq, tk), 1) <= lax.broadcasted_iota(jnp.int32, (tq, tk), 0))
        for h in range(FOX_HEADS):
            bl = slice(h // per * LANES, (h // per + 1) * LANES)
            kaug = jnp.concatenate([k_ref[:, bl], ka_ref[...]], axis=1)
            vaug = jnp.concatenate([v_ref[:, bl], ones], axis=1)
            s = _dot_nt(qaug_ref[h], kaug)
            if diagonal:
                s = jnp.where(causal, s, NEG_BIG)
            m_old = m_ref[h]
            m_new = jnp.maximum(m_old, jnp.max(s, axis=-1, keepdims=True))
            alpha = jnp.exp2(m_old - m_new)
            p = jnp.exp2(s - m_new[:, :1]).astype(BF16)
            acc_ref[h] = jnp.concatenate([alpha, alpha], axis=1) * acc_ref[h] + _dot(p, vaug)
            m_ref[h] = m_new

    @pl.when(j < i)
    def _():
        step(False)

    @pl.when(j == i)
    def _():
        step(True)
        for b in range(FOX_HEADS // per):
            out = jnp.zeros((tq, LANES), F32)
            for e in range(per):
                a = acc_ref[b * per + e]
                out = jnp.where(lane // FOX_HEAD_DIM == e, a[:, :LANES] / a[:, LANES:], out)
            o_ref[:, b * LANES:(b + 1) * LANES] = out.astype(BF16)


def _fox_mixer(proj, qa, ka, batch, seq, tq):
    nq = seq // tq
    w = BRANCH_W
    wa = LANES
    pairs = [(i, j) for i in range(nq) for j in range(i + 1)]
    ii = jnp.asarray([p[0] for p in pairs], jnp.int32)
    jj = jnp.asarray([p[1] for p in pairs], jnp.int32)
    qrow = lambda b, t, ii, jj: b * nq + ii[t]
    krow = lambda b, t, ii, jj: b * nq + jj[t]
    grid_spec = pltpu.PrefetchScalarGridSpec(
        num_scalar_prefetch=2,
        grid=(batch, len(pairs)),
        in_specs=[pl.BlockSpec((tq, w), lambda b, t, ii, jj: (qrow(b, t, ii, jj), COL_FQ)),
                  pl.BlockSpec((tq, w), lambda b, t, ii, jj: (krow(b, t, ii, jj), COL_FK)),
                  pl.BlockSpec((tq, w), lambda b, t, ii, jj: (krow(b, t, ii, jj), COL_FV)),
                  pl.BlockSpec((tq, wa), lambda b, t, ii, jj: (qrow(b, t, ii, jj), 0)),
                  pl.BlockSpec((tq, wa), lambda b, t, ii, jj: (krow(b, t, ii, jj), 0))],
        out_specs=pl.BlockSpec((tq, w), lambda b, t, ii, jj: (qrow(b, t, ii, jj), 0)),
        scratch_shapes=[pltpu.VMEM((FOX_HEADS, tq, 2 * LANES), BF16),
                        pltpu.VMEM((FOX_HEADS, tq, LANES), F32),
                        pltpu.VMEM((FOX_HEADS, tq, 2 * LANES), F32)])
    return pl.pallas_call(
        _fox_kernel,
        grid_spec=grid_spec,
        out_shape=jax.ShapeDtypeStruct((batch * seq, w), BF16),
        compiler_params=_cparams(("parallel", "arbitrary")),
        name="fox_mixer",
    )(ii, jj, proj, proj, proj, qa, ka)


def _merge_kernel(alpha, ya_ref, yb_ref, yc_ref, yd_ref, g0_ref, g1_ref, g2_ref, g3_ref,
                  wb_ref, wo_ref, h_ref, lg_ref, lb_ref, o32_ref, o16_ref):
    merged = None
    for i, (y_ref, g_ref) in enumerate(((ya_ref, g0_ref), (yb_ref, g1_ref), (yc_ref, g2_ref), (yd_ref, g3_ref))):
        term = g_ref[...].astype(F32) * _dot(y_ref[...], wb_ref[i])
        merged = term if merged is None else merged + term
    mix = _dot(merged.astype(BF16), wo_ref[...])
    y = _layer_norm(alpha * h_ref[...] + mix, lg_ref[...], lb_ref[...])
    o32_ref[...] = y
    o16_ref[...] = y.astype(BF16)


def _merge(ys, proj, h32, p, alpha, tm):
    t, d = h32.shape
    w = BRANCH_W
    gate0 = (COL_GATE * w) // d
    yspec = pl.BlockSpec((tm, w), lambda i: (i, 0))
    gspec = lambda n: pl.BlockSpec((tm, d), lambda i, n=n: (i, gate0 + n))
    row = pl.BlockSpec((tm, d), lambda i: (i, 0))
    return pl.pallas_call(
        functools.partial(_merge_kernel, alpha),
        grid=(t // tm,),
        in_specs=[yspec] * 4 + [gspec(n) for n in range(N_BRANCH)] +
                 [_const_spec((N_BRANCH, w, d)), _const_spec((d, d)), row,
                  _const_spec((1, d)), _const_spec((1, d))],
        out_specs=[row, row],
        out_shape=[jax.ShapeDtypeStruct((t, d), F32), jax.ShapeDtypeStruct((t, d), BF16)],
        compiler_params=_cparams(("parallel",)),
        name="merge_norm",
    )(*ys, proj, proj, proj, proj, p["w_branch"], p["w_out"], h32, p["ln1_g"], p["ln1_b"])


def _ffn_kernel(alpha, h16_ref, h32_ref, wu_ref, wd_ref, lg_ref, lb_ref, o32_ref, o16_ref, acc_ref):
    f = pl.program_id(1)

    @pl.when(f == 0)
    def _():
        acc_ref[...] = jnp.zeros_like(acc_ref)

    up = jnp.maximum(_dot(h16_ref[...], wu_ref[...]), 0.0)
    acc_ref[...] += _dot((up * up).astype(BF16), wd_ref[...])

    @pl.when(f == pl.num_programs(1) - 1)
    def _():
        y = _layer_norm(alpha * h32_ref[...] + acc_ref[...], lg_ref[...], lb_ref[...])
        o32_ref[...] = y
        o16_ref[...] = y.astype(BF16)


def _ffn(h16, h32, p, alpha, tm, tf):
    t, d = h32.shape
    dff = p["w_up"].shape[1]
    row = pl.BlockSpec((tm, d), lambda i, f: (i, 0))
    return pl.pallas_call(
        functools.partial(_ffn_kernel, alpha),
        grid=(t // tm, dff // tf),
        in_specs=[row, row, pl.BlockSpec((d, tf), lambda i, f: (0, f)), pl.BlockSpec((tf, d), lambda i, f: (f, 0)),
                  _const_spec((1, d)), _const_spec((1, d))],
        out_specs=[row, row],
        out_shape=[jax.ShapeDtypeStruct((t, d), F32), jax.ShapeDtypeStruct((t, d), BF16)],
        scratch_shapes=[pltpu.VMEM((tm, d), F32)],
        compiler_params=_cparams(("parallel", "arbitrary")),
        name="ffn_norm",
    )(h16, h32, p["w_up"], p["w_down"], p["ln2_g"], p["ln2_b"])


def _pad_lanes(v, start):
    return jnp.zeros((1, LANES), F32).at[0, start:start + v.shape[0]].set(v.astype(F32))


def _expander(start, heads, width):
    m = np.zeros((LANES, heads * width), np.float32)
    for h in range(heads):
        m[start + h, h * width:(h + 1) * width] = 1.0
    return jnp.asarray(m, BF16)


def _tri(n):
    return jnp.asarray(np.tril(np.ones((n, n), np.float32)), BF16)


def _block_diag(n, c, lower):
    blk = np.tril(np.ones((c, c), np.float32)) if lower else np.ones((c, c), np.float32)
    return jnp.asarray(np.kron(np.eye(n // c, dtype=np.float32), blk), BF16)


def _fox_aug_constants():
    wa = LANES
    eq = np.zeros((FOX_SPLIT, LANES, wa), np.float32)
    ek = np.zeros((FOX_SPLIT, LANES, wa), np.float32)
    oq = np.zeros((1, wa), np.float32)
    ok = np.zeros((1, wa), np.float32)
    for h in range(FOX_HEADS):
        for i in range(FOX_SPLIT):
            ek[i, SM_F + h, h * FOX_AUG + i] = 1.0
            eq[i, SM_F + h, h * FOX_AUG + FOX_SPLIT + i] = 1.0
            oq[0, h * FOX_AUG + i] = 1.0
            ok[0, h * FOX_AUG + FOX_SPLIT + i] = 1.0
    return jnp.asarray(eq, BF16), jnp.asarray(ek, BF16), jnp.asarray(oq), jnp.asarray(ok)


def _layer_params(l, a, tc):
    w = BRANCH_W
    w_in = a["w_in"][l]
    o_dt = w + SSD_XBC
    o_qkv = o_dt + SSD_HEADS
    o_beta = o_qkv + 3 * w
    o_a = o_beta + DN_HEADS
    o_gate = o_a + DN_HEADS
    o_f = o_gate + w + 2 * w + 3 * w
    o_gates = o_f + FOX_HEADS
    w_big = jnp.concatenate([w_in[:, :o_dt], w_in[:, o_qkv:o_beta], w_in[:, o_gate:o_f], w_in[:, o_gates:]],
                            axis=1).astype(BF16)
    w_small = jnp.zeros((w_in.shape[0], LANES), F32)
    w_small = w_small.at[:, SM_DT:SM_DT + SSD_HEADS].set(w_in[:, o_dt:o_qkv])
    w_small = w_small.at[:, SM_BETA:SM_BETA + DN_HEADS].set(w_in[:, o_beta:o_a])
    w_small = w_small.at[:, SM_A:SM_A + DN_HEADS].set(w_in[:, o_a:o_gate])
    w_small = w_small.at[:, SM_F:SM_F + FOX_HEADS].set(w_in[:, o_f:o_gates])
    fox_eq, fox_ek, fox_oq, fox_ok = _fox_aug_constants()
    n_big = w_big.shape[1]
    conv_w = jnp.zeros((CONV_TAPS, n_big), F32)
    conv_w = conv_w.at[:, COL_X * w:COL_X * w + SSD_XBC].set(a["ssd_conv_w"][l])
    conv_w = conv_w.at[:, COL_DQ * w:COL_DQ * w + 3 * w].set(a["dn_conv_w"][l])
    bias = jnp.zeros((1, n_big), F32)
    bias = bias.at[0, COL_X * w:COL_X * w + SSD_XBC].set(a["ssd_conv_b"][l])
    bias = bias.at[0, COL_GATE * w:].set(a["gate_b"][l].reshape(-1))
    return {
        "w_big": w_big, "w_small": w_small.astype(BF16), "conv_w": conv_w, "bias": bias,
        "ssd_dtb": _pad_lanes(a["ssd_dt_bias"][l], SM_DT), "ssd_alog": _pad_lanes(a["ssd_a_log"][l], SM_DT),
        "ssd_dskip": jnp.repeat(a["ssd_d"][l], SSD_HEAD_DIM)[None, :], "ssd_nw": a["ssd_norm_w"][l][None, :],
        "e8": _expander(SM_DT, SSD_HEADS, SSD_HEAD_DIM), "tri_ssd": _tri(SSD_CHUNK),
        "dn_alog": _pad_lanes(a["dn_a_log"][l], SM_A), "dn_dtb": _pad_lanes(a["dn_dt_bias"][l], SM_A),
        "dn_nw": a["dn_norm_w"][l][None, :],
        "sg_g": a["sg_ln_g"][l][None, :], "sg_b": a["sg_ln_b"][l][None, :], "sg_w": a["sg_w"][l],
        "sg_bs": jnp.repeat(a["sg_b"][l].T, BRANCH_W // SG_GROUPS, axis=1),
        "fox_fb": _pad_lanes(a["fox_f_bias"][l], SM_F), "tri_fox": _tri(tc),
        "fox_eq": fox_eq, "fox_ek": fox_ek, "fox_oq": fox_oq, "fox_ok": fox_ok,
        "w_branch": a["w_branch"][l].astype(BF16), "w_out": a["w_out"][l].astype(BF16),
        "ln1_g": a["ln1_g"][l][None, :], "ln1_b": a["ln1_b"][l][None, :],
        "w_up": a["w_up"][l].astype(BF16), "w_down": a["w_down"][l].astype(BF16),
        "ln2_g": a["ln2_g"][l][None, :], "ln2_b": a["ln2_b"][l][None, :],
    }


def _tiles(batch, seq):
    t = batch * seq
    pick = lambda n, cap: math.gcd(n, cap)
    return {
        "ln": pick(t, 512), "proj_m": pick(seq, 2048), "ssd": pick(seq, 256), "dn": pick(seq, 256),
        "sg": pick(t, 512), "fcum": pick(seq, 512), "fox": pick(seq, 512), "merge": pick(t, 512),
        "ffn_m": pick(t, 1024), "ffn_f": 1024,
    }


def kernel(x, ln_in_g, ln_in_b, w_in, ssd_conv_w, ssd_conv_b, ssd_dt_bias, ssd_a_log, ssd_d, ssd_norm_w, dn_conv_w,
           dn_a_log, dn_dt_bias, dn_norm_w, sg_ln_g, sg_ln_b, sg_w, sg_b, fox_f_bias, gate_b, w_branch, w_out,
           ln1_g, ln1_b, w_up, w_down, ln2_g, ln2_b):
    batch, seq, d = x.shape
    depth = w_in.shape[0]
    alpha = (2 * depth) ** 0.25
    a = dict(w_in=w_in, ssd_conv_w=ssd_conv_w, ssd_conv_b=ssd_conv_b, ssd_dt_bias=ssd_dt_bias, ssd_a_log=ssd_a_log,
             ssd_d=ssd_d, ssd_norm_w=ssd_norm_w, dn_conv_w=dn_conv_w, dn_a_log=dn_a_log, dn_dt_bias=dn_dt_bias,
             dn_norm_w=dn_norm_w, sg_ln_g=sg_ln_g, sg_ln_b=sg_ln_b, sg_w=sg_w, sg_b=sg_b, fox_f_bias=fox_f_bias,
             gate_b=gate_b, w_branch=w_branch, w_out=w_out, ln1_g=ln1_g, ln1_b=ln1_b, w_up=w_up, w_down=w_down,
             ln2_g=ln2_g, ln2_b=ln2_b)
    tl = _tiles(batch, seq)
    t = batch * seq
    h32, h16 = _entry_norm(x.reshape(t, d), ln_in_g, ln_in_b, tl["ln"])
    for l in range(depth):
        p = _layer_params(l, a, tl["fcum"])
        proj, small = _in_proj(h16, p["w_big"], p["w_small"], p["conv_w"], p["bias"], seq, tl["proj_m"])
        y_a = _ssd_mixer(proj, small, p, batch, seq, tl["ssd"])
        y_b = _dn_mixer(proj, small, p, batch, seq, tl["dn"])
        y_c = _sg_mixer(proj, p, t, tl["sg"])
        qa, ka = _forget_cumsum(small, p, batch, seq, tl["fcum"])
        y_d = _fox_mixer(proj, qa, ka, batch, seq, tl["fox"])
        h32, h16 = _merge((y_a, y_b, y_c, y_d), proj, h32, p, alpha, tl["merge"])
        h32, h16 = _ffn(h16, h32, p, alpha, tl["ffn_m"], tl["ffn_f"])
    return h32.reshape(batch, seq, d)
```

```python
import functools
import math

import numpy as np
import jax
import jax.numpy as jnp
from jax import lax
from jax.experimental import pallas as pl
from jax.experimental.pallas import tpu as pltpu

F32 = jnp.float32
BF16 = jnp.bfloat16

BRANCH_W = 512
N_BRANCH = 4
SSD_HEADS, SSD_HEAD_DIM, SSD_GROUPS, SSD_STATE, SSD_CONV = 8, 64, 2, 128, 4
SSD_XBC = BRANCH_W + 2 * SSD_GROUPS * SSD_STATE
DN_HEADS, DN_HEAD_DIM, DN_CONV = 4, 128, 4
SG_GROUPS, SG_CHUNK = 4, 128
FOX_HEADS, FOX_HEAD_DIM = 8, 64
LN_EPS = 1e-5
NORM_EPS = 1e-6
NEG_BIG = -1e30

LANES = 128
SUBLANES = 8
VMEM_LIMIT_BYTES = 56 * 1024 * 1024

SM_DT, SM_BETA, SM_A, SM_F = 0, 8, 12, 16

SSD_CHUNK = 128
DN_CHUNK = 64


def _dot(a, b):
    return jnp.dot(a, b, preferred_element_type=F32)


def _dot_nt(a, b):
    return lax.dot_general(a, b, (((1,), (1,)), ((), ())), preferred_element_type=F32)


def _dot_tn(a, b):
    return lax.dot_general(a, b, (((0,), (0,)), ((), ())), preferred_element_type=F32)


def _bdot(a, b):
    return _dot(a.astype(BF16), b.astype(BF16))


def _split_terms(x, n):
    terms, r = [], x
    for i in range(n):
        p = r.astype(BF16)
        terms.append(p)
        if i + 1 < n:
            r = r - p.astype(F32)
    return terms


def _sel_right(x, m, n):
    return sum(_dot(p, m) for p in _split_terms(x, n))


def _sel_left(m, x, n):
    return sum(_dot(m, p) for p in _split_terms(x, n))


def _sigmoid(x):
    return 0.5 * jnp.tanh(0.5 * x) + 0.5


def _silu(x):
    hx = 0.5 * x
    return hx + hx * jnp.tanh(hx)


def _softplus(x):
    return jnp.maximum(x, 0.0) + jnp.log1p(jnp.exp(-jnp.abs(x)))


def _layer_norm(x, g, b):
    mu = jnp.mean(x, axis=-1, keepdims=True)
    xc = x - mu
    var = jnp.mean(xc * xc, axis=-1, keepdims=True)
    return xc * lax.rsqrt(var + LN_EPS) * g + b


def _causal_conv(x, tail, w):
    n = x.shape[0]
    k = w.shape[0]
    row8 = lax.broadcasted_iota(jnp.int32, (SUBLANES, x.shape[1]), 0)
    acc = x * w[k - 1:k, :]
    for s in range(1, k):
        xr = pltpu.roll(x, s, axis=0)
        pr = pltpu.roll(tail, s, axis=0)
        head = jnp.where(row8 < s, pr, xr[:SUBLANES])
        xs = jnp.concatenate([head, xr[SUBLANES:]], axis=0)
        acc = acc + xs * w[k - 1 - s:k - s, :]
    return acc, x[n - SUBLANES:]


def _cparams(sem):
    return pltpu.CompilerParams(dimension_semantics=sem, vmem_limit_bytes=VMEM_LIMIT_BYTES)


def _const_spec(shape):
    nd = len(shape)
    return pl.BlockSpec(shape, lambda *_: (0,) * nd)


def _ln_kernel(x_ref, g_ref, b_ref, o32_ref, o16_ref):
    y = _layer_norm(x_ref[...], g_ref[...], b_ref[...])
    o32_ref[...] = y
    o16_ref[...] = y.astype(BF16)


def _entry_norm(x2, g, b, tm):
    t, d = x2.shape
    row = pl.BlockSpec((tm, d), lambda i: (i, 0))
    return pl.pallas_call(
        _ln_kernel,
        grid=(t // tm,),
        in_specs=[row, _const_spec((1, d)), _const_spec((1, d))],
        out_specs=[row, row],
        out_shape=[jax.ShapeDtypeStruct((t, d), F32), jax.ShapeDtypeStruct((t, d), BF16)],
        compiler_params=_cparams(("parallel",)),
        name="entry_norm",
    )(x2, g.reshape(1, d), b.reshape(1, d))


COL_Z, COL_X, COL_BC, COL_DQ, COL_DK, COL_DV, COL_DG, COL_SU, COL_SV, COL_FQ, COL_FK, COL_FV, COL_GATE = range(13)


def _proj_kernel(h_ref, w_ref, ws_ref, o_ref, os_ref):
    h = h_ref[...]
    o_ref[...] = _dot(h, w_ref[...]).astype(BF16)

    @pl.when(pl.program_id(1) == 0)
    def _():
        os_ref[...] = _dot(h, ws_ref[...])


def _in_proj(h16, w_big, w_small, tm, tn):
    t, d = h16.shape
    n = w_big.shape[1]
    return pl.pallas_call(
        _proj_kernel,
        grid=(t // tm, n // tn),
        in_specs=[pl.BlockSpec((tm, d), lambda i, j: (i, 0)),
                  pl.BlockSpec((d, tn), lambda i, j: (0, j)),
                  pl.BlockSpec((d, LANES), lambda i, j: (0, 0))],
        out_specs=[pl.BlockSpec((tm, tn), lambda i, j: (i, j)),
                   pl.BlockSpec((tm, LANES), lambda i, j: (i, 0))],
        out_shape=[jax.ShapeDtypeStruct((t, n), BF16), jax.ShapeDtypeStruct((t, LANES), F32)],
        compiler_params=_cparams(("parallel", "arbitrary")),
        name="in_proj",
    )(h16, w_big, w_small)


def _ssd_kernel(z_ref, x_ref, bc_ref, sm_ref, cwx_ref, cwbc_ref, cbx_ref, cbbc_ref, dtb_ref,
                alog_ref, dskip_ref, nw_ref, e8_ref, tri_ref, o_ref, state_ref, xtail_ref, bctail_ref):
    first = pl.program_id(1) == 0
    ts = x_ref.shape[0]
    q = SSD_CHUNK
    hp = SSD_HEADS // SSD_GROUPS * SSD_HEAD_DIM
    e8 = e8_ref[...]
    tri = tri_ref[...]

    @pl.when(first)
    def _():
        state_ref[...] = jnp.zeros_like(state_ref)
        xtail_ref[...] = jnp.zeros_like(xtail_ref)
        bctail_ref[...] = jnp.zeros_like(bctail_ref)

    xs, xtail_ref[...] = _causal_conv(x_ref[...].astype(F32), xtail_ref[...], cwx_ref[...])
    xs = _silu(xs + cbx_ref[...])
    bcm, bctail_ref[...] = _causal_conv(bc_ref[...].astype(F32), bctail_ref[...], cwbc_ref[...])
    bcm = _silu(bcm + cbbc_ref[...]).astype(BF16)
    lane = lax.broadcasted_iota(jnp.int32, (1, LANES), 1)
    head_lane = (lane >= SM_DT) & (lane < SM_DT + SSD_HEADS)
    a_neg = jnp.where(head_lane, -jnp.exp(alog_ref[...]), 0.0)
    dt = _softplus(sm_ref[...] + dtb_ref[...])
    dta = dt * a_neg
    xdt = xs * _sel_right(dt, e8, 2)
    ri = lax.broadcasted_iota(jnp.int32, (q, q), 0)
    ci = lax.broadcasted_iota(jnp.int32, (q, q), 1)
    lower = ci <= ri
    lane_g = lax.broadcasted_iota(jnp.int32, (1, hp), 1) // SSD_HEAD_DIM

    for c in range(ts // q):
        sl = slice(c * q, (c + 1) * q)
        acs = _sel_left(tri, dta[sl], 3)
        acs_t = acs.T
        eacs = _sel_right(jnp.exp(acs), e8, 2)
        dec_end = _sel_right(jnp.exp(acs[q - 1:q, :] - acs), e8, 2)
        ys = []
        for g in range(SSD_GROUPS):
            gl = slice(g * hp, (g + 1) * hp)
            bm = bcm[sl, g * SSD_STATE:(g + 1) * SSD_STATE]
            cm = bcm[sl, (SSD_GROUPS + g) * SSD_STATE:(SSD_GROUPS + g + 1) * SSD_STATE]
            cb = _dot_nt(cm, bm)
            xg = xdt[sl, gl]
            s_prev = state_ref[g]
            y = _dot(cm, s_prev.astype(BF16)) * eacs[:, gl]
            for e in range(SSD_HEADS // SSD_GROUPS):
                h = SM_DT + g * (SSD_HEADS // SSD_GROUPS) + e
                seg = jnp.exp(jnp.where(lower, acs[:, h:h + 1] - acs_t[h:h + 1, :], NEG_BIG))
                xe = jnp.where(lane_g == e, xg, 0.0).astype(BF16)
                y = y + _dot((cb * seg).astype(BF16), xe)
            state_ref[g] = s_prev * eacs[q - 1:q, gl] + _dot_tn(bm, (xg * dec_end[:, gl]).astype(BF16))
            ys.append(y)
        y = jnp.concatenate(ys, axis=1) + dskip_ref[...] * xs[sl]
        y = y * _silu(z_ref[sl, :].astype(F32))
        y = y * lax.rsqrt(jnp.mean(y * y, axis=-1, keepdims=True) + NORM_EPS) * nw_ref[...]
        o_ref[sl, :] = y.astype(BF16)


def _ssd_mixer(proj, small, p, batch, seq, ts):
    nb = seq // ts
    w = BRANCH_W
    rows = lambda col: pl.BlockSpec((ts, w), lambda b, s, col=col: (b * nb + s, col))
    hp = SSD_HEADS // SSD_GROUPS * SSD_HEAD_DIM
    return pl.pallas_call(
        _ssd_kernel,
        grid=(batch, nb),
        in_specs=[rows(COL_Z), rows(COL_X), rows(COL_BC),
                  pl.BlockSpec((ts, LANES), lambda b, s: (b * nb + s, 0)),
                  _const_spec((SSD_CONV, w)), _const_spec((SSD_CONV, w)),
                  _const_spec((1, w)), _const_spec((1, w)),
                  _const_spec((1, LANES)), _const_spec((1, LANES)),
                  _const_spec((1, w)), _const_spec((1, w)),
                  _const_spec((LANES, w)), _const_spec((SSD_CHUNK, SSD_CHUNK))],
        out_specs=pl.BlockSpec((ts, w), lambda b, s: (b * nb + s, 0)),
        out_shape=jax.ShapeDtypeStruct((batch * seq, w), BF16),
        scratch_shapes=[pltpu.VMEM((SSD_GROUPS, SSD_STATE, hp), F32),
                        pltpu.VMEM((SUBLANES, w), F32), pltpu.VMEM((SUBLANES, w), F32)],
        compiler_params=_cparams(("parallel", "arbitrary")),
        name="ssd_mixer",
    )(proj, proj, proj, small, p["ssd_cwx"], p["ssd_cwbc"], p["ssd_cbx"], p["ssd_cbbc"], p["ssd_dtb"],
      p["ssd_alog"], p["ssd_dskip"], p["ssd_nw"], p["e8"], p["tri_ssd"])


def _unit_lower_inverses(mats, c):
    n = mats[0].shape[0]
    ri = lax.broadcasted_iota(jnp.int32, (n, n), 0)
    ci = lax.broadcasted_iota(jnp.int32, (n, n), 1)
    eye = jnp.where(ri == ci, 1.0, 0.0).astype(F32)
    pws = [(-a).astype(BF16) for a in mats]
    invs = [eye - a for a in mats]
    for _ in range(int(math.log2(c)) - 1):
        sq = [_dot(pw, pw) for pw in pws]
        pws = [s.astype(BF16) for s in sq]
        invs = [inv + _dot(inv.astype(BF16), pw) for inv, pw in zip(invs, pws)]
    return invs


def _dn_kernel(q_ref, k_ref, v_ref, gate_ref, sm_ref, cwq_ref, cwk_ref, cwv_ref, alog_ref, dtb_ref, nw_ref,
               tri_ref, hsum_ref, o_ref, state_ref, qtail_ref, ktail_ref, vtail_ref):
    first = pl.program_id(1) == 0
    ts = q_ref.shape[0]
    c = DN_CHUNK
    dk = DN_HEAD_DIM

    @pl.when(first)
    def _():
        state_ref[...] = jnp.zeros_like(state_ref)
        qtail_ref[...] = jnp.zeros_like(qtail_ref)
        ktail_ref[...] = jnp.zeros_like(ktail_ref)
        vtail_ref[...] = jnp.zeros_like(vtail_ref)

    qf, qtail_ref[...] = _causal_conv(q_ref[...].astype(F32), qtail_ref[...], cwq_ref[...])
    kf, ktail_ref[...] = _causal_conv(k_ref[...].astype(F32), ktail_ref[...], cwk_ref[...])
    vf, vtail_ref[...] = _causal_conv(v_ref[...].astype(F32), vtail_ref[...], cwv_ref[...])
    qf, kf, vf = _silu(qf), _silu(kf), _silu(vf)
    q_ss = _sel_right(qf * qf, hsum_ref[...], 1)
    k_ss = _sel_right(kf * kf, hsum_ref[...], 1)
    qf = qf * (lax.rsqrt(q_ss + NORM_EPS) * (dk ** -0.5))
    kf = kf * lax.rsqrt(k_ss + NORM_EPS)
    sm = sm_ref[...]
    lane = lax.broadcasted_iota(jnp.int32, (1, LANES), 1)
    a_lane = (lane >= SM_A) & (lane < SM_A + DN_HEADS)
    neg_a = jnp.where(a_lane, -jnp.exp(alog_ref[...]), 0.0)
    gdec = neg_a * _softplus(sm + dtb_ref[...])
    beta_s = _sigmoid(sm)
    gcs = _sel_left(tri_ref[...], gdec, 3)
    gtot = jnp.concatenate([jnp.broadcast_to(gcs[(n + 1) * c - 1:(n + 1) * c], (c, LANES))
                            for n in range(ts // c)], axis=0)
    gcs_t = gcs.T
    egcs_s = jnp.exp(gcs)
    edec_s = jnp.exp(gtot - gcs)
    elast_s = jnp.exp(gtot)

    def head_lanes(x, lane_idx):
        return jnp.broadcast_to(x[:, lane_idx:lane_idx + 1], (x.shape[0], dk))

    ri = lax.broadcasted_iota(jnp.int32, (ts, ts), 0)
    ci = lax.broadcasted_iota(jnp.int32, (ts, ts), 1)
    lower = (ci >= ri // c * c) & (ci <= ri)
    offdiag = ci != ri

    a_mats, rhs, qk, qg, kd, elast = [], [], [], [], [], []
    for h in range(DN_HEADS):
        hl = slice(h * dk, (h + 1) * dk)
        gl = SM_A + h
        q, k = qf[:, hl], kf[:, hl]
        bh = head_lanes(beta_s, SM_BETA + h)
        egcs = head_lanes(egcs_s, gl)
        gamma = jnp.exp(jnp.where(lower, gcs[:, gl:gl + 1] - gcs_t[gl:gl + 1, :], NEG_BIG))
        kb = k * bh
        k16 = k.astype(BF16)
        a_mats.append(jnp.where(offdiag, _dot_nt(kb.astype(BF16), k16) * gamma, 0.0))
        rhs.append(jnp.concatenate([kb * egcs, vf[:, hl] * bh], axis=1).astype(BF16))
        qk.append((_dot_nt(q.astype(BF16), k16) * gamma).astype(BF16))
        qg.append((q * egcs).astype(BF16))
        kd.append((k * head_lanes(edec_s, gl)).astype(BF16))
        elast.append([head_lanes(elast_s[n * c:n * c + 1], gl) for n in range(ts // c)])
    t_invs = _unit_lower_inverses(a_mats, c)
    wu = [_dot(t.astype(BF16), r) for t, r in zip(t_invs, rhs)]

    states = [state_ref[h] for h in range(DN_HEADS)]
    v_new = [[] for _ in range(DN_HEADS)]
    o_state = [[] for _ in range(DN_HEADS)]
    for ch in range(ts // c):
        sl = slice(ch * c, (ch + 1) * c)
        for h in range(DN_HEADS):
            hl = slice(h * dk, (h + 1) * dk)
            lhs = jnp.concatenate([wu[h][sl, :dk].astype(BF16), qg[h][sl]], axis=0)
            ws = _dot(lhs, states[h].astype(BF16))
            vn = wu[h][sl, dk:] - ws[:c]
            v_new[h].append(vn)
            o_state[h].append(ws[c:])
            states[h] = states[h] * elast[h][ch] + _dot_tn(kd[h][sl], vn.astype(BF16))

    for h in range(DN_HEADS):
        hl = slice(h * dk, (h + 1) * dk)
        state_ref[h] = states[h]
        vn_all = jnp.concatenate(v_new[h], axis=0).astype(BF16)
        o = jnp.concatenate(o_state[h], axis=0) + _dot(qk[h], vn_all)
        o = o * lax.rsqrt(jnp.mean(o * o, axis=-1, keepdims=True) + NORM_EPS) * nw_ref[...]
        o_ref[:, hl] = (o * _silu(gate_ref[:, hl].astype(F32))).astype(BF16)


def _dn_mixer(proj, small, p, batch, seq, ts):
    nb = seq // ts
    w = BRANCH_W
    rows = lambda col: pl.BlockSpec((ts, w), lambda b, s, col=col: (b * nb + s, col))
    return pl.pallas_call(
        _dn_kernel,
        grid=(batch, nb),
        in_specs=[rows(COL_DQ), rows(COL_DK), rows(COL_DV), rows(COL_DG),
                  pl.BlockSpec((ts, LANES), lambda b, s: (b * nb + s, 0)),
                  _const_spec((DN_CONV, w)), _const_spec((DN_CONV, w)), _const_spec((DN_CONV, w)),
                  _const_spec((1, LANES)), _const_spec((1, LANES)), _const_spec((1, DN_HEAD_DIM)),
                  _const_spec((ts, ts)), _const_spec((w, w))],
        out_specs=pl.BlockSpec((ts, w), lambda b, s: (b * nb + s, 0)),
        out_shape=jax.ShapeDtypeStruct((batch * seq, w), BF16),
        scratch_shapes=[pltpu.VMEM((DN_HEADS, DN_HEAD_DIM, DN_HEAD_DIM), F32),
                        pltpu.VMEM((SUBLANES, w), F32), pltpu.VMEM((SUBLANES, w), F32),
                        pltpu.VMEM((SUBLANES, w), F32)],
        compiler_params=_cparams(("parallel", "arbitrary")),
        name="dn_mixer",
    )(proj, proj, proj, proj, small, p["dn_cwq"], p["dn_cwk"], p["dn_cwv"], p["dn_alog"], p["dn_dtb"], p["dn_nw"],
      _block_diag(ts, DN_CHUNK, True), _block_diag(w, DN_HEAD_DIM, False))


def _sg_kernel(u_ref, v_ref, g_ref, b_ref, w_ref, bs_ref, o_ref):
    ts = u_ref.shape[0]
    q = SG_CHUNK
    gd = BRANCH_W // SG_GROUPS
    u = jax.nn.gelu(u_ref[...].astype(F32))
    v = _layer_norm(jax.nn.gelu(v_ref[...].astype(F32)), g_ref[...], b_ref[...])
    ri = lax.broadcasted_iota(jnp.int32, (q, q), 0)
    ci = lax.broadcasted_iota(jnp.int32, (q, q), 1)
    lower = ci <= ri
    for g in range(SG_GROUPS):
        gl = slice(g * gd, (g + 1) * gd)
        wg = jnp.where(lower, w_ref[g], 0.0).astype(BF16)
        for c in range(ts // q):
            sl = slice(c * q, (c + 1) * q)
            mixed = _dot(wg, v[sl, gl].astype(BF16)) + bs_ref[:, gl]
            o_ref[sl, gl] = (u[sl, gl] * mixed).astype(BF16)


def _sg_mixer(proj, p, t, ts):
    w = BRANCH_W
    rows = lambda col: pl.BlockSpec((ts, w), lambda i, col=col: (i, col))
    return pl.pallas_call(
        _sg_kernel,
        grid=(t // ts,),
        in_specs=[rows(COL_SU), rows(COL_SV), _const_spec((1, w)), _const_spec((1, w)),
                  _const_spec((SG_GROUPS, SG_CHUNK, SG_CHUNK)), _const_spec((SG_CHUNK, w))],
        out_specs=pl.BlockSpec((ts, w), lambda i: (i, 0)),
        out_shape=jax.ShapeDtypeStruct((t, w), BF16),
        compiler_params=_cparams(("parallel",)),
        name="sg_mixer",
    )(proj, proj, p["sg_g"], p["sg_b"], p["sg_w"], p["sg_bs"])


FOX_SPLIT = 3
FOX_AUG = LANES // (2 * FOX_HEADS)
assert FOX_AUG >= 2 * FOX_SPLIT
LOG2E = math.log2(math.e)


def _fcum_kernel(sm_ref, k_ref, fb_ref, tri_ref, eq_ref, ek_ref, oq_ref, ok_ref, qa_ref, kt_ref, carry_ref):
    @pl.when(pl.program_id(1) == 0)
    def _():
        carry_ref[...] = jnp.zeros_like(carry_ref)

    n = sm_ref.shape[0]
    per = LANES // FOX_HEAD_DIM
    ls = jax.nn.log_sigmoid(sm_ref[...] + fb_ref[...])
    c = _sel_left(tri_ref[...], ls, 3) + carry_ref[...]
    carry_ref[...] = c[n - 1:n, :]
    terms = _split_terms(c * LOG2E, FOX_SPLIT)
    cq = sum(_dot(t, eq_ref[i]) for i, t in enumerate(terms))
    qa_ref[...] = (oq_ref[...] + cq).astype(BF16)
    ck_t = sum(_dot_nt(ek_ref[i], t) for i, t in enumerate(terms))
    aug_t = (ok_ref[...] - ck_t).astype(BF16)
    pad = jnp.zeros((LANES - FOX_HEAD_DIM - FOX_AUG, n), BF16)
    for b in range(FOX_HEADS // per):
        k_t = k_ref[:, b * LANES:(b + 1) * LANES].astype(F32).T.astype(BF16)
        for e in range(per):
            h = b * per + e
            kt_ref[h] = jnp.concatenate([k_t[e * FOX_HEAD_DIM:(e + 1) * FOX_HEAD_DIM],
                                         aug_t[h * FOX_AUG:(h + 1) * FOX_AUG], pad], axis=0)


def _forget_cumsum(proj, small, p, batch, seq, tc):
    nb = seq // tc
    na = FOX_HEADS * FOX_AUG
    return pl.pallas_call(
        _fcum_kernel,
        grid=(batch, nb),
        in_specs=[pl.BlockSpec((tc, LANES), lambda b, s: (b * nb + s, 0)),
                  pl.BlockSpec((tc, BRANCH_W), lambda b, s: (b * nb + s, COL_FK)),
                  _const_spec((1, LANES)), _const_spec((tc, tc)),
                  _const_spec((FOX_SPLIT, LANES, LANES)), _const_spec((FOX_SPLIT, na, LANES)),
                  _const_spec((1, LANES)), _const_spec((na, 1))],
        out_specs=[pl.BlockSpec((tc, LANES), lambda b, s: (b * nb + s, 0)),
                   pl.BlockSpec((None, FOX_HEADS, LANES, tc), lambda b, s: (b, 0, 0, s))],
        out_shape=[jax.ShapeDtypeStruct((batch * seq, LANES), BF16),
                   jax.ShapeDtypeStruct((batch, FOX_HEADS, LANES, seq), BF16)],
        scratch_shapes=[pltpu.VMEM((1, LANES), F32)],
        compiler_params=_cparams(("parallel", "arbitrary")),
        name="forget_cumsum",
    )(small, proj, p["fox_fb"], p["tri_fox"], p["fox_eq"], p["fox_ek"], p["fox_oq"], p["fox_ok"])


def _roll_lanes(x, shift):
    shift %= x.shape[-1]
    return x if shift == 0 else pltpu.roll(x, shift, axis=x.ndim - 1)


def _fox_kernel(ii_ref, jj_ref, q_ref, kt_ref, v_ref, qa_ref, o_ref, qaug_ref, m_ref, acc_ref):
    t = pl.program_id(1)
    i = ii_ref[t]
    j = jj_ref[t]
    tq = q_ref.shape[0]
    per = LANES // FOX_HEAD_DIM
    lane = lax.broadcasted_iota(jnp.int32, (1, LANES), 1)

    @pl.when(j == 0)
    def _():
        m_ref[...] = jnp.full_like(m_ref, NEG_BIG)
        acc_ref[...] = jnp.zeros_like(acc_ref)
        qa = qa_ref[...].astype(F32)
        for h in range(FOX_HEADS):
            bl = slice(h // per * LANES, (h // per + 1) * LANES)
            q2 = (q_ref[:, bl].astype(F32) * (LOG2E * FOX_HEAD_DIM ** -0.5)).astype(BF16).astype(F32)
            q_own = _roll_lanes(q2, LANES - h % per * FOX_HEAD_DIM)
            extra = _roll_lanes(qa, FOX_HEAD_DIM - h * FOX_AUG)
            in_extra = (lane >= FOX_HEAD_DIM) & (lane < FOX_HEAD_DIM + FOX_AUG)
            qaug_ref[h] = jnp.where(lane < FOX_HEAD_DIM, q_own, jnp.where(in_extra, extra, 0.0)).astype(BF16)

    def logits(h):
        return _dot(qaug_ref[h], kt_ref[h])

    def softmax(h, s, masked):
        if masked:
            s = jnp.where(lax.broadcasted_iota(jnp.int32, (tq, tq), 1) <= lax.broadcasted_iota(jnp.int32, (tq, tq), 0),
                          s, NEG_BIG)
        m_old = m_ref[h]
        m_new = jnp.maximum(m_old, jnp.max(s, axis=-1, keepdims=True))
        m_ref[h] = m_new
        return jnp.exp2(s - m_new[:, :1]).astype(BF16), jnp.exp2(m_old - m_new)

    def accumulate(h, p, alpha):
        bl = slice(h // per * LANES, (h // per + 1) * LANES)
        v2 = v_ref[:, bl]
        vaug = jnp.concatenate([v2, jnp.ones_like(v2)], axis=1)
        acc_ref[h] = jnp.concatenate([alpha, alpha], axis=1) * acc_ref[h] + _dot(p, vaug)

    def step(masked):
        s_next = logits(0)
        for h in range(FOX_HEADS):
            s = s_next
            if h + 1 < FOX_HEADS:
                s_next = logits(h + 1)
            accumulate(h, *softmax(h, s, masked))

    @pl.when(j < i)
    def _():
        step(False)

    @pl.when(j == i)
    def _():
        step(True)
        for b in range(FOX_HEADS // per):
            outs = []
            for e in range(per):
                a = acc_ref[b * per + e]
                outs.append(a[:, :LANES] / a[:, LANES:])
            o_ref[:, b * LANES:(b + 1) * LANES] = jnp.where(lane // FOX_HEAD_DIM == 0, outs[0], outs[1]).astype(BF16)


def _fox_mixer(proj, qa, kt, batch, seq, tq):
    nq = seq // tq
    w = BRANCH_W
    pairs = [(i, j) for i in range(nq) for j in range(i + 1)]
    ii = jnp.asarray([p[0] for p in pairs], jnp.int32)
    jj = jnp.asarray([p[1] for p in pairs], jnp.int32)
    qrow = lambda b, t, ii, jj: b * nq + ii[t]
    krow = lambda b, t, ii, jj: b * nq + jj[t]
    grid_spec = pltpu.PrefetchScalarGridSpec(
        num_scalar_prefetch=2,
        grid=(batch, len(pairs)),
        in_specs=[pl.BlockSpec((tq, w), lambda b, t, ii, jj: (qrow(b, t, ii, jj), COL_FQ)),
                  pl.BlockSpec((None, FOX_HEADS, LANES, tq), lambda b, t, ii, jj: (b, 0, 0, jj[t])),
                  pl.BlockSpec((tq, w), lambda b, t, ii, jj: (krow(b, t, ii, jj), COL_FV)),
                  pl.BlockSpec((tq, LANES), lambda b, t, ii, jj: (qrow(b, t, ii, jj), 0))],
        out_specs=pl.BlockSpec((tq, w), lambda b, t, ii, jj: (qrow(b, t, ii, jj), 0)),
        scratch_shapes=[pltpu.VMEM((FOX_HEADS, tq, LANES), BF16),
                        pltpu.VMEM((FOX_HEADS, tq, LANES), F32),
                        pltpu.VMEM((FOX_HEADS, tq, 2 * LANES), F32)])
    return pl.pallas_call(
        _fox_kernel,
        grid_spec=grid_spec,
        out_shape=jax.ShapeDtypeStruct((batch * seq, w), BF16),
        compiler_params=_cparams(("parallel", "arbitrary")),
        name="fox_mixer",
    )(ii, jj, proj, kt, proj, qa)


def _merge_kernel(alpha, ya_ref, yb_ref, yc_ref, yd_ref, g0_ref, g1_ref, g2_ref, g3_ref, gb_ref,
                  wb_ref, wo_ref, h_ref, lg_ref, lb_ref, o32_ref, o16_ref):
    merged = None
    for i, (y_ref, gl_ref) in enumerate(((ya_ref, g0_ref), (yb_ref, g1_ref), (yc_ref, g2_ref), (yd_ref, g3_ref))):
        gate = _sigmoid(gl_ref[...].astype(F32) + gb_ref[i:i + 1, :])
        term = gate * _dot(y_ref[...], wb_ref[i])
        merged = term if merged is None else merged + term
    mix = _dot(merged.astype(BF16), wo_ref[...])
    y = _layer_norm(alpha * h_ref[...] + mix, lg_ref[...], lb_ref[...])
    o32_ref[...] = y
    o16_ref[...] = y.astype(BF16)


def _merge(ys, proj, h32, p, alpha, tm):
    t, d = h32.shape
    w = BRANCH_W
    gate0 = (COL_GATE * w) // d
    yspec = pl.BlockSpec((tm, w), lambda i: (i, 0))
    gspec = lambda n: pl.BlockSpec((tm, d), lambda i, n=n: (i, gate0 + n))
    row = pl.BlockSpec((tm, d), lambda i: (i, 0))
    return pl.pallas_call(
        functools.partial(_merge_kernel, alpha),
        grid=(t // tm,),
        in_specs=[yspec] * 4 + [gspec(n) for n in range(N_BRANCH)] +
                 [_const_spec((N_BRANCH, d)), _const_spec((N_BRANCH, w, d)), _const_spec((d, d)), row,
                  _const_spec((1, d)), _const_spec((1, d))],
        out_specs=[row, row],
        out_shape=[jax.ShapeDtypeStruct((t, d), F32), jax.ShapeDtypeStruct((t, d), BF16)],
        compiler_params=_cparams(("parallel",)),
        name="merge_norm",
    )(*ys, proj, proj, proj, proj, p["gate_b"], p["w_branch"], p["w_out"], h32, p["ln1_g"], p["ln1_b"])


def _ffn_kernel(alpha, h16_ref, h32_ref, wu_ref, wd_ref, lg_ref, lb_ref, o32_ref, o16_ref, acc_ref):
    f = pl.program_id(1)

    @pl.when(f == 0)
    def _():
        acc_ref[...] = jnp.zeros_like(acc_ref)

    up = jnp.maximum(_dot(h16_ref[...], wu_ref[...]), 0.0)
    acc_ref[...] += _dot((up * up).astype(BF16), wd_ref[...])

    @pl.when(f == pl.num_programs(1) - 1)
    def _():
        y = _layer_norm(alpha * h32_ref[...] + acc_ref[...], lg_ref[...], lb_ref[...])
        o32_ref[...] = y
        o16_ref[...] = y.astype(BF16)


def _ffn(h16, h32, p, alpha, tm, tf):
    t, d = h32.shape
    dff = p["w_up"].shape[1]
    row = pl.BlockSpec((tm, d), lambda i, f: (i, 0))
    return pl.pallas_call(
        functools.partial(_ffn_kernel, alpha),
        grid=(t // tm, dff // tf),
        in_specs=[row, row, pl.BlockSpec((d, tf), lambda i, f: (0, f)), pl.BlockSpec((tf, d), lambda i, f: (f, 0)),
                  _const_spec((1, d)), _const_spec((1, d))],
        out_specs=[row, row],
        out_shape=[jax.ShapeDtypeStruct((t, d), F32), jax.ShapeDtypeStruct((t, d), BF16)],
        scratch_shapes=[pltpu.VMEM((tm, d), F32)],
        compiler_params=_cparams(("parallel", "arbitrary")),
        name="ffn_norm",
    )(h16, h32, p["w_up"], p["w_down"], p["ln2_g"], p["ln2_b"])


def _pad_lanes(v, start):
    return jnp.zeros((1, LANES), F32).at[0, start:start + v.shape[0]].set(v.astype(F32))


def _expander(start, heads, width):
    m = np.zeros((LANES, heads * width), np.float32)
    for h in range(heads):
        m[start + h, h * width:(h + 1) * width] = 1.0
    return jnp.asarray(m, BF16)


def _tri(n):
    return jnp.asarray(np.tril(np.ones((n, n), np.float32)), BF16)


def _block_diag(n, c, lower):
    blk = np.tril(np.ones((c, c), np.float32)) if lower else np.ones((c, c), np.float32)
    return jnp.asarray(np.kron(np.eye(n // c, dtype=np.float32), blk), BF16)


def _fox_aug_constants():
    na = FOX_HEADS * FOX_AUG
    eq = np.zeros((FOX_SPLIT, LANES, LANES), np.float32)
    ek = np.zeros((FOX_SPLIT, na, LANES), np.float32)
    oq = np.zeros((1, LANES), np.float32)
    ok = np.zeros((na, 1), np.float32)
    for h in range(FOX_HEADS):
        for i in range(FOX_SPLIT):
            ek[i, h * FOX_AUG + i, SM_F + h] = 1.0
            eq[i, SM_F + h, h * FOX_AUG + FOX_SPLIT + i] = 1.0
            oq[0, h * FOX_AUG + i] = 1.0
            ok[h * FOX_AUG + FOX_SPLIT + i, 0] = 1.0
    return jnp.asarray(eq, BF16), jnp.asarray(ek, BF16), jnp.asarray(oq), jnp.asarray(ok)


def _layer_params(l, a, tc):
    w = BRANCH_W
    w_in = a["w_in"][l]
    o_dt = w + SSD_XBC
    o_qkv = o_dt + SSD_HEADS
    o_beta = o_qkv + 3 * w
    o_a = o_beta + DN_HEADS
    o_gate = o_a + DN_HEADS
    o_f = o_gate + w + 2 * w + 3 * w
    o_gates = o_f + FOX_HEADS
    w_big = jnp.concatenate([w_in[:, :o_dt], w_in[:, o_qkv:o_beta], w_in[:, o_gate:o_f], w_in[:, o_gates:]],
                            axis=1).astype(BF16)
    w_small = jnp.zeros((w_in.shape[0], LANES), F32)
    w_small = w_small.at[:, SM_DT:SM_DT + SSD_HEADS].set(w_in[:, o_dt:o_qkv])
    w_small = w_small.at[:, SM_BETA:SM_BETA + DN_HEADS].set(w_in[:, o_beta:o_a])
    w_small = w_small.at[:, SM_A:SM_A + DN_HEADS].set(w_in[:, o_a:o_gate])
    w_small = w_small.at[:, SM_F:SM_F + FOX_HEADS].set(w_in[:, o_f:o_gates])
    fox_eq, fox_ek, fox_oq, fox_ok = _fox_aug_constants()
    scw, dcw, scb = a["ssd_conv_w"][l], a["dn_conv_w"][l], a["ssd_conv_b"][l]
    return {
        "w_big": w_big, "w_small": w_small.astype(BF16),
        "ssd_cwx": scw[:, :w], "ssd_cwbc": scw[:, w:], "ssd_cbx": scb[None, :w], "ssd_cbbc": scb[None, w:],
        "dn_cwq": dcw[:, :w], "dn_cwk": dcw[:, w:2 * w], "dn_cwv": dcw[:, 2 * w:], "gate_b": a["gate_b"][l],
        "ssd_dtb": _pad_lanes(a["ssd_dt_bias"][l], SM_DT), "ssd_alog": _pad_lanes(a["ssd_a_log"][l], SM_DT),
        "ssd_dskip": jnp.repeat(a["ssd_d"][l], SSD_HEAD_DIM)[None, :], "ssd_nw": a["ssd_norm_w"][l][None, :],
        "e8": _expander(SM_DT, SSD_HEADS, SSD_HEAD_DIM), "tri_ssd": _tri(SSD_CHUNK),
        "dn_alog": _pad_lanes(a["dn_a_log"][l], SM_A), "dn_dtb": _pad_lanes(a["dn_dt_bias"][l], SM_A),
        "dn_nw": a["dn_norm_w"][l][None, :],
        "sg_g": a["sg_ln_g"][l][None, :], "sg_b": a["sg_ln_b"][l][None, :], "sg_w": a["sg_w"][l],
        "sg_bs": jnp.repeat(a["sg_b"][l].T, BRANCH_W // SG_GROUPS, axis=1),
        "fox_fb": _pad_lanes(a["fox_f_bias"][l], SM_F), "tri_fox": _tri(tc),
        "fox_eq": fox_eq, "fox_ek": fox_ek, "fox_oq": fox_oq, "fox_ok": fox_ok,
        "w_branch": a["w_branch"][l].astype(BF16), "w_out": a["w_out"][l].astype(BF16),
        "ln1_g": a["ln1_g"][l][None, :], "ln1_b": a["ln1_b"][l][None, :],
        "w_up": a["w_up"][l].astype(BF16), "w_down": a["w_down"][l].astype(BF16),
        "ln2_g": a["ln2_g"][l][None, :], "ln2_b": a["ln2_b"][l][None, :],
    }


def _tiles(batch, seq):
    t = batch * seq
    pick = lambda n, cap: math.gcd(n, cap)
    return {
        "ln": pick(t, 512), "proj_m": pick(t, 2048), "proj_n": 2048, "ssd": pick(seq, 256), "dn": pick(seq, 256),
        "sg": pick(t, 512), "fcum": pick(seq, 512), "fox": pick(seq, 512), "merge": pick(t, 512),
        "ffn_m": pick(t, 1024), "ffn_f": 1024,
    }


def kernel(x, ln_in_g, ln_in_b, w_in, ssd_conv_w, ssd_conv_b, ssd_dt_bias, ssd_a_log, ssd_d, ssd_norm_w, dn_conv_w,
           dn_a_log, dn_dt_bias, dn_norm_w, sg_ln_g, sg_ln_b, sg_w, sg_b, fox_f_bias, gate_b, w_branch, w_out,
           ln1_g, ln1_b, w_up, w_down, ln2_g, ln2_b):
    batch, seq, d = x.shape
    depth = w_in.shape[0]
    alpha = (2 * depth) ** 0.25
    a = dict(w_in=w_in, ssd_conv_w=ssd_conv_w, ssd_conv_b=ssd_conv_b, ssd_dt_bias=ssd_dt_bias, ssd_a_log=ssd_a_log,
             ssd_d=ssd_d, ssd_norm_w=ssd_norm_w, dn_conv_w=dn_conv_w, dn_a_log=dn_a_log, dn_dt_bias=dn_dt_bias,
             dn_norm_w=dn_norm_w, sg_ln_g=sg_ln_g, sg_ln_b=sg_ln_b, sg_w=sg_w, sg_b=sg_b, fox_f_bias=fox_f_bias,
             gate_b=gate_b, w_branch=w_branch, w_out=w_out, ln1_g=ln1_g, ln1_b=ln1_b, w_up=w_up, w_down=w_down,
             ln2_g=ln2_g, ln2_b=ln2_b)
    tl = _tiles(batch, seq)
    t = batch * seq
    h32, h16 = _entry_norm(x.reshape(t, d), ln_in_g, ln_in_b, tl["ln"])
    for l in range(depth):
        p = _layer_params(l, a, tl["fcum"])
        proj, small = _in_proj(h16, p["w_big"], p["w_small"], tl["proj_m"], tl["proj_n"])
        y_a = _ssd_mixer(proj, small, p, batch, seq, tl["ssd"])
        y_b = _dn_mixer(proj, small, p, batch, seq, tl["dn"])
        y_c = _sg_mixer(proj, p, t, tl["sg"])
        qa, kt = _forget_cumsum(proj, small, p, batch, seq, tl["fcum"])
        y_d = _fox_mixer(proj, qa, kt, batch, seq, tl["fox"])
        h32, h16 = _merge((y_a, y_b, y_c, y_d), proj, h32, p, alpha, tl["merge"])
        h32, h16 = _ffn(h16, h32, p, alpha, tl["ffn_m"], tl["ffn_f"])
    return h32.reshape(batch, seq, d)
```

```python
import functools
import math

import numpy as np
import jax
import jax.numpy as jnp
from jax import lax
from jax.experimental import pallas as pl
from jax.experimental.pallas import tpu as pltpu

F32 = jnp.float32
BF16 = jnp.bfloat16

BRANCH_W = 512
N_BRANCH = 4
SSD_HEADS, SSD_HEAD_DIM, SSD_GROUPS, SSD_STATE, SSD_CONV = 8, 64, 2, 128, 4
SSD_XBC = BRANCH_W + 2 * SSD_GROUPS * SSD_STATE
DN_HEADS, DN_HEAD_DIM, DN_CONV = 4, 128, 4
SG_GROUPS, SG_CHUNK = 4, 128
FOX_HEADS, FOX_HEAD_DIM = 8, 64
LN_EPS = 1e-5
NORM_EPS = 1e-6
NEG_BIG = -1e30

LANES = 128
SUBLANES = 8
VMEM_LIMIT_BYTES = 56 * 1024 * 1024

SM_DT, SM_BETA, SM_A, SM_F = 0, 8, 12, 16

SSD_CHUNK = 128
DN_CHUNK = 64


def _dot(a, b):
    return jnp.dot(a, b, preferred_element_type=F32)


def _dot_nt(a, b):
    return lax.dot_general(a, b, (((1,), (1,)), ((), ())), preferred_element_type=F32)


def _dot_tn(a, b):
    return lax.dot_general(a, b, (((0,), (0,)), ((), ())), preferred_element_type=F32)


def _bdot(a, b):
    return _dot(a.astype(BF16), b.astype(BF16))


def _split_terms(x, n):
    terms, r = [], x
    for i in range(n):
        p = r.astype(BF16)
        terms.append(p)
        if i + 1 < n:
            r = r - p.astype(F32)
    return terms


def _sel_right(x, m, n):
    return sum(_dot(p, m) for p in _split_terms(x, n))


def _sel_left(m, x, n):
    return sum(_dot(m, p) for p in _split_terms(x, n))


def _sigmoid(x):
    return 0.5 * jnp.tanh(0.5 * x) + 0.5


def _silu(x):
    hx = 0.5 * x
    return hx + hx * jnp.tanh(hx)


def _softplus(x):
    return jnp.maximum(x, 0.0) + jnp.log1p(jnp.exp(-jnp.abs(x)))


def _layer_norm(x, g, b):
    mu = jnp.mean(x, axis=-1, keepdims=True)
    xc = x - mu
    var = jnp.mean(xc * xc, axis=-1, keepdims=True)
    return xc * lax.rsqrt(var + LN_EPS) * g + b


def _causal_conv(x, tail, w):
    n = x.shape[0]
    k = w.shape[0]
    row8 = lax.broadcasted_iota(jnp.int32, (SUBLANES, x.shape[1]), 0)
    acc = x * w[k - 1:k, :]
    for s in range(1, k):
        xr = pltpu.roll(x, s, axis=0)
        pr = pltpu.roll(tail, s, axis=0)
        head = jnp.where(row8 < s, pr, xr[:SUBLANES])
        xs = jnp.concatenate([head, xr[SUBLANES:]], axis=0)
        acc = acc + xs * w[k - 1 - s:k - s, :]
    return acc, x[n - SUBLANES:]


def _cparams(sem):
    return pltpu.CompilerParams(dimension_semantics=sem, vmem_limit_bytes=VMEM_LIMIT_BYTES)


def _const_spec(shape):
    nd = len(shape)
    return pl.BlockSpec(shape, lambda *_: (0,) * nd)


def _ln_kernel(x_ref, g_ref, b_ref, o32_ref, o16_ref):
    y = _layer_norm(x_ref[...], g_ref[...], b_ref[...])
    o32_ref[...] = y
    o16_ref[...] = y.astype(BF16)


def _entry_norm(x2, g, b, tm):
    t, d = x2.shape
    row = pl.BlockSpec((tm, d), lambda i: (i, 0))
    return pl.pallas_call(
        _ln_kernel,
        grid=(t // tm,),
        in_specs=[row, _const_spec((1, d)), _const_spec((1, d))],
        out_specs=[row, row],
        out_shape=[jax.ShapeDtypeStruct((t, d), F32), jax.ShapeDtypeStruct((t, d), BF16)],
        compiler_params=_cparams(("parallel",)),
        name="entry_norm",
    )(x2, g.reshape(1, d), b.reshape(1, d))


COL_Z, COL_X, COL_BC, COL_DQ, COL_DK, COL_DV, COL_DG, COL_SU, COL_SV, COL_FQ, COL_FK, COL_FV, COL_GATE = range(13)


def _proj_kernel(h_ref, w_ref, ws_ref, o_ref, os_ref):
    h = h_ref[...]
    o_ref[...] = _dot(h, w_ref[...]).astype(BF16)

    @pl.when(pl.program_id(1) == 0)
    def _():
        os_ref[...] = _dot(h, ws_ref[...])


def _in_proj(h16, w_big, w_small, tm, tn):
    t, d = h16.shape
    n = w_big.shape[1]
    return pl.pallas_call(
        _proj_kernel,
        grid=(t // tm, n // tn),
        in_specs=[pl.BlockSpec((tm, d), lambda i, j: (i, 0)),
                  pl.BlockSpec((d, tn), lambda i, j: (0, j)),
                  pl.BlockSpec((d, LANES), lambda i, j: (0, 0))],
        out_specs=[pl.BlockSpec((tm, tn), lambda i, j: (i, j)),
                   pl.BlockSpec((tm, LANES), lambda i, j: (i, 0))],
        out_shape=[jax.ShapeDtypeStruct((t, n), BF16), jax.ShapeDtypeStruct((t, LANES), F32)],
        compiler_params=_cparams(("parallel", "arbitrary")),
        name="in_proj",
    )(h16, w_big, w_small)


def _ssd_kernel(z_ref, x_ref, bc_ref, sm_ref, cwx_ref, cwbc_ref, cbx_ref, cbbc_ref, dtb_ref,
                alog_ref, dskip_ref, nw_ref, e8_ref, tri_ref, o_ref, state_ref, xtail_ref, bctail_ref):
    first = pl.program_id(1) == 0
    ts = x_ref.shape[0]
    q = SSD_CHUNK
    hp = SSD_HEADS // SSD_GROUPS * SSD_HEAD_DIM
    e8 = e8_ref[...]
    tri = tri_ref[...]

    @pl.when(first)
    def _():
        state_ref[...] = jnp.zeros_like(state_ref)
        xtail_ref[...] = jnp.zeros_like(xtail_ref)
        bctail_ref[...] = jnp.zeros_like(bctail_ref)

    xs, xtail_ref[...] = _causal_conv(x_ref[...].astype(F32), xtail_ref[...], cwx_ref[...])
    xs = _silu(xs + cbx_ref[...])
    bcm, bctail_ref[...] = _causal_conv(bc_ref[...].astype(F32), bctail_ref[...], cwbc_ref[...])
    bcm = _silu(bcm + cbbc_ref[...]).astype(BF16)
    lane = lax.broadcasted_iota(jnp.int32, (1, LANES), 1)
    head_lane = (lane >= SM_DT) & (lane < SM_DT + SSD_HEADS)
    a_neg = jnp.where(head_lane, -jnp.exp(alog_ref[...]), 0.0)
    dt = _softplus(sm_ref[...] + dtb_ref[...])
    dta = dt * a_neg
    xdt = xs * _sel_right(dt, e8, 2)
    ri = lax.broadcasted_iota(jnp.int32, (q, q), 0)
    ci = lax.broadcasted_iota(jnp.int32, (q, q), 1)
    lower = ci <= ri
    lane_g = lax.broadcasted_iota(jnp.int32, (1, hp), 1) // SSD_HEAD_DIM

    for c in range(ts // q):
        sl = slice(c * q, (c + 1) * q)
        acs = _sel_left(tri, dta[sl], 3)
        acs_t = acs.T
        eacs = _sel_right(jnp.exp(acs), e8, 2)
        dec_end = _sel_right(jnp.exp(acs[q - 1:q, :] - acs), e8, 2)
        ys = []
        for g in range(SSD_GROUPS):
            gl = slice(g * hp, (g + 1) * hp)
            bm = bcm[sl, g * SSD_STATE:(g + 1) * SSD_STATE]
            cm = bcm[sl, (SSD_GROUPS + g) * SSD_STATE:(SSD_GROUPS + g + 1) * SSD_STATE]
            cb = _dot_nt(cm, bm)
            xg = xdt[sl, gl]
            s_prev = state_ref[g]
            y = _dot(cm, s_prev.astype(BF16)) * eacs[:, gl]
            for e in range(SSD_HEADS // SSD_GROUPS):
                h = SM_DT + g * (SSD_HEADS // SSD_GROUPS) + e
                seg = jnp.exp(jnp.where(lower, acs[:, h:h + 1] - acs_t[h:h + 1, :], NEG_BIG))
                xe = jnp.where(lane_g == e, xg, 0.0).astype(BF16)
                y = y + _dot((cb * seg).astype(BF16), xe)
            state_ref[g] = s_prev * eacs[q - 1:q, gl] + _dot_tn(bm, (xg * dec_end[:, gl]).astype(BF16))
            ys.append(y)
        y = jnp.concatenate(ys, axis=1) + dskip_ref[...] * xs[sl]
        y = y * _silu(z_ref[sl, :].astype(F32))
        y = y * lax.rsqrt(jnp.mean(y * y, axis=-1, keepdims=True) + NORM_EPS) * nw_ref[...]
        o_ref[sl, :] = y.astype(BF16)


def _ssd_mixer(proj, small, p, batch, seq, ts):
    nb = seq // ts
    w = BRANCH_W
    rows = lambda col: pl.BlockSpec((ts, w), lambda b, s, col=col: (b * nb + s, col))
    hp = SSD_HEADS // SSD_GROUPS * SSD_HEAD_DIM
    return pl.pallas_call(
        _ssd_kernel,
        grid=(batch, nb),
        in_specs=[rows(COL_Z), rows(COL_X), rows(COL_BC),
                  pl.BlockSpec((ts, LANES), lambda b, s: (b * nb + s, 0)),
                  _const_spec((SSD_CONV, w)), _const_spec((SSD_CONV, w)),
                  _const_spec((1, w)), _const_spec((1, w)),
                  _const_spec((1, LANES)), _const_spec((1, LANES)),
                  _const_spec((1, w)), _const_spec((1, w)),
                  _const_spec((LANES, w)), _const_spec((SSD_CHUNK, SSD_CHUNK))],
        out_specs=pl.BlockSpec((ts, w), lambda b, s: (b * nb + s, 0)),
        out_shape=jax.ShapeDtypeStruct((batch * seq, w), BF16),
        scratch_shapes=[pltpu.VMEM((SSD_GROUPS, SSD_STATE, hp), F32),
                        pltpu.VMEM((SUBLANES, w), F32), pltpu.VMEM((SUBLANES, w), F32)],
        compiler_params=_cparams(("parallel", "arbitrary")),
        name="ssd_mixer",
    )(proj, proj, proj, small, p["ssd_cwx"], p["ssd_cwbc"], p["ssd_cbx"], p["ssd_cbbc"], p["ssd_dtb"],
      p["ssd_alog"], p["ssd_dskip"], p["ssd_nw"], p["e8"], p["tri_ssd"])


DN_SLOTS = 2
DN_STAGGER = 17


def _interleave_staggered(gens, stagger):
    active, waiting, rounds = [], list(gens), 0
    while active or waiting:
        if waiting and rounds % stagger == 0:
            active.append(waiting.pop(0))
        for g in list(active):
            try:
                next(g)
            except StopIteration:
                active.remove(g)
        rounds += 1


def _dn_kernel(q_ref, k_ref, v_ref, gate_ref, sm_ref, cwq_ref, cwk_ref, cwv_ref, alog_ref, dtb_ref, nw_ref,
               tri_ref, hsum_ref, o_ref, state_ref, qtail_ref, ktail_ref, vtail_ref):
    ts = q_ref.shape[1]
    c = DN_CHUNK
    dk = DN_HEAD_DIM
    nh = DN_HEADS

    @pl.when(pl.program_id(1) == 0)
    def _():
        state_ref[...] = jnp.zeros_like(state_ref)
        qtail_ref[...] = jnp.zeros_like(qtail_ref)
        ktail_ref[...] = jnp.zeros_like(ktail_ref)
        vtail_ref[...] = jnp.zeros_like(vtail_ref)

    def row_work(r):
        qf, qtail_ref[r] = _causal_conv(q_ref[r].astype(F32), qtail_ref[r], cwq_ref[...])
        qf = _silu(qf)
        yield
        kf, ktail_ref[r] = _causal_conv(k_ref[r].astype(F32), ktail_ref[r], cwk_ref[...])
        kf = _silu(kf)
        yield
        vf, vtail_ref[r] = _causal_conv(v_ref[r].astype(F32), vtail_ref[r], cwv_ref[...])
        vf = _silu(vf)
        yield
        qf = qf * (lax.rsqrt(_sel_right(qf * qf, hsum_ref[...], 1) + NORM_EPS) * (dk ** -0.5))
        kf = kf * lax.rsqrt(_sel_right(kf * kf, hsum_ref[...], 1) + NORM_EPS)
        yield
        sm = sm_ref[r]
        lane = lax.broadcasted_iota(jnp.int32, (1, LANES), 1)
        a_lane = (lane >= SM_A) & (lane < SM_A + nh)
        neg_a = jnp.where(a_lane, -jnp.exp(alog_ref[...]), 0.0)
        gdec = neg_a * _softplus(sm + dtb_ref[...])
        beta_s = _sigmoid(sm)
        gcs = _sel_left(tri_ref[...], gdec, 3)
        gtot = jnp.concatenate([jnp.broadcast_to(gcs[(n + 1) * c - 1:(n + 1) * c], (c, LANES))
                                for n in range(ts // c)], axis=0)
        gcs_t = gcs.T
        egcs_s = jnp.exp(gcs)
        edec_s = jnp.exp(gtot - gcs)
        elast_s = jnp.exp(gtot)
        ri = lax.broadcasted_iota(jnp.int32, (ts, ts), 0)
        ci = lax.broadcasted_iota(jnp.int32, (ts, ts), 1)
        lower = (ci >= ri // c * c) & (ci <= ri)
        offdiag = ci != ri
        yield

        def head_lanes(x, lane_idx):
            return jnp.broadcast_to(x[:, lane_idx:lane_idx + 1], (x.shape[0], dk))

        a_mats, rhs, qk, qg, kd, elast = [], [], [], [], [], []
        for h in range(nh):
            hl = slice(h * dk, (h + 1) * dk)
            gl = SM_A + h
            q, k = qf[:, hl], kf[:, hl]
            bh = head_lanes(beta_s, SM_BETA + h)
            egcs = head_lanes(egcs_s, gl)
            gamma = jnp.exp(jnp.where(lower, gcs[:, gl:gl + 1] - gcs_t[gl:gl + 1, :], NEG_BIG))
            kb = k * bh
            k16 = k.astype(BF16)
            a_mats.append(jnp.where(offdiag, _dot_nt(kb.astype(BF16), k16) * gamma, 0.0))
            yield
            rhs.append(jnp.concatenate([kb * egcs, vf[:, hl] * bh], axis=1).astype(BF16))
            qk.append((_dot_nt(q.astype(BF16), k16) * gamma).astype(BF16))
            yield
            qg.append((q * egcs).astype(BF16))
            kd.append((k * head_lanes(edec_s, gl)).astype(BF16))
            elast.append([head_lanes(elast_s[n * c:n * c + 1], gl) for n in range(ts // c)])
            yield

        eye = jnp.where(offdiag, 0.0, 1.0).astype(F32)
        pws = [(-a).astype(BF16) for a in a_mats]
        invs = [eye - a for a in a_mats]
        for _ in range(int(math.log2(c)) - 1):
            sq = [_dot(pw, pw) for pw in pws]
            yield
            pws = [x.astype(BF16) for x in sq]
            invs = [inv + _dot(inv.astype(BF16), pw) for inv, pw in zip(invs, pws)]
            yield
        wu = [_dot(invs[h].astype(BF16), rhs[h]) for h in range(nh)]
        yield

        pn = [[_dot_tn(kd[h][ch * c:(ch + 1) * c], wu[h][ch * c:(ch + 1) * c].astype(BF16))
               for ch in range(ts // c)] for h in range(nh)]
        yield
        states = [state_ref[r, h] for h in range(nh)]
        v_new = [[] for _ in range(nh)]
        o_state = [[] for _ in range(nh)]
        for ch in range(ts // c):
            sl = slice(ch * c, (ch + 1) * c)
            for h in range(nh):
                s16 = states[h].astype(BF16)
                states[h] = (states[h] * elast[h][ch] - _dot(pn[h][ch][:, :dk].astype(BF16), s16)
                             + pn[h][ch][:, dk:])
                lhs = jnp.concatenate([wu[h][sl, :dk].astype(BF16), qg[h][sl]], axis=0)
                ws = _dot(lhs, s16)
                v_new[h].append(wu[h][sl, dk:] - ws[:c])
                o_state[h].append(ws[c:])
            yield
        for h in range(nh):
            hl = slice(h * dk, (h + 1) * dk)
            state_ref[r, h] = states[h]
            vn_all = jnp.concatenate(v_new[h], axis=0).astype(BF16)
            o = jnp.concatenate(o_state[h], axis=0) + _dot(qk[h], vn_all)
            o = o * lax.rsqrt(jnp.mean(o * o, axis=-1, keepdims=True) + NORM_EPS) * nw_ref[...]
            o_ref[r, :, hl] = (o * _silu(gate_ref[r, :, hl].astype(F32))).astype(BF16)
            yield

    _interleave_staggered([row_work(r) for r in range(q_ref.shape[0])], DN_STAGGER)


def _dn_mixer(proj, small, p, batch, seq, ts):
    nb = seq // ts
    w = BRANCH_W
    nh, dk = DN_HEADS, DN_HEAD_DIM
    nr = math.gcd(batch, DN_SLOTS)
    proj3 = proj.reshape(batch, seq, proj.shape[1])
    small3 = small.reshape(batch, seq, LANES)
    rows = lambda col: pl.BlockSpec((nr, ts, w), lambda b, s, col=col: (b, s, col))
    out = pl.pallas_call(
        _dn_kernel,
        grid=(batch // nr, nb),
        in_specs=[rows(COL_DQ), rows(COL_DK), rows(COL_DV), rows(COL_DG),
                  pl.BlockSpec((nr, ts, LANES), lambda b, s: (b, s, 0)),
                  _const_spec((DN_CONV, w)), _const_spec((DN_CONV, w)), _const_spec((DN_CONV, w)),
                  _const_spec((1, LANES)), _const_spec((1, LANES)), _const_spec((1, dk)),
                  _const_spec((ts, ts)), _const_spec((w, w))],
        out_specs=pl.BlockSpec((nr, ts, w), lambda b, s: (b, s, 0)),
        out_shape=jax.ShapeDtypeStruct((batch, seq, w), BF16),
        scratch_shapes=[pltpu.VMEM((nr, nh, dk, dk), F32),
                        pltpu.VMEM((nr, SUBLANES, w), F32), pltpu.VMEM((nr, SUBLANES, w), F32),
                        pltpu.VMEM((nr, SUBLANES, w), F32)],
        compiler_params=_cparams(("parallel", "arbitrary")),
        name="dn_mixer",
    )(proj3, proj3, proj3, proj3, small3, p["dn_cwq"], p["dn_cwk"], p["dn_cwv"], p["dn_alog"], p["dn_dtb"],
      p["dn_nw"], _block_diag(ts, DN_CHUNK, True), _block_diag(w, DN_HEAD_DIM, False))
    return out.reshape(batch * seq, w)


def _sg_kernel(u_ref, v_ref, g_ref, b_ref, w_ref, bs_ref, o_ref):
    ts = u_ref.shape[0]
    q = SG_CHUNK
    gd = BRANCH_W // SG_GROUPS
    u = jax.nn.gelu(u_ref[...].astype(F32))
    v = _layer_norm(jax.nn.gelu(v_ref[...].astype(F32)), g_ref[...], b_ref[...])
    ri = lax.broadcasted_iota(jnp.int32, (q, q), 0)
    ci = lax.broadcasted_iota(jnp.int32, (q, q), 1)
    lower = ci <= ri
    for g in range(SG_GROUPS):
        gl = slice(g * gd, (g + 1) * gd)
        wg = jnp.where(lower, w_ref[g], 0.0).astype(BF16)
        for c in range(ts // q):
            sl = slice(c * q, (c + 1) * q)
            mixed = _dot(wg, v[sl, gl].astype(BF16)) + bs_ref[:, gl]
            o_ref[sl, gl] = (u[sl, gl] * mixed).astype(BF16)


def _sg_mixer(proj, p, t, ts):
    w = BRANCH_W
    rows = lambda col: pl.BlockSpec((ts, w), lambda i, col=col: (i, col))
    return pl.pallas_call(
        _sg_kernel,
        grid=(t // ts,),
        in_specs=[rows(COL_SU), rows(COL_SV), _const_spec((1, w)), _const_spec((1, w)),
                  _const_spec((SG_GROUPS, SG_CHUNK, SG_CHUNK)), _const_spec((SG_CHUNK, w))],
        out_specs=pl.BlockSpec((ts, w), lambda i: (i, 0)),
        out_shape=jax.ShapeDtypeStruct((t, w), BF16),
        compiler_params=_cparams(("parallel",)),
        name="sg_mixer",
    )(proj, proj, p["sg_g"], p["sg_b"], p["sg_w"], p["sg_bs"])


FOX_SPLIT = 3
FOX_AUG = LANES // (2 * FOX_HEADS)
assert FOX_AUG >= 2 * FOX_SPLIT
LOG2E = math.log2(math.e)


def _fcum_kernel(sm_ref, k_ref, fb_ref, tri_ref, eq_ref, ek_ref, oq_ref, ok_ref, qa_ref, kt_ref, carry_ref):
    @pl.when(pl.program_id(1) == 0)
    def _():
        carry_ref[...] = jnp.zeros_like(carry_ref)

    n = sm_ref.shape[0]
    per = LANES // FOX_HEAD_DIM
    ls = jax.nn.log_sigmoid(sm_ref[...] + fb_ref[...])
    c = _sel_left(tri_ref[...], ls, 3) + carry_ref[...]
    carry_ref[...] = c[n - 1:n, :]
    terms = _split_terms(c * LOG2E, FOX_SPLIT)
    cq = sum(_dot(t, eq_ref[i]) for i, t in enumerate(terms))
    qa_ref[...] = (oq_ref[...] + cq).astype(BF16)
    ck_t = sum(_dot_nt(ek_ref[i], t) for i, t in enumerate(terms))
    aug_t = (ok_ref[...] - ck_t).astype(BF16)
    pad = jnp.zeros((LANES - FOX_HEAD_DIM - FOX_AUG, n), BF16)
    for b in range(FOX_HEADS // per):
        k_t = k_ref[:, b * LANES:(b + 1) * LANES].astype(F32).T.astype(BF16)
        for e in range(per):
            h = b * per + e
            kt_ref[h] = jnp.concatenate([k_t[e * FOX_HEAD_DIM:(e + 1) * FOX_HEAD_DIM],
                                         aug_t[h * FOX_AUG:(h + 1) * FOX_AUG], pad], axis=0)


def _forget_cumsum(proj, small, p, batch, seq, tc):
    nb = seq // tc
    na = FOX_HEADS * FOX_AUG
    return pl.pallas_call(
        _fcum_kernel,
        grid=(batch, nb),
        in_specs=[pl.BlockSpec((tc, LANES), lambda b, s: (b * nb + s, 0)),
                  pl.BlockSpec((tc, BRANCH_W), lambda b, s: (b * nb + s, COL_FK)),
                  _const_spec((1, LANES)), _const_spec((tc, tc)),
                  _const_spec((FOX_SPLIT, LANES, LANES)), _const_spec((FOX_SPLIT, na, LANES)),
                  _const_spec((1, LANES)), _const_spec((na, 1))],
        out_specs=[pl.BlockSpec((tc, LANES), lambda b, s: (b * nb + s, 0)),
                   pl.BlockSpec((None, FOX_HEADS, LANES, tc), lambda b, s: (b, 0, 0, s))],
        out_shape=[jax.ShapeDtypeStruct((batch * seq, LANES), BF16),
                   jax.ShapeDtypeStruct((batch, FOX_HEADS, LANES, seq), BF16)],
        scratch_shapes=[pltpu.VMEM((1, LANES), F32)],
        compiler_params=_cparams(("parallel", "arbitrary")),
        name="forget_cumsum",
    )(small, proj, p["fox_fb"], p["tri_fox"], p["fox_eq"], p["fox_ek"], p["fox_oq"], p["fox_ok"])


def _roll_lanes(x, shift):
    shift %= x.shape[-1]
    return x if shift == 0 else pltpu.roll(x, shift, axis=x.ndim - 1)


def _fox_kernel(ii_ref, jj_ref, q_ref, kt_ref, v_ref, qa_ref, o_ref, qaug_ref, m_ref, acc_ref):
    t = pl.program_id(1)
    i = ii_ref[t]
    j = jj_ref[t]
    tq = q_ref.shape[0]
    per = LANES // FOX_HEAD_DIM
    lane = lax.broadcasted_iota(jnp.int32, (1, LANES), 1)

    @pl.when(j == 0)
    def _():
        m_ref[...] = jnp.full_like(m_ref, NEG_BIG)
        acc_ref[...] = jnp.zeros_like(acc_ref)
        qa = qa_ref[...].astype(F32)
        for h in range(FOX_HEADS):
            bl = slice(h // per * LANES, (h // per + 1) * LANES)
            q2 = (q_ref[:, bl].astype(F32) * (LOG2E * FOX_HEAD_DIM ** -0.5)).astype(BF16).astype(F32)
            q_own = _roll_lanes(q2, LANES - h % per * FOX_HEAD_DIM)
            extra = _roll_lanes(qa, FOX_HEAD_DIM - h * FOX_AUG)
            in_extra = (lane >= FOX_HEAD_DIM) & (lane < FOX_HEAD_DIM + FOX_AUG)
            qaug_ref[h] = jnp.where(lane < FOX_HEAD_DIM, q_own, jnp.where(in_extra, extra, 0.0)).astype(BF16)

    def logits(h):
        return _dot(qaug_ref[h], kt_ref[h])

    def softmax(h, s, masked):
        if masked:
            s = jnp.where(lax.broadcasted_iota(jnp.int32, (tq, tq), 1) <= lax.broadcasted_iota(jnp.int32, (tq, tq), 0),
                          s, NEG_BIG)
        m_old = m_ref[h]
        m_new = jnp.maximum(m_old, jnp.max(s, axis=-1, keepdims=True))
        m_ref[h] = m_new
        return jnp.exp2(s - m_new[:, :1]).astype(BF16), jnp.exp2(m_old - m_new)

    def accumulate(h, p, alpha):
        bl = slice(h // per * LANES, (h // per + 1) * LANES)
        v2 = v_ref[:, bl]
        vaug = jnp.concatenate([v2, jnp.ones_like(v2)], axis=1)
        acc_ref[h] = jnp.concatenate([alpha, alpha], axis=1) * acc_ref[h] + _dot(p, vaug)

    def step(masked):
        s_next = logits(0)
        for h in range(FOX_HEADS):
            s = s_next
            if h + 1 < FOX_HEADS:
                s_next = logits(h + 1)
            accumulate(h, *softmax(h, s, masked))

    @pl.when(j < i)
    def _():
        step(False)

    @pl.when(j == i)
    def _():
        step(True)
        for b in range(FOX_HEADS // per):
            outs = []
            for e in range(per):
                a = acc_ref[b * per + e]
                outs.append(a[:, :LANES] / a[:, LANES:])
            o_ref[:, b * LANES:(b + 1) * LANES] = jnp.where(lane // FOX_HEAD_DIM == 0, outs[0], outs[1]).astype(BF16)


def _fox_mixer(proj, qa, kt, batch, seq, tq):
    nq = seq // tq
    w = BRANCH_W
    pairs = [(i, j) for i in range(nq) for j in range(i + 1)]
    ii = jnp.asarray([p[0] for p in pairs], jnp.int32)
    jj = jnp.asarray([p[1] for p in pairs], jnp.int32)
    qrow = lambda b, t, ii, jj: b * nq + ii[t]
    krow = lambda b, t, ii, jj: b * nq + jj[t]
    grid_spec = pltpu.PrefetchScalarGridSpec(
        num_scalar_prefetch=2,
        grid=(batch, len(pairs)),
        in_specs=[pl.BlockSpec((tq, w), lambda b, t, ii, jj: (qrow(b, t, ii, jj), COL_FQ)),
                  pl.BlockSpec((None, FOX_HEADS, LANES, tq), lambda b, t, ii, jj: (b, 0, 0, jj[t])),
                  pl.BlockSpec((tq, w), lambda b, t, ii, jj: (krow(b, t, ii, jj), COL_FV)),
                  pl.BlockSpec((tq, LANES), lambda b, t, ii, jj: (qrow(b, t, ii, jj), 0))],
        out_specs=pl.BlockSpec((tq, w), lambda b, t, ii, jj: (qrow(b, t, ii, jj), 0)),
        scratch_shapes=[pltpu.VMEM((FOX_HEADS, tq, LANES), BF16),
                        pltpu.VMEM((FOX_HEADS, tq, LANES), F32),
                        pltpu.VMEM((FOX_HEADS, tq, 2 * LANES), F32)])
    return pl.pallas_call(
        _fox_kernel,
        grid_spec=grid_spec,
        out_shape=jax.ShapeDtypeStruct((batch * seq, w), BF16),
        compiler_params=_cparams(("parallel", "arbitrary")),
        name="fox_mixer",
    )(ii, jj, proj, kt, proj, qa)


def _merge_kernel(alpha, ya_ref, yb_ref, yc_ref, yd_ref, g0_ref, g1_ref, g2_ref, g3_ref, gb_ref,
                  wb_ref, wo_ref, h_ref, lg_ref, lb_ref, o32_ref, o16_ref):
    merged = None
    for i, (y_ref, gl_ref) in enumerate(((ya_ref, g0_ref), (yb_ref, g1_ref), (yc_ref, g2_ref), (yd_ref, g3_ref))):
        gate = _sigmoid(gl_ref[...].astype(F32) + gb_ref[i:i + 1, :])
        term = gate * _dot(y_ref[...], wb_ref[i])
        merged = term if merged is None else merged + term
    mix = _dot(merged.astype(BF16), wo_ref[...])
    y = _layer_norm(alpha * h_ref[...] + mix, lg_ref[...], lb_ref[...])
    o32_ref[...] = y
    o16_ref[...] = y.astype(BF16)


def _merge(ys, proj, h32, p, alpha, tm):
    t, d = h32.shape
    w = BRANCH_W
    gate0 = (COL_GATE * w) // d
    yspec = pl.BlockSpec((tm, w), lambda i: (i, 0))
    gspec = lambda n: pl.BlockSpec((tm, d), lambda i, n=n: (i, gate0 + n))
    row = pl.BlockSpec((tm, d), lambda i: (i, 0))
    return pl.pallas_call(
        functools.partial(_merge_kernel, alpha),
        grid=(t // tm,),
        in_specs=[yspec] * 4 + [gspec(n) for n in range(N_BRANCH)] +
                 [_const_spec((N_BRANCH, d)), _const_spec((N_BRANCH, w, d)), _const_spec((d, d)), row,
                  _const_spec((1, d)), _const_spec((1, d))],
        out_specs=[row, row],
        out_shape=[jax.ShapeDtypeStruct((t, d), F32), jax.ShapeDtypeStruct((t, d), BF16)],
        compiler_params=_cparams(("parallel",)),
        name="merge_norm",
    )(*ys, proj, proj, proj, proj, p["gate_b"], p["w_branch"], p["w_out"], h32, p["ln1_g"], p["ln1_b"])


def _ffn_kernel(alpha, h16_ref, h32_ref, wu_ref, wd_ref, lg_ref, lb_ref, o32_ref, o16_ref, acc_ref):
    f = pl.program_id(1)

    @pl.when(f == 0)
    def _():
        acc_ref[...] = jnp.zeros_like(acc_ref)

    up = jnp.maximum(_dot(h16_ref[...], wu_ref[...]), 0.0)
    acc_ref[...] += _dot((up * up).astype(BF16), wd_ref[...])

    @pl.when(f == pl.num_programs(1) - 1)
    def _():
        y = _layer_norm(alpha * h32_ref[...] + acc_ref[...], lg_ref[...], lb_ref[...])
        o32_ref[...] = y
        o16_ref[...] = y.astype(BF16)


def _ffn(h16, h32, p, alpha, tm, tf):
    t, d = h32.shape
    dff = p["w_up"].shape[1]
    row = pl.BlockSpec((tm, d), lambda i, f: (i, 0))
    return pl.pallas_call(
        functools.partial(_ffn_kernel, alpha),
        grid=(t // tm, dff // tf),
        in_specs=[row, row, pl.BlockSpec((d, tf), lambda i, f: (0, f)), pl.BlockSpec((tf, d), lambda i, f: (f, 0)),
                  _const_spec((1, d)), _const_spec((1, d))],
        out_specs=[row, row],
        out_shape=[jax.ShapeDtypeStruct((t, d), F32), jax.ShapeDtypeStruct((t, d), BF16)],
        scratch_shapes=[pltpu.VMEM((tm, d), F32)],
        compiler_params=_cparams(("parallel", "arbitrary")),
        name="ffn_norm",
    )(h16, h32, p["w_up"], p["w_down"], p["ln2_g"], p["ln2_b"])


def _pad_lanes(v, start):
    return jnp.zeros((1, LANES), F32).at[0, start:start + v.shape[0]].set(v.astype(F32))


def _expander(start, heads, width):
    m = np.zeros((LANES, heads * width), np.float32)
    for h in range(heads):
        m[start + h, h * width:(h + 1) * width] = 1.0
    return jnp.asarray(m, BF16)


def _tri(n):
    return jnp.asarray(np.tril(np.ones((n, n), np.float32)), BF16)


def _block_diag(n, c, lower):
    blk = np.tril(np.ones((c, c), np.float32)) if lower else np.ones((c, c), np.float32)
    return jnp.asarray(np.kron(np.eye(n // c, dtype=np.float32), blk), BF16)


def _fox_aug_constants():
    na = FOX_HEADS * FOX_AUG
    eq = np.zeros((FOX_SPLIT, LANES, LANES), np.float32)
    ek = np.zeros((FOX_SPLIT, na, LANES), np.float32)
    oq = np.zeros((1, LANES), np.float32)
    ok = np.zeros((na, 1), np.float32)
    for h in range(FOX_HEADS):
        for i in range(FOX_SPLIT):
            ek[i, h * FOX_AUG + i, SM_F + h] = 1.0
            eq[i, SM_F + h, h * FOX_AUG + FOX_SPLIT + i] = 1.0
            oq[0, h * FOX_AUG + i] = 1.0
            ok[h * FOX_AUG + FOX_SPLIT + i, 0] = 1.0
    return jnp.asarray(eq, BF16), jnp.asarray(ek, BF16), jnp.asarray(oq), jnp.asarray(ok)


def _layer_params(l, a, tc):
    w = BRANCH_W
    w_in = a["w_in"][l]
    o_dt = w + SSD_XBC
    o_qkv = o_dt + SSD_HEADS
    o_beta = o_qkv + 3 * w
    o_a = o_beta + DN_HEADS
    o_gate = o_a + DN_HEADS
    o_f = o_gate + w + 2 * w + 3 * w
    o_gates = o_f + FOX_HEADS
    w_big = jnp.concatenate([w_in[:, :o_dt], w_in[:, o_qkv:o_beta], w_in[:, o_gate:o_f], w_in[:, o_gates:]],
                            axis=1).astype(BF16)
    w_small = jnp.zeros((w_in.shape[0], LANES), F32)
    w_small = w_small.at[:, SM_DT:SM_DT + SSD_HEADS].set(w_in[:, o_dt:o_qkv])
    w_small = w_small.at[:, SM_BETA:SM_BETA + DN_HEADS].set(w_in[:, o_beta:o_a])
    w_small = w_small.at[:, SM_A:SM_A + DN_HEADS].set(w_in[:, o_a:o_gate])
    w_small = w_small.at[:, SM_F:SM_F + FOX_HEADS].set(w_in[:, o_f:o_gates])
    fox_eq, fox_ek, fox_oq, fox_ok = _fox_aug_constants()
    scw, dcw, scb = a["ssd_conv_w"][l], a["dn_conv_w"][l], a["ssd_conv_b"][l]
    return {
        "w_big": w_big, "w_small": w_small.astype(BF16),
        "ssd_cwx": scw[:, :w], "ssd_cwbc": scw[:, w:], "ssd_cbx": scb[None, :w], "ssd_cbbc": scb[None, w:],
        "dn_cwq": dcw[:, :w], "dn_cwk": dcw[:, w:2 * w], "dn_cwv": dcw[:, 2 * w:], "gate_b": a["gate_b"][l],
        "ssd_dtb": _pad_lanes(a["ssd_dt_bias"][l], SM_DT), "ssd_alog": _pad_lanes(a["ssd_a_log"][l], SM_DT),
        "ssd_dskip": jnp.repeat(a["ssd_d"][l], SSD_HEAD_DIM)[None, :], "ssd_nw": a["ssd_norm_w"][l][None, :],
        "e8": _expander(SM_DT, SSD_HEADS, SSD_HEAD_DIM), "tri_ssd": _tri(SSD_CHUNK),
        "dn_alog": _pad_lanes(a["dn_a_log"][l], SM_A), "dn_dtb": _pad_lanes(a["dn_dt_bias"][l], SM_A),
        "dn_nw": a["dn_norm_w"][l][None, :],
        "sg_g": a["sg_ln_g"][l][None, :], "sg_b": a["sg_ln_b"][l][None, :], "sg_w": a["sg_w"][l],
        "sg_bs": jnp.repeat(a["sg_b"][l].T, BRANCH_W // SG_GROUPS, axis=1),
        "fox_fb": _pad_lanes(a["fox_f_bias"][l], SM_F), "tri_fox": _tri(tc),
        "fox_eq": fox_eq, "fox_ek": fox_ek, "fox_oq": fox_oq, "fox_ok": fox_ok,
        "w_branch": a["w_branch"][l].astype(BF16), "w_out": a["w_out"][l].astype(BF16),
        "ln1_g": a["ln1_g"][l][None, :], "ln1_b": a["ln1_b"][l][None, :],
        "w_up": a["w_up"][l].astype(BF16), "w_down": a["w_down"][l].astype(BF16),
        "ln2_g": a["ln2_g"][l][None, :], "ln2_b": a["ln2_b"][l][None, :],
    }


def _tiles(batch, seq):
    t = batch * seq
    pick = lambda n, cap: math.gcd(n, cap)
    return {
        "ln": pick(t, 1024), "proj_m": pick(t, 2048), "proj_n": 2048, "ssd": pick(seq, 512), "dn": pick(seq, 256),
        "sg": pick(t, 2048), "fcum": pick(seq, 1024), "fox": pick(seq, 512), "merge": pick(t, 512),
        "ffn_m": pick(t, 1024), "ffn_f": 1024,
    }


def kernel(x, ln_in_g, ln_in_b, w_in, ssd_conv_w, ssd_conv_b, ssd_dt_bias, ssd_a_log, ssd_d, ssd_norm_w, dn_conv_w,
           dn_a_log, dn_dt_bias, dn_norm_w, sg_ln_g, sg_ln_b, sg_w, sg_b, fox_f_bias, gate_b, w_branch, w_out,
           ln1_g, ln1_b, w_up, w_down, ln2_g, ln2_b):
    batch, seq, d = x.shape
    depth = w_in.shape[0]
    alpha = (2 * depth) ** 0.25
    a = dict(w_in=w_in, ssd_conv_w=ssd_conv_w, ssd_conv_b=ssd_conv_b, ssd_dt_bias=ssd_dt_bias, ssd_a_log=ssd_a_log,
             ssd_d=ssd_d, ssd_norm_w=ssd_norm_w, dn_conv_w=dn_conv_w, dn_a_log=dn_a_log, dn_dt_bias=dn_dt_bias,
             dn_norm_w=dn_norm_w, sg_ln_g=sg_ln_g, sg_ln_b=sg_ln_b, sg_w=sg_w, sg_b=sg_b, fox_f_bias=fox_f_bias,
             gate_b=gate_b, w_branch=w_branch, w_out=w_out, ln1_g=ln1_g, ln1_b=ln1_b, w_up=w_up, w_down=w_down,
             ln2_g=ln2_g, ln2_b=ln2_b)
    tl = _tiles(batch, seq)
    t = batch * seq
    h32, h16 = _entry_norm(x.reshape(t, d), ln_in_g, ln_in_b, tl["ln"])
    for l in range(depth):
        p = _layer_params(l, a, tl["fcum"])
        proj, small = _in_proj(h16, p["w_big"], p["w_small"], tl["proj_m"], tl["proj_n"])
        y_a = _ssd_mixer(proj, small, p, batch, seq, tl["ssd"])
        y_b = _dn_mixer(proj, small, p, batch, seq, tl["dn"])
        y_c = _sg_mixer(proj, p, t, tl["sg"])
        qa, kt = _forget_cumsum(proj, small, p, batch, seq, tl["fcum"])
        y_d = _fox_mixer(proj, qa, kt, batch, seq, tl["fox"])
        h32, h16 = _merge((y_a, y_b, y_c, y_d), proj, h32, p, alpha, tl["merge"])
        h32, h16 = _ffn(h16, h32, p, alpha, tl["ffn_m"], tl["ffn_f"])
    return h32.reshape(batch, seq, d)
```

```python
import functools
import math

import numpy as np
import jax
import jax.numpy as jnp
from jax import lax
from jax.experimental import pallas as pl
from jax.experimental.pallas import tpu as pltpu

F32 = jnp.float32
BF16 = jnp.bfloat16

BRANCH_W = 512
N_BRANCH = 4
SSD_HEADS, SSD_HEAD_DIM, SSD_GROUPS, SSD_STATE, SSD_CONV = 8, 64, 2, 128, 4
SSD_XBC = BRANCH_W + 2 * SSD_GROUPS * SSD_STATE
DN_HEADS, DN_HEAD_DIM, DN_CONV = 4, 128, 4
SG_GROUPS, SG_CHUNK = 4, 128
FOX_HEADS, FOX_HEAD_DIM = 8, 64
LN_EPS = 1e-5
NORM_EPS = 1e-6
NEG_BIG = -1e30

LANES = 128
SUBLANES = 8
VMEM_LIMIT_BYTES = 56 * 1024 * 1024

SM_DT, SM_BETA, SM_A, SM_F = 0, 8, 12, 16

SSD_CHUNK = 128
DN_CHUNK = 64


def _dot(a, b):
    return jnp.dot(a, b, preferred_element_type=F32)


def _dot_nt(a, b):
    return lax.dot_general(a, b, (((1,), (1,)), ((), ())), preferred_element_type=F32)


def _dot_tn(a, b):
    return lax.dot_general(a, b, (((0,), (0,)), ((), ())), preferred_element_type=F32)


def _bdot(a, b):
    return _dot(a.astype(BF16), b.astype(BF16))


def _split_terms(x, n):
    terms, r = [], x
    for i in range(n):
        p = r.astype(BF16)
        terms.append(p)
        if i + 1 < n:
            r = r - p.astype(F32)
    return terms


def _sel_right(x, m, n):
    return sum(_dot(p, m) for p in _split_terms(x, n))


def _sel_left(m, x, n):
    return sum(_dot(m, p) for p in _split_terms(x, n))


def _sigmoid(x):
    return 0.5 * jnp.tanh(0.5 * x) + 0.5


def _silu(x):
    hx = 0.5 * x
    return hx + hx * jnp.tanh(hx)


def _softplus(x):
    return jnp.maximum(x, 0.0) + jnp.log1p(jnp.exp(-jnp.abs(x)))


def _layer_norm(x, g, b):
    mu = jnp.mean(x, axis=-1, keepdims=True)
    xc = x - mu
    var = jnp.mean(xc * xc, axis=-1, keepdims=True)
    return xc * lax.rsqrt(var + LN_EPS) * g + b


def _causal_conv(x, tail, w):
    n = x.shape[0]
    k = w.shape[0]
    row8 = lax.broadcasted_iota(jnp.int32, (SUBLANES, x.shape[1]), 0)
    acc = x * w[k - 1:k, :]
    for s in range(1, k):
        xr = pltpu.roll(x, s, axis=0)
        pr = pltpu.roll(tail, s, axis=0)
        head = jnp.where(row8 < s, pr, xr[:SUBLANES])
        xs = jnp.concatenate([head, xr[SUBLANES:]], axis=0)
        acc = acc + xs * w[k - 1 - s:k - s, :]
    return acc, x[n - SUBLANES:]


def _cparams(sem):
    return pltpu.CompilerParams(dimension_semantics=sem, vmem_limit_bytes=VMEM_LIMIT_BYTES)


def _const_spec(shape):
    nd = len(shape)
    return pl.BlockSpec(shape, lambda *_: (0,) * nd)


def _ln_kernel(x_ref, g_ref, b_ref, o32_ref, o16_ref):
    y = _layer_norm(x_ref[...], g_ref[...], b_ref[...])
    o32_ref[...] = y
    o16_ref[...] = y.astype(BF16)


def _entry_norm(x2, g, b, tm):
    t, d = x2.shape
    row = pl.BlockSpec((tm, d), lambda i: (i, 0))
    return pl.pallas_call(
        _ln_kernel,
        grid=(t // tm,),
        in_specs=[row, _const_spec((1, d)), _const_spec((1, d))],
        out_specs=[row, row],
        out_shape=[jax.ShapeDtypeStruct((t, d), F32), jax.ShapeDtypeStruct((t, d), BF16)],
        compiler_params=_cparams(("parallel",)),
        name="entry_norm",
    )(x2, g.reshape(1, d), b.reshape(1, d))


COL_Z, COL_X, COL_BC, COL_DQ, COL_DK, COL_DV, COL_DG, COL_SU, COL_SV, COL_FQ, COL_FK, COL_FV, COL_GATE = range(13)


def _proj_kernel(h_ref, w_ref, ws_ref, o_ref, os_ref):
    h = h_ref[...]
    o_ref[...] = _dot(h, w_ref[...]).astype(BF16)

    @pl.when(pl.program_id(1) == 0)
    def _():
        os_ref[...] = _dot(h, ws_ref[...])


def _layer_spec(l, shape):
    nd = len(shape)
    return pl.BlockSpec((None,) + tuple(shape), lambda *_: (l,) + (0,) * nd)


def _in_proj(h16, p, l, tm, tn):
    t, d = h16.shape
    w_big, w_small = p["w_big"], p["w_small"]
    n = w_big.shape[2]
    return pl.pallas_call(
        _proj_kernel,
        grid=(t // tm, n // tn),
        in_specs=[pl.BlockSpec((tm, d), lambda i, j: (i, 0)),
                  pl.BlockSpec((None, d, tn), lambda i, j: (l, 0, j)),
                  _layer_spec(l, (d, LANES))],
        out_specs=[pl.BlockSpec((tm, tn), lambda i, j: (i, j)),
                   pl.BlockSpec((tm, LANES), lambda i, j: (i, 0))],
        out_shape=[jax.ShapeDtypeStruct((t, n), BF16), jax.ShapeDtypeStruct((t, LANES), F32)],
        compiler_params=_cparams(("parallel", "arbitrary")),
        name="in_proj",
    )(h16, w_big, w_small)


def _ssd_kernel(z_ref, x_ref, bc_ref, sm_ref, cwx_ref, cwbc_ref, cbx_ref, cbbc_ref, vec_ref,
                dskip_ref, nw_ref, e8_ref, tri_ref, o_ref, state_ref, xtail_ref, bctail_ref):
    dtb_ref = vec_ref.at[VEC_SSD_DTB:VEC_SSD_DTB + 1]
    alog_ref = vec_ref.at[VEC_SSD_ALOG:VEC_SSD_ALOG + 1]
    first = pl.program_id(1) == 0
    ts = x_ref.shape[0]
    q = SSD_CHUNK
    hp = SSD_HEADS // SSD_GROUPS * SSD_HEAD_DIM
    e8 = e8_ref[...]
    tri = tri_ref[...]

    @pl.when(first)
    def _():
        state_ref[...] = jnp.zeros_like(state_ref)
        xtail_ref[...] = jnp.zeros_like(xtail_ref)
        bctail_ref[...] = jnp.zeros_like(bctail_ref)

    xs, xtail_ref[...] = _causal_conv(x_ref[...].astype(F32), xtail_ref[...], cwx_ref[...])
    xs = _silu(xs + cbx_ref[...])
    bcm, bctail_ref[...] = _causal_conv(bc_ref[...].astype(F32), bctail_ref[...], cwbc_ref[...])
    bcm = _silu(bcm + cbbc_ref[...]).astype(BF16)
    lane = lax.broadcasted_iota(jnp.int32, (1, LANES), 1)
    head_lane = (lane >= SM_DT) & (lane < SM_DT + SSD_HEADS)
    a_neg = jnp.where(head_lane, -jnp.exp(alog_ref[...]), 0.0)
    dt = _softplus(sm_ref[...] + dtb_ref[...])
    dta = dt * a_neg
    xdt = xs * _sel_right(dt, e8, 2)
    ri = lax.broadcasted_iota(jnp.int32, (q, q), 0)
    ci = lax.broadcasted_iota(jnp.int32, (q, q), 1)
    lower = ci <= ri
    lane_g = lax.broadcasted_iota(jnp.int32, (1, hp), 1) // SSD_HEAD_DIM

    for c in range(ts // q):
        sl = slice(c * q, (c + 1) * q)
        acs = _sel_left(tri, dta[sl], 3)
        acs_t = acs.T
        eacs = _sel_right(jnp.exp(acs), e8, 2)
        dec_end = _sel_right(jnp.exp(acs[q - 1:q, :] - acs), e8, 2)
        ys = []
        for g in range(SSD_GROUPS):
            gl = slice(g * hp, (g + 1) * hp)
            bm = bcm[sl, g * SSD_STATE:(g + 1) * SSD_STATE]
            cm = bcm[sl, (SSD_GROUPS + g) * SSD_STATE:(SSD_GROUPS + g + 1) * SSD_STATE]
            cb = _dot_nt(cm, bm)
            xg = xdt[sl, gl]
            s_prev = state_ref[g]
            y = _dot(cm, s_prev.astype(BF16)) * eacs[:, gl]
            for e in range(SSD_HEADS // SSD_GROUPS):
                h = SM_DT + g * (SSD_HEADS // SSD_GROUPS) + e
                seg = jnp.exp(jnp.where(lower, acs[:, h:h + 1] - acs_t[h:h + 1, :], NEG_BIG))
                xe = jnp.where(lane_g == e, xg, 0.0).astype(BF16)
                y = y + _dot((cb * seg).astype(BF16), xe)
            state_ref[g] = s_prev * eacs[q - 1:q, gl] + _dot_tn(bm, (xg * dec_end[:, gl]).astype(BF16))
            ys.append(y)
        y = jnp.concatenate(ys, axis=1) + dskip_ref[...] * xs[sl]
        y = y * _silu(z_ref[sl, :].astype(F32))
        y = y * lax.rsqrt(jnp.mean(y * y, axis=-1, keepdims=True) + NORM_EPS) * nw_ref[...]
        o_ref[sl, :] = y.astype(BF16)


def _ssd_mixer(proj, small, p, l, batch, seq, ts):
    nb = seq // ts
    w = BRANCH_W
    rows = lambda col: pl.BlockSpec((ts, w), lambda b, s, col=col: (b * nb + s, col))
    half = lambda r, n: pl.BlockSpec((None, r, w), lambda b, s, n=n: (l, 0, n))
    hp = SSD_HEADS // SSD_GROUPS * SSD_HEAD_DIM
    return pl.pallas_call(
        _ssd_kernel,
        grid=(batch, nb),
        in_specs=[rows(COL_Z), rows(COL_X), rows(COL_BC),
                  pl.BlockSpec((ts, LANES), lambda b, s: (b * nb + s, 0)),
                  half(SSD_CONV, 0), half(SSD_CONV, 1), half(1, 0), half(1, 1),
                  _layer_spec(l, (VEC_ROWS, LANES)),
                  _layer_spec(l, (1, w)), _layer_spec(l, (1, w)),
                  _const_spec((LANES, w)), _const_spec((SSD_CHUNK, SSD_CHUNK))],
        out_specs=pl.BlockSpec((ts, w), lambda b, s: (b * nb + s, 0)),
        out_shape=jax.ShapeDtypeStruct((batch * seq, w), BF16),
        scratch_shapes=[pltpu.VMEM((SSD_GROUPS, SSD_STATE, hp), F32),
                        pltpu.VMEM((SUBLANES, w), F32), pltpu.VMEM((SUBLANES, w), F32)],
        compiler_params=_cparams(("parallel", "arbitrary")),
        name="ssd_mixer",
    )(proj, proj, proj, small, p["ssd_conv_w"], p["ssd_conv_w"], p["ssd_conv_b"], p["ssd_conv_b"], p["vec"],
      p["ssd_dskip"], p["ssd_nw"], p["e8"], p["tri_ssd"])


DN_SLOTS = 2
DN_STAGGER = 17


def _interleave_staggered(gens, stagger):
    active, waiting, rounds = [], list(gens), 0
    while active or waiting:
        if waiting and rounds % stagger == 0:
            active.append(waiting.pop(0))
        for g in list(active):
            try:
                next(g)
            except StopIteration:
                active.remove(g)
        rounds += 1


def _dn_kernel(q_ref, k_ref, v_ref, gate_ref, sm_ref, cwq_ref, cwk_ref, cwv_ref, vec_ref, nw_ref,
               tri_ref, hsum_ref, o_ref, state_ref, qtail_ref, ktail_ref, vtail_ref):
    alog_ref = vec_ref.at[VEC_DN_ALOG:VEC_DN_ALOG + 1]
    dtb_ref = vec_ref.at[VEC_DN_DTB:VEC_DN_DTB + 1]
    ts = q_ref.shape[1]
    c = DN_CHUNK
    dk = DN_HEAD_DIM
    nh = DN_HEADS

    @pl.when(pl.program_id(1) == 0)
    def _():
        state_ref[...] = jnp.zeros_like(state_ref)
        qtail_ref[...] = jnp.zeros_like(qtail_ref)
        ktail_ref[...] = jnp.zeros_like(ktail_ref)
        vtail_ref[...] = jnp.zeros_like(vtail_ref)

    def row_work(r):
        qf, qtail_ref[r] = _causal_conv(q_ref[r].astype(F32), qtail_ref[r], cwq_ref[...])
        qf = _silu(qf)
        yield
        kf, ktail_ref[r] = _causal_conv(k_ref[r].astype(F32), ktail_ref[r], cwk_ref[...])
        kf = _silu(kf)
        yield
        vf, vtail_ref[r] = _causal_conv(v_ref[r].astype(F32), vtail_ref[r], cwv_ref[...])
        vf = _silu(vf)
        yield
        qf = qf * (lax.rsqrt(_sel_right(qf * qf, hsum_ref[...], 1) + NORM_EPS) * (dk ** -0.5))
        kf = kf * lax.rsqrt(_sel_right(kf * kf, hsum_ref[...], 1) + NORM_EPS)
        yield
        sm = sm_ref[r]
        lane = lax.broadcasted_iota(jnp.int32, (1, LANES), 1)
        a_lane = (lane >= SM_A) & (lane < SM_A + nh)
        neg_a = jnp.where(a_lane, -jnp.exp(alog_ref[...]), 0.0)
        gdec = neg_a * _softplus(sm + dtb_ref[...])
        beta_s = _sigmoid(sm)
        gcs = _sel_left(tri_ref[...], gdec, 3)
        gtot = jnp.concatenate([jnp.broadcast_to(gcs[(n + 1) * c - 1:(n + 1) * c], (c, LANES))
                                for n in range(ts // c)], axis=0)
        gcs_t = gcs.T
        egcs_s = jnp.exp(gcs)
        edec_s = jnp.exp(gtot - gcs)
        elast_s = jnp.exp(gtot)
        ri = lax.broadcasted_iota(jnp.int32, (ts, ts), 0)
        ci = lax.broadcasted_iota(jnp.int32, (ts, ts), 1)
        lower = (ci >= ri // c * c) & (ci <= ri)
        offdiag = ci != ri
        yield

        def head_lanes(x, lane_idx):
            return jnp.broadcast_to(x[:, lane_idx:lane_idx + 1], (x.shape[0], dk))

        a_mats, rhs, qk, qg, kd, elast = [], [], [], [], [], []
        for h in range(nh):
            hl = slice(h * dk, (h + 1) * dk)
            gl = SM_A + h
            q, k = qf[:, hl], kf[:, hl]
            bh = head_lanes(beta_s, SM_BETA + h)
            egcs = head_lanes(egcs_s, gl)
            gamma = jnp.exp(jnp.where(lower, gcs[:, gl:gl + 1] - gcs_t[gl:gl + 1, :], NEG_BIG))
            kb = k * bh
            k16 = k.astype(BF16)
            a_mats.append(jnp.where(offdiag, _dot_nt(kb.astype(BF16), k16) * gamma, 0.0))
            yield
            rhs.append(jnp.concatenate([kb * egcs, vf[:, hl] * bh], axis=1).astype(BF16))
            qk.append((_dot_nt(q.astype(BF16), k16) * gamma).astype(BF16))
            yield
            qg.append((q * egcs).astype(BF16))
            kd.append((k * head_lanes(edec_s, gl)).astype(BF16))
            elast.append([head_lanes(elast_s[n * c:n * c + 1], gl) for n in range(ts // c)])
            yield

        eye = jnp.where(offdiag, 0.0, 1.0).astype(F32)
        pws = [(-a).astype(BF16) for a in a_mats]
        invs = [eye - a for a in a_mats]
        for _ in range(int(math.log2(c)) - 1):
            sq = [_dot(pw, pw) for pw in pws]
            yield
            pws = [x.astype(BF16) for x in sq]
            invs = [inv + _dot(inv.astype(BF16), pw) for inv, pw in zip(invs, pws)]
            yield
        wu = [_dot(invs[h].astype(BF16), rhs[h]) for h in range(nh)]
        yield

        pn = [[_dot_tn(kd[h][ch * c:(ch + 1) * c], wu[h][ch * c:(ch + 1) * c].astype(BF16))
               for ch in range(ts // c)] for h in range(nh)]
        yield
        states = [state_ref[r, h] for h in range(nh)]
        v_new = [[] for _ in range(nh)]
        o_state = [[] for _ in range(nh)]
        for ch in range(ts // c):
            sl = slice(ch * c, (ch + 1) * c)
            for h in range(nh):
                s16 = states[h].astype(BF16)
                states[h] = (states[h] * elast[h][ch] - _dot(pn[h][ch][:, :dk].astype(BF16), s16)
                             + pn[h][ch][:, dk:])
                lhs = jnp.concatenate([wu[h][sl, :dk].astype(BF16), qg[h][sl]], axis=0)
                ws = _dot(lhs, s16)
                v_new[h].append(wu[h][sl, dk:] - ws[:c])
                o_state[h].append(ws[c:])
            yield
        for h in range(nh):
            hl = slice(h * dk, (h + 1) * dk)
            state_ref[r, h] = states[h]
            vn_all = jnp.concatenate(v_new[h], axis=0).astype(BF16)
            o = jnp.concatenate(o_state[h], axis=0) + _dot(qk[h], vn_all)
            o = o * lax.rsqrt(jnp.mean(o * o, axis=-1, keepdims=True) + NORM_EPS) * nw_ref[...]
            o_ref[r, :, hl] = (o * _silu(gate_ref[r, :, hl].astype(F32))).astype(BF16)
            yield

    _interleave_staggered([row_work(r) for r in range(q_ref.shape[0])], DN_STAGGER)


def _dn_mixer(proj, small, p, l, batch, seq, ts):
    nb = seq // ts
    w = BRANCH_W
    nh, dk = DN_HEADS, DN_HEAD_DIM
    nr = math.gcd(batch, DN_SLOTS)
    proj3 = proj.reshape(batch, seq, proj.shape[1])
    small3 = small.reshape(batch, seq, LANES)
    rows = lambda col: pl.BlockSpec((nr, ts, w), lambda b, s, col=col: (b, s, col))
    conv = lambda n: pl.BlockSpec((None, DN_CONV, w), lambda b, s, n=n: (l, 0, n))
    out = pl.pallas_call(
        _dn_kernel,
        grid=(batch // nr, nb),
        in_specs=[rows(COL_DQ), rows(COL_DK), rows(COL_DV), rows(COL_DG),
                  pl.BlockSpec((nr, ts, LANES), lambda b, s: (b, s, 0)),
                  conv(0), conv(1), conv(2),
                  _layer_spec(l, (VEC_ROWS, LANES)), _layer_spec(l, (1, dk)),
                  _const_spec((ts, ts)), _const_spec((w, w))],
        out_specs=pl.BlockSpec((nr, ts, w), lambda b, s: (b, s, 0)),
        out_shape=jax.ShapeDtypeStruct((batch, seq, w), BF16),
        scratch_shapes=[pltpu.VMEM((nr, nh, dk, dk), F32),
                        pltpu.VMEM((nr, SUBLANES, w), F32), pltpu.VMEM((nr, SUBLANES, w), F32),
                        pltpu.VMEM((nr, SUBLANES, w), F32)],
        compiler_params=_cparams(("parallel", "arbitrary")),
        name="dn_mixer",
    )(proj3, proj3, proj3, proj3, small3, p["dn_conv_w"], p["dn_conv_w"], p["dn_conv_w"], p["vec"],
      p["dn_nw"], _block_diag(ts, DN_CHUNK, True), _block_diag(w, DN_HEAD_DIM, False))
    return out.reshape(batch * seq, w)


def _sg_kernel(u_ref, v_ref, g_ref, b_ref, w_ref, bs_ref, o_ref):
    ts = u_ref.shape[0]
    q = SG_CHUNK
    gd = BRANCH_W // SG_GROUPS
    u = jax.nn.gelu(u_ref[...].astype(F32))
    v = _layer_norm(jax.nn.gelu(v_ref[...].astype(F32)), g_ref[...], b_ref[...])
    ri = lax.broadcasted_iota(jnp.int32, (q, q), 0)
    ci = lax.broadcasted_iota(jnp.int32, (q, q), 1)
    lower = ci <= ri
    for g in range(SG_GROUPS):
        gl = slice(g * gd, (g + 1) * gd)
        wg = jnp.where(lower, w_ref[g], 0.0).astype(BF16)
        for c in range(ts // q):
            sl = slice(c * q, (c + 1) * q)
            mixed = _dot(wg, v[sl, gl].astype(BF16)) + bs_ref[:, gl]
            o_ref[sl, gl] = (u[sl, gl] * mixed).astype(BF16)


def _sg_mixer(proj, p, l, t, ts):
    w = BRANCH_W
    rows = lambda col: pl.BlockSpec((ts, w), lambda i, col=col: (i, col))
    return pl.pallas_call(
        _sg_kernel,
        grid=(t // ts,),
        in_specs=[rows(COL_SU), rows(COL_SV), _layer_spec(l, (1, w)), _layer_spec(l, (1, w)),
                  _layer_spec(l, (SG_GROUPS, SG_CHUNK, SG_CHUNK)), _layer_spec(l, (SG_CHUNK, w))],
        out_specs=pl.BlockSpec((ts, w), lambda i: (i, 0)),
        out_shape=jax.ShapeDtypeStruct((t, w), BF16),
        compiler_params=_cparams(("parallel",)),
        name="sg_mixer",
    )(proj, proj, p["sg_g"], p["sg_b"], p["sg_w"], p["sg_bs"])


FOX_SPLIT = 3
FOX_AUG = LANES // (2 * FOX_HEADS)
assert FOX_AUG >= 2 * FOX_SPLIT
LOG2E = math.log2(math.e)


def _fcum_kernel(sm_ref, k_ref, vec_ref, tri_ref, eq_ref, ek_ref, oq_ref, ok_ref, qa_ref, kt_ref, carry_ref):
    fb_ref = vec_ref.at[VEC_FOX_FB:VEC_FOX_FB + 1]
    @pl.when(pl.program_id(1) == 0)
    def _():
        carry_ref[...] = jnp.zeros_like(carry_ref)

    n = sm_ref.shape[0]
    per = LANES // FOX_HEAD_DIM
    ls = jax.nn.log_sigmoid(sm_ref[...] + fb_ref[...])
    c = _sel_left(tri_ref[...], ls, 3) + carry_ref[...]
    carry_ref[...] = c[n - 1:n, :]
    terms = _split_terms(c * LOG2E, FOX_SPLIT)
    cq = sum(_dot(t, eq_ref[i]) for i, t in enumerate(terms))
    qa_ref[...] = (oq_ref[...] + cq).astype(BF16)
    ck_t = sum(_dot_nt(ek_ref[i], t) for i, t in enumerate(terms))
    aug_t = (ok_ref[...] - ck_t).astype(BF16)
    pad = jnp.zeros((LANES - FOX_HEAD_DIM - FOX_AUG, n), BF16)
    for b in range(FOX_HEADS // per):
        k_t = k_ref[:, b * LANES:(b + 1) * LANES].astype(F32).T.astype(BF16)
        for e in range(per):
            h = b * per + e
            kt_ref[h] = jnp.concatenate([k_t[e * FOX_HEAD_DIM:(e + 1) * FOX_HEAD_DIM],
                                         aug_t[h * FOX_AUG:(h + 1) * FOX_AUG], pad], axis=0)


def _forget_cumsum(proj, small, p, l, batch, seq, tc):
    nb = seq // tc
    na = FOX_HEADS * FOX_AUG
    return pl.pallas_call(
        _fcum_kernel,
        grid=(batch, nb),
        in_specs=[pl.BlockSpec((tc, LANES), lambda b, s: (b * nb + s, 0)),
                  pl.BlockSpec((tc, BRANCH_W), lambda b, s: (b * nb + s, COL_FK)),
                  _layer_spec(l, (VEC_ROWS, LANES)), _const_spec((tc, tc)),
                  _const_spec((FOX_SPLIT, LANES, LANES)), _const_spec((FOX_SPLIT, na, LANES)),
                  _const_spec((1, LANES)), _const_spec((na, 1))],
        out_specs=[pl.BlockSpec((tc, LANES), lambda b, s: (b * nb + s, 0)),
                   pl.BlockSpec((None, FOX_HEADS, LANES, tc), lambda b, s: (b, 0, 0, s))],
        out_shape=[jax.ShapeDtypeStruct((batch * seq, LANES), BF16),
                   jax.ShapeDtypeStruct((batch, FOX_HEADS, LANES, seq), BF16)],
        scratch_shapes=[pltpu.VMEM((1, LANES), F32)],
        compiler_params=_cparams(("parallel", "arbitrary")),
        name="forget_cumsum",
    )(small, proj, p["vec"], p["tri_fox"], p["fox_eq"], p["fox_ek"], p["fox_oq"], p["fox_ok"])


def _roll_lanes(x, shift):
    shift %= x.shape[-1]
    return x if shift == 0 else pltpu.roll(x, shift, axis=x.ndim - 1)


def _fox_kernel(ii_ref, jj_ref, q_ref, kt_ref, v_ref, qa_ref, o_ref, qaug_ref, m_ref, acc_ref):
    t = pl.program_id(1)
    i = ii_ref[t]
    j = jj_ref[t]
    tq = q_ref.shape[0]
    per = LANES // FOX_HEAD_DIM
    lane = lax.broadcasted_iota(jnp.int32, (1, LANES), 1)

    @pl.when(j == 0)
    def _():
        m_ref[...] = jnp.full_like(m_ref, NEG_BIG)
        acc_ref[...] = jnp.zeros_like(acc_ref)
        qa = qa_ref[...].astype(F32)
        for h in range(FOX_HEADS):
            bl = slice(h // per * LANES, (h // per + 1) * LANES)
            q2 = (q_ref[:, bl].astype(F32) * (LOG2E * FOX_HEAD_DIM ** -0.5)).astype(BF16).astype(F32)
            q_own = _roll_lanes(q2, LANES - h % per * FOX_HEAD_DIM)
            extra = _roll_lanes(qa, FOX_HEAD_DIM - h * FOX_AUG)
            in_extra = (lane >= FOX_HEAD_DIM) & (lane < FOX_HEAD_DIM + FOX_AUG)
            qaug_ref[h] = jnp.where(lane < FOX_HEAD_DIM, q_own, jnp.where(in_extra, extra, 0.0)).astype(BF16)

    def logits(h):
        return _dot(qaug_ref[h], kt_ref[h])

    def softmax(h, s, masked):
        if masked:
            s = jnp.where(lax.broadcasted_iota(jnp.int32, (tq, tq), 1) <= lax.broadcasted_iota(jnp.int32, (tq, tq), 0),
                          s, NEG_BIG)
        m_old = m_ref[h]
        m_new = jnp.maximum(m_old, jnp.max(s, axis=-1, keepdims=True))
        m_ref[h] = m_new
        return jnp.exp2(s - m_new[:, :1]).astype(BF16), jnp.exp2(m_old - m_new)

    def accumulate(h, p, alpha):
        bl = slice(h // per * LANES, (h // per + 1) * LANES)
        v2 = v_ref[:, bl]
        vaug = jnp.concatenate([v2, jnp.ones_like(v2)], axis=1)
        acc_ref[h] = jnp.concatenate([alpha, alpha], axis=1) * acc_ref[h] + _dot(p, vaug)

    def step(masked):
        s_next = logits(0)
        for h in range(FOX_HEADS):
            s = s_next
            if h + 1 < FOX_HEADS:
                s_next = logits(h + 1)
            accumulate(h, *softmax(h, s, masked))

    @pl.when(j < i)
    def _():
        step(False)

    @pl.when(j == i)
    def _():
        step(True)
        for b in range(FOX_HEADS // per):
            outs = []
            for e in range(per):
                a = acc_ref[b * per + e]
                outs.append(a[:, :LANES] / a[:, LANES:])
            o_ref[:, b * LANES:(b + 1) * LANES] = jnp.where(lane // FOX_HEAD_DIM == 0, outs[0], outs[1]).astype(BF16)


def _fox_mixer(proj, qa, kt, batch, seq, tq):
    nq = seq // tq
    w = BRANCH_W
    pairs = [(i, j) for i in range(nq) for j in range(i + 1)]
    ii = jnp.asarray([p[0] for p in pairs], jnp.int32)
    jj = jnp.asarray([p[1] for p in pairs], jnp.int32)
    qrow = lambda b, t, ii, jj: b * nq + ii[t]
    krow = lambda b, t, ii, jj: b * nq + jj[t]
    grid_spec = pltpu.PrefetchScalarGridSpec(
        num_scalar_prefetch=2,
        grid=(batch, len(pairs)),
        in_specs=[pl.BlockSpec((tq, w), lambda b, t, ii, jj: (qrow(b, t, ii, jj), COL_FQ)),
                  pl.BlockSpec((None, FOX_HEADS, LANES, tq), lambda b, t, ii, jj: (b, 0, 0, jj[t])),
                  pl.BlockSpec((tq, w), lambda b, t, ii, jj: (krow(b, t, ii, jj), COL_FV)),
                  pl.BlockSpec((tq, LANES), lambda b, t, ii, jj: (qrow(b, t, ii, jj), 0))],
        out_specs=pl.BlockSpec((tq, w), lambda b, t, ii, jj: (qrow(b, t, ii, jj), 0)),
        scratch_shapes=[pltpu.VMEM((FOX_HEADS, tq, LANES), BF16),
                        pltpu.VMEM((FOX_HEADS, tq, LANES), F32),
                        pltpu.VMEM((FOX_HEADS, tq, 2 * LANES), F32)])
    return pl.pallas_call(
        _fox_kernel,
        grid_spec=grid_spec,
        out_shape=jax.ShapeDtypeStruct((batch * seq, w), BF16),
        compiler_params=_cparams(("parallel", "arbitrary")),
        name="fox_mixer",
    )(ii, jj, proj, kt, proj, qa)


def _merge_kernel(alpha, ya_ref, yb_ref, yc_ref, yd_ref, g0_ref, g1_ref, g2_ref, g3_ref, gb_ref,
                  wb_ref, wo_ref, h_ref, lg_ref, lb_ref, o32_ref, o16_ref):
    merged = None
    for i, (y_ref, gl_ref) in enumerate(((ya_ref, g0_ref), (yb_ref, g1_ref), (yc_ref, g2_ref), (yd_ref, g3_ref))):
        gate = _sigmoid(gl_ref[...].astype(F32) + gb_ref[i:i + 1, :])
        term = gate * _dot(y_ref[...], wb_ref[i])
        merged = term if merged is None else merged + term
    mix = _dot(merged.astype(BF16), wo_ref[...])
    y = _layer_norm(alpha * h_ref[...] + mix, lg_ref[...], lb_ref[...])
    o32_ref[...] = y
    o16_ref[...] = y.astype(BF16)


def _merge(ys, proj, h32, p, l, alpha, tm):
    t, d = h32.shape
    w = BRANCH_W
    gate0 = (COL_GATE * w) // d
    yspec = pl.BlockSpec((tm, w), lambda i: (i, 0))
    gspec = lambda n: pl.BlockSpec((tm, d), lambda i, n=n: (i, gate0 + n))
    row = pl.BlockSpec((tm, d), lambda i: (i, 0))
    return pl.pallas_call(
        functools.partial(_merge_kernel, alpha),
        grid=(t // tm,),
        in_specs=[yspec] * 4 + [gspec(n) for n in range(N_BRANCH)] +
                 [_layer_spec(l, (N_BRANCH, d)), _layer_spec(l, (N_BRANCH, w, d)), _layer_spec(l, (d, d)), row,
                  _layer_spec(l, (1, d)), _layer_spec(l, (1, d))],
        out_specs=[row, row],
        out_shape=[jax.ShapeDtypeStruct((t, d), F32), jax.ShapeDtypeStruct((t, d), BF16)],
        compiler_params=_cparams(("parallel",)),
        name="merge_norm",
    )(*ys, proj, proj, proj, proj, p["gate_b"], p["w_branch"], p["w_out"], h32, p["ln1_g"], p["ln1_b"])


def _ffn_kernel(alpha, h16_ref, h32_ref, wu_ref, wd_ref, lg_ref, lb_ref, o32_ref, o16_ref, acc_ref):
    f = pl.program_id(1)

    @pl.when(f == 0)
    def _():
        acc_ref[...] = jnp.zeros_like(acc_ref)

    up = jnp.maximum(_dot(h16_ref[...], wu_ref[...]), 0.0)
    acc_ref[...] += _dot((up * up).astype(BF16), wd_ref[...])

    @pl.when(f == pl.num_programs(1) - 1)
    def _():
        y = _layer_norm(alpha * h32_ref[...] + acc_ref[...], lg_ref[...], lb_ref[...])
        o32_ref[...] = y
        o16_ref[...] = y.astype(BF16)


def _ffn(h16, h32, p, l, alpha, tm, tf):
    t, d = h32.shape
    dff = p["w_up"].shape[2]
    row = pl.BlockSpec((tm, d), lambda i, f: (i, 0))
    return pl.pallas_call(
        functools.partial(_ffn_kernel, alpha),
        grid=(t // tm, dff // tf),
        in_specs=[row, row, pl.BlockSpec((None, d, tf), lambda i, f: (l, 0, f)),
                  pl.BlockSpec((None, tf, d), lambda i, f: (l, f, 0)),
                  _layer_spec(l, (1, d)), _layer_spec(l, (1, d))],
        out_specs=[row, row],
        out_shape=[jax.ShapeDtypeStruct((t, d), F32), jax.ShapeDtypeStruct((t, d), BF16)],
        scratch_shapes=[pltpu.VMEM((tm, d), F32)],
        compiler_params=_cparams(("parallel", "arbitrary")),
        name="ffn_norm",
    )(h16, h32, p["w_up"], p["w_down"], p["ln2_g"], p["ln2_b"])


VEC_SSD_DTB, VEC_SSD_ALOG, VEC_DN_ALOG, VEC_DN_DTB, VEC_FOX_FB, VEC_ROWS = 0, 1, 2, 3, 4, SUBLANES


def _lane_row(v, start):
    return jnp.pad(v.astype(F32), ((0, 0), (start, LANES - start - v.shape[1])))[:, None, :]


def _expander(start, heads, width):
    m = np.zeros((LANES, heads * width), np.float32)
    for h in range(heads):
        m[start + h, h * width:(h + 1) * width] = 1.0
    return jnp.asarray(m, BF16)


def _tri(n):
    return jnp.asarray(np.tril(np.ones((n, n), np.float32)), BF16)


def _block_diag(n, c, lower):
    blk = np.tril(np.ones((c, c), np.float32)) if lower else np.ones((c, c), np.float32)
    return jnp.asarray(np.kron(np.eye(n // c, dtype=np.float32), blk), BF16)


def _fox_aug_constants():
    na = FOX_HEADS * FOX_AUG
    eq = np.zeros((FOX_SPLIT, LANES, LANES), np.float32)
    ek = np.zeros((FOX_SPLIT, na, LANES), np.float32)
    oq = np.zeros((1, LANES), np.float32)
    ok = np.zeros((na, 1), np.float32)
    for h in range(FOX_HEADS):
        for i in range(FOX_SPLIT):
            ek[i, h * FOX_AUG + i, SM_F + h] = 1.0
            eq[i, SM_F + h, h * FOX_AUG + FOX_SPLIT + i] = 1.0
            oq[0, h * FOX_AUG + i] = 1.0
            ok[h * FOX_AUG + FOX_SPLIT + i, 0] = 1.0
    return jnp.asarray(eq, BF16), jnp.asarray(ek, BF16), jnp.asarray(oq), jnp.asarray(ok)


def _prepare_params(a, tc):
    w = BRANCH_W
    w_in = a["w_in"]
    depth, d = w_in.shape[:2]
    o_dt = w + SSD_XBC
    o_qkv = o_dt + SSD_HEADS
    o_beta = o_qkv + 3 * w
    o_a = o_beta + DN_HEADS
    o_gate = o_a + DN_HEADS
    o_f = o_gate + w + 2 * w + 3 * w
    o_gates = o_f + FOX_HEADS
    w_big = jnp.concatenate([w_in[..., :o_dt], w_in[..., o_qkv:o_beta], w_in[..., o_gate:o_f], w_in[..., o_gates:]],
                            axis=-1).astype(BF16)
    n_small = SSD_HEADS + 2 * DN_HEADS + FOX_HEADS
    assert (SM_DT, SM_BETA, SM_A, SM_F) == (0, SSD_HEADS, SSD_HEADS + DN_HEADS, SSD_HEADS + 2 * DN_HEADS)
    w_small = jnp.concatenate([w_in[..., o_dt:o_qkv], w_in[..., o_beta:o_a], w_in[..., o_a:o_gate],
                               w_in[..., o_f:o_gates], jnp.zeros((depth, d, LANES - n_small), w_in.dtype)],
                              axis=-1).astype(BF16)
    vec = jnp.concatenate([_lane_row(a["ssd_dt_bias"], SM_DT), _lane_row(a["ssd_a_log"], SM_DT),
                           _lane_row(a["dn_a_log"], SM_A), _lane_row(a["dn_dt_bias"], SM_A),
                           _lane_row(a["fox_f_bias"], SM_F),
                           jnp.zeros((depth, VEC_ROWS - 5, LANES), F32)], axis=1)
    fox_eq, fox_ek, fox_oq, fox_ok = _fox_aug_constants()
    row = lambda v: v[:, None, :]
    return {
        "w_big": w_big, "w_small": w_small, "vec": vec,
        "ssd_conv_w": a["ssd_conv_w"], "ssd_conv_b": row(a["ssd_conv_b"]), "dn_conv_w": a["dn_conv_w"],
        "ssd_dskip": row(jnp.repeat(a["ssd_d"], SSD_HEAD_DIM, axis=1)), "ssd_nw": row(a["ssd_norm_w"]),
        "e8": _expander(SM_DT, SSD_HEADS, SSD_HEAD_DIM), "tri_ssd": _tri(SSD_CHUNK),
        "dn_nw": row(a["dn_norm_w"]),
        "sg_g": row(a["sg_ln_g"]), "sg_b": row(a["sg_ln_b"]), "sg_w": a["sg_w"],
        "sg_bs": jnp.repeat(jnp.swapaxes(a["sg_b"], 1, 2), BRANCH_W // SG_GROUPS, axis=2),
        "tri_fox": _tri(tc), "fox_eq": fox_eq, "fox_ek": fox_ek, "fox_oq": fox_oq, "fox_ok": fox_ok,
        "gate_b": a["gate_b"], "w_branch": a["w_branch"].astype(BF16), "w_out": a["w_out"].astype(BF16),
        "ln1_g": row(a["ln1_g"]), "ln1_b": row(a["ln1_b"]),
        "w_up": a["w_up"].astype(BF16), "w_down": a["w_down"].astype(BF16),
        "ln2_g": row(a["ln2_g"]), "ln2_b": row(a["ln2_b"]),
    }


def _tiles(batch, seq):
    t = batch * seq
    pick = lambda n, cap: math.gcd(n, cap)
    return {
        "ln": pick(t, 1024), "proj_m": pick(t, 2048), "proj_n": 2048, "ssd": pick(seq, 512), "dn": pick(seq, 256),
        "sg": pick(t, 2048), "fcum": pick(seq, 1024), "fox": pick(seq, 512), "merge": pick(t, 1024),
        "ffn_m": pick(t, 1024), "ffn_f": 1024,
    }


def kernel(x, ln_in_g, ln_in_b, w_in, ssd_conv_w, ssd_conv_b, ssd_dt_bias, ssd_a_log, ssd_d, ssd_norm_w, dn_conv_w,
           dn_a_log, dn_dt_bias, dn_norm_w, sg_ln_g, sg_ln_b, sg_w, sg_b, fox_f_bias, gate_b, w_branch, w_out,
           ln1_g, ln1_b, w_up, w_down, ln2_g, ln2_b):
    batch, seq, d = x.shape
    depth = w_in.shape[0]
    alpha = (2 * depth) ** 0.25
    a = dict(w_in=w_in, ssd_conv_w=ssd_conv_w, ssd_conv_b=ssd_conv_b, ssd_dt_bias=ssd_dt_bias, ssd_a_log=ssd_a_log,
             ssd_d=ssd_d, ssd_norm_w=ssd_norm_w, dn_conv_w=dn_conv_w, dn_a_log=dn_a_log, dn_dt_bias=dn_dt_bias,
             dn_norm_w=dn_norm_w, sg_ln_g=sg_ln_g, sg_ln_b=sg_ln_b, sg_w=sg_w, sg_b=sg_b, fox_f_bias=fox_f_bias,
             gate_b=gate_b, w_branch=w_branch, w_out=w_out, ln1_g=ln1_g, ln1_b=ln1_b, w_up=w_up, w_down=w_down,
             ln2_g=ln2_g, ln2_b=ln2_b)
    tl = _tiles(batch, seq)
    t = batch * seq
    h32, h16 = _entry_norm(x.reshape(t, d), ln_in_g, ln_in_b, tl["ln"])
    p = _prepare_params(a, tl["fcum"])
    for l in range(depth):
        proj, small = _in_proj(h16, p, l, tl["proj_m"], tl["proj_n"])
        y_a = _ssd_mixer(proj, small, p, l, batch, seq, tl["ssd"])
        y_b = _dn_mixer(proj, small, p, l, batch, seq, tl["dn"])
        y_c = _sg_mixer(proj, p, l, t, tl["sg"])
        qa, kt = _forget_cumsum(proj, small, p, l, batch, seq, tl["fcum"])
        y_d = _fox_mixer(proj, qa, kt, batch, seq, tl["fox"])
        h32, h16 = _merge((y_a, y_b, y_c, y_d), proj, h32, p, l, alpha, tl["merge"])
        h32, h16 = _ffn(h16, h32, p, l, alpha, tl["ffn_m"], tl["ffn_f"])
    return h32.reshape(batch, seq, d)
```

```python
import functools
import math

import numpy as np
import jax
import jax.numpy as jnp
from jax import lax
from jax.experimental import pallas as pl
from jax.experimental.pallas import tpu as pltpu

F32 = jnp.float32
BF16 = jnp.bfloat16

BRANCH_W = 512
N_BRANCH = 4
SSD_HEADS, SSD_HEAD_DIM, SSD_GROUPS, SSD_STATE, SSD_CONV = 8, 64, 2, 128, 4
SSD_XBC = BRANCH_W + 2 * SSD_GROUPS * SSD_STATE
DN_HEADS, DN_HEAD_DIM, DN_CONV = 4, 128, 4
SG_GROUPS, SG_CHUNK = 4, 128
FOX_HEADS, FOX_HEAD_DIM = 8, 64
LN_EPS = 1e-5
NORM_EPS = 1e-6
NEG_BIG = -1e30

LANES = 128
SUBLANES = 8
VMEM_LIMIT_BYTES = 56 * 1024 * 1024

SM_DT, SM_BETA, SM_A, SM_F = 0, 8, 12, 16

SSD_CHUNK = 128
DN_CHUNK = 64


def _dot(a, b):
    return jnp.dot(a, b, preferred_element_type=F32)


def _dot_nt(a, b):
    return lax.dot_general(a, b, (((1,), (1,)), ((), ())), preferred_element_type=F32)


def _dot_tn(a, b):
    return lax.dot_general(a, b, (((0,), (0,)), ((), ())), preferred_element_type=F32)


def _bdot(a, b):
    return _dot(a.astype(BF16), b.astype(BF16))


def _split_terms(x, n):
    terms, r = [], x
    for i in range(n):
        p = r.astype(BF16)
        terms.append(p)
        if i + 1 < n:
            r = r - p.astype(F32)
    return terms


def _sel_right(x, m, n):
    return sum(_dot(p, m) for p in _split_terms(x, n))


def _sel_left(m, x, n):
    return sum(_dot(m, p) for p in _split_terms(x, n))


def _sigmoid(x):
    return 0.5 * jnp.tanh(0.5 * x) + 0.5


def _silu(x):
    hx = 0.5 * x
    return hx + hx * jnp.tanh(hx)


def _softplus(x):
    return jnp.maximum(x, 0.0) + jnp.log1p(jnp.exp(-jnp.abs(x)))


def _layer_norm(x, g, b):
    mu = jnp.mean(x, axis=-1, keepdims=True)
    xc = x - mu
    var = jnp.mean(xc * xc, axis=-1, keepdims=True)
    return xc * lax.rsqrt(var + LN_EPS) * g + b


def _causal_conv(x, tail, w):
    n = x.shape[0]
    k = w.shape[0]
    row8 = lax.broadcasted_iota(jnp.int32, (SUBLANES, x.shape[1]), 0)
    acc = x * w[k - 1:k, :]
    for s in range(1, k):
        xr = pltpu.roll(x, s, axis=0)
        pr = pltpu.roll(tail, s, axis=0)
        head = jnp.where(row8 < s, pr, xr[:SUBLANES])
        xs = jnp.concatenate([head, xr[SUBLANES:]], axis=0)
        acc = acc + xs * w[k - 1 - s:k - s, :]
    return acc, x[n - SUBLANES:]


def _cparams(sem):
    return pltpu.CompilerParams(dimension_semantics=sem, vmem_limit_bytes=VMEM_LIMIT_BYTES)


def _const_spec(shape):
    nd = len(shape)
    return pl.BlockSpec(shape, lambda *_: (0,) * nd)


def _ln_kernel(x_ref, g_ref, b_ref, o32_ref, o16_ref):
    y = _layer_norm(x_ref[...], g_ref[...], b_ref[...])
    o32_ref[...] = y
    o16_ref[...] = y.astype(BF16)


def _entry_norm(x2, g, b, tm):
    t, d = x2.shape
    row = pl.BlockSpec((tm, d), lambda i: (i, 0))
    return pl.pallas_call(
        _ln_kernel,
        grid=(t // tm,),
        in_specs=[row, _const_spec((1, d)), _const_spec((1, d))],
        out_specs=[row, row],
        out_shape=[jax.ShapeDtypeStruct((t, d), F32), jax.ShapeDtypeStruct((t, d), BF16)],
        compiler_params=_cparams(("parallel",)),
        name="entry_norm",
    )(x2, g.reshape(1, d), b.reshape(1, d))


COL_Z, COL_X, COL_BC, COL_DQ, COL_DK, COL_DV, COL_DG, COL_SU, COL_SV, COL_FQ, COL_FK, COL_FV, COL_GATE = range(13)


def _proj_kernel(h_ref, w_ref, ws_ref, o_ref, os_ref):
    h = h_ref[...]
    o_ref[...] = _dot(h, w_ref[...]).astype(BF16)

    @pl.when(pl.program_id(1) == 0)
    def _():
        os_ref[...] = _dot(h, ws_ref[...])


def _layer_spec(l, shape):
    nd = len(shape)
    return pl.BlockSpec((None,) + tuple(shape), lambda *_: (l,) + (0,) * nd)


def _in_proj(h16, p, l, tm, tn):
    t, d = h16.shape
    w_big, w_small = p["w_big"], p["w_small"]
    n = w_big.shape[2]
    return pl.pallas_call(
        _proj_kernel,
        grid=(t // tm, n // tn),
        in_specs=[pl.BlockSpec((tm, d), lambda i, j: (i, 0)),
                  pl.BlockSpec((None, d, tn), lambda i, j: (l, 0, j)),
                  _layer_spec(l, (d, LANES))],
        out_specs=[pl.BlockSpec((tm, tn), lambda i, j: (i, j)),
                   pl.BlockSpec((tm, LANES), lambda i, j: (i, 0))],
        out_shape=[jax.ShapeDtypeStruct((t, n), BF16), jax.ShapeDtypeStruct((t, LANES), F32)],
        compiler_params=_cparams(("parallel", "arbitrary")),
        name="in_proj",
    )(h16, w_big, w_small)


def _ssd_kernel(z_ref, x_ref, bc_ref, sm_ref, cwx_ref, cwbc_ref, cbx_ref, cbbc_ref, vec_ref,
                dskip_ref, nw_ref, e8_ref, tri_ref, o_ref, state_ref, xtail_ref, bctail_ref):
    dtb_ref = vec_ref.at[VEC_SSD_DTB:VEC_SSD_DTB + 1]
    alog_ref = vec_ref.at[VEC_SSD_ALOG:VEC_SSD_ALOG + 1]
    first = pl.program_id(1) == 0
    ts = x_ref.shape[0]
    q = SSD_CHUNK
    hp = SSD_HEADS // SSD_GROUPS * SSD_HEAD_DIM
    e8 = e8_ref[...]
    tri = tri_ref[...]

    @pl.when(first)
    def _():
        state_ref[...] = jnp.zeros_like(state_ref)
        xtail_ref[...] = jnp.zeros_like(xtail_ref)
        bctail_ref[...] = jnp.zeros_like(bctail_ref)

    xs, xtail_ref[...] = _causal_conv(x_ref[...].astype(F32), xtail_ref[...], cwx_ref[...])
    xs = _silu(xs + cbx_ref[...])
    bcm, bctail_ref[...] = _causal_conv(bc_ref[...].astype(F32), bctail_ref[...], cwbc_ref[...])
    bcm = _silu(bcm + cbbc_ref[...]).astype(BF16)
    lane = lax.broadcasted_iota(jnp.int32, (1, LANES), 1)
    head_lane = (lane >= SM_DT) & (lane < SM_DT + SSD_HEADS)
    a_neg = jnp.where(head_lane, -jnp.exp(alog_ref[...]), 0.0)
    dt = _softplus(sm_ref[...] + dtb_ref[...])
    dta = dt * a_neg
    xdt = xs * _sel_right(dt, e8, 2)
    ri = lax.broadcasted_iota(jnp.int32, (q, q), 0)
    ci = lax.broadcasted_iota(jnp.int32, (q, q), 1)
    lower = ci <= ri
    lane_g = lax.broadcasted_iota(jnp.int32, (1, hp), 1) // SSD_HEAD_DIM

    for c in range(ts // q):
        sl = slice(c * q, (c + 1) * q)
        acs = _sel_left(tri, dta[sl], 3)
        acs_t = acs.T
        eacs = _sel_right(jnp.exp(acs), e8, 2)
        dec_end = _sel_right(jnp.exp(acs[q - 1:q, :] - acs), e8, 2)
        ys = []
        for g in range(SSD_GROUPS):
            gl = slice(g * hp, (g + 1) * hp)
            bm = bcm[sl, g * SSD_STATE:(g + 1) * SSD_STATE]
            cm = bcm[sl, (SSD_GROUPS + g) * SSD_STATE:(SSD_GROUPS + g + 1) * SSD_STATE]
            cb = _dot_nt(cm, bm)
            xg = xdt[sl, gl]
            s_prev = state_ref[g]
            y = _dot(cm, s_prev.astype(BF16)) * eacs[:, gl]
            for e in range(SSD_HEADS // SSD_GROUPS):
                h = SM_DT + g * (SSD_HEADS // SSD_GROUPS) + e
                seg = jnp.exp(jnp.where(lower, acs[:, h:h + 1] - acs_t[h:h + 1, :], NEG_BIG))
                xe = jnp.where(lane_g == e, xg, 0.0).astype(BF16)
                y = y + _dot((cb * seg).astype(BF16), xe)
            state_ref[g] = s_prev * eacs[q - 1:q, gl] + _dot_tn(bm, (xg * dec_end[:, gl]).astype(BF16))
            ys.append(y)
        y = jnp.concatenate(ys, axis=1) + dskip_ref[...] * xs[sl]
        y = y * _silu(z_ref[sl, :].astype(F32))
        y = y * lax.rsqrt(jnp.mean(y * y, axis=-1, keepdims=True) + NORM_EPS) * nw_ref[...]
        o_ref[sl, :] = y.astype(BF16)


def _ssd_mixer(proj, small, p, l, batch, seq, ts):
    nb = seq // ts
    w = BRANCH_W
    rows = lambda col: pl.BlockSpec((ts, w), lambda b, s, col=col: (b * nb + s, col))
    half = lambda r, n: pl.BlockSpec((None, r, w), lambda b, s, n=n: (l, 0, n))
    hp = SSD_HEADS // SSD_GROUPS * SSD_HEAD_DIM
    return pl.pallas_call(
        _ssd_kernel,
        grid=(batch, nb),
        in_specs=[rows(COL_Z), rows(COL_X), rows(COL_BC),
                  pl.BlockSpec((ts, LANES), lambda b, s: (b * nb + s, 0)),
                  half(SSD_CONV, 0), half(SSD_CONV, 1), half(1, 0), half(1, 1),
                  _layer_spec(l, (VEC_ROWS, LANES)),
                  _layer_spec(l, (1, w)), _layer_spec(l, (1, w)),
                  _const_spec((LANES, w)), _const_spec((SSD_CHUNK, SSD_CHUNK))],
        out_specs=pl.BlockSpec((ts, w), lambda b, s: (b * nb + s, 0)),
        out_shape=jax.ShapeDtypeStruct((batch * seq, w), BF16),
        scratch_shapes=[pltpu.VMEM((SSD_GROUPS, SSD_STATE, hp), F32),
                        pltpu.VMEM((SUBLANES, w), F32), pltpu.VMEM((SUBLANES, w), F32)],
        compiler_params=_cparams(("parallel", "arbitrary")),
        name="ssd_mixer",
    )(proj, proj, proj, small, p["ssd_conv_w"], p["ssd_conv_w"], p["ssd_conv_b"], p["ssd_conv_b"], p["vec"],
      p["ssd_dskip"], p["ssd_nw"], p["e8"], p["tri_ssd"])


DN_SLOTS = 2
DN_STAGGER = 9


def _interleave_staggered(gens, stagger):
    active, waiting, rounds = [], list(gens), 0
    while active or waiting:
        if waiting and rounds % stagger == 0:
            active.append(waiting.pop(0))
        for g in list(active):
            try:
                next(g)
            except StopIteration:
                active.remove(g)
        rounds += 1


def _dn_kernel(q_ref, k_ref, v_ref, gate_ref, sm_ref, cwq_ref, cwk_ref, cwv_ref, vec_ref, nw_ref,
               tri_ref, hsum_ref, o_ref, state_ref, qtail_ref, ktail_ref, vtail_ref):
    alog_ref = vec_ref.at[VEC_DN_ALOG:VEC_DN_ALOG + 1]
    dtb_ref = vec_ref.at[VEC_DN_DTB:VEC_DN_DTB + 1]
    ts = q_ref.shape[1]
    c = DN_CHUNK
    dk = DN_HEAD_DIM
    nh = DN_HEADS

    @pl.when(pl.program_id(1) == 0)
    def _():
        state_ref[...] = jnp.zeros_like(state_ref)
        qtail_ref[...] = jnp.zeros_like(qtail_ref)
        ktail_ref[...] = jnp.zeros_like(ktail_ref)
        vtail_ref[...] = jnp.zeros_like(vtail_ref)

    def row_work(r):
        qf, qtail_ref[r] = _causal_conv(q_ref[r].astype(F32), qtail_ref[r], cwq_ref[...])
        qf = _silu(qf)
        yield
        kf, ktail_ref[r] = _causal_conv(k_ref[r].astype(F32), ktail_ref[r], cwk_ref[...])
        kf = _silu(kf)
        yield
        vf, vtail_ref[r] = _causal_conv(v_ref[r].astype(F32), vtail_ref[r], cwv_ref[...])
        vf = _silu(vf)
        yield
        qf = qf * (lax.rsqrt(_sel_right(qf * qf, hsum_ref[...], 1) + NORM_EPS) * (dk ** -0.5))
        kf = kf * lax.rsqrt(_sel_right(kf * kf, hsum_ref[...], 1) + NORM_EPS)
        yield
        sm = sm_ref[r]
        lane = lax.broadcasted_iota(jnp.int32, (1, LANES), 1)
        a_lane = (lane >= SM_A) & (lane < SM_A + nh)
        neg_a = jnp.where(a_lane, -jnp.exp(alog_ref[...]), 0.0)
        gdec = neg_a * _softplus(sm + dtb_ref[...])
        beta_s = _sigmoid(sm)
        gcs = _sel_left(tri_ref[...], gdec, 3)
        gtot = jnp.concatenate([jnp.broadcast_to(gcs[(n + 1) * c - 1:(n + 1) * c], (c, LANES))
                                for n in range(ts // c)], axis=0)
        gcs_t = gcs.T
        egcs_s = jnp.exp(gcs)
        edec_s = jnp.exp(gtot - gcs)
        elast_s = jnp.exp(gtot)
        ri = lax.broadcasted_iota(jnp.int32, (ts, ts), 0)
        ci = lax.broadcasted_iota(jnp.int32, (ts, ts), 1)
        lower = (ci >= ri // c * c) & (ci <= ri)
        offdiag = ci != ri
        yield

        def head_lanes(x, lane_idx):
            return jnp.broadcast_to(x[:, lane_idx:lane_idx + 1], (x.shape[0], dk))

        a_mats, rhs, qk, qg, kd, elast = [], [], [], [], [], []
        for h in range(nh):
            hl = slice(h * dk, (h + 1) * dk)
            gl = SM_A + h
            q, k = qf[:, hl], kf[:, hl]
            bh = head_lanes(beta_s, SM_BETA + h)
            egcs = head_lanes(egcs_s, gl)
            gamma = jnp.exp(jnp.where(lower, gcs[:, gl:gl + 1] - gcs_t[gl:gl + 1, :], NEG_BIG))
            kb = k * bh
            k16 = k.astype(BF16)
            a_mats.append(jnp.where(offdiag, _dot_nt(kb.astype(BF16), k16) * gamma, 0.0))
            yield
            rhs.append(jnp.concatenate([kb * egcs, vf[:, hl] * bh], axis=1).astype(BF16))
            qk.append((_dot_nt(q.astype(BF16), k16) * gamma).astype(BF16))
            yield
            qg.append((q * egcs).astype(BF16))
            kd.append((k * head_lanes(edec_s, gl)).astype(BF16))
            elast.append([head_lanes(elast_s[n * c:n * c + 1], gl) for n in range(ts // c)])
            yield

        eye = jnp.where(offdiag, 0.0, 1.0).astype(F32)
        pws = [(-a).astype(BF16) for a in a_mats]
        invs = [eye - a for a in a_mats]
        for _ in range(int(math.log2(c)) - 1):
            sq = [_dot(pw, pw) for pw in pws]
            yield
            pws = [x.astype(BF16) for x in sq]
            invs = [inv + _dot(inv.astype(BF16), pw) for inv, pw in zip(invs, pws)]
            yield
        wu = [_dot(invs[h].astype(BF16), rhs[h]) for h in range(nh)]
        yield

        pn = [[_dot_tn(kd[h][ch * c:(ch + 1) * c], wu[h][ch * c:(ch + 1) * c].astype(BF16))
               for ch in range(ts // c)] for h in range(nh)]
        yield
        states = [state_ref[r, h] for h in range(nh)]
        v_new = [[] for _ in range(nh)]
        o_state = [[] for _ in range(nh)]
        for ch in range(ts // c):
            sl = slice(ch * c, (ch + 1) * c)
            for h in range(nh):
                s16 = states[h].astype(BF16)
                states[h] = (states[h] * elast[h][ch] - _dot(pn[h][ch][:, :dk].astype(BF16), s16)
                             + pn[h][ch][:, dk:])
                lhs = jnp.concatenate([wu[h][sl, :dk].astype(BF16), qg[h][sl]], axis=0)
                ws = _dot(lhs, s16)
                v_new[h].append(wu[h][sl, dk:] - ws[:c])
                o_state[h].append(ws[c:])
            yield
        for h in range(nh):
            hl = slice(h * dk, (h + 1) * dk)
            state_ref[r, h] = states[h]
            vn_all = jnp.concatenate(v_new[h], axis=0).astype(BF16)
            o = jnp.concatenate(o_state[h], axis=0) + _dot(qk[h], vn_all)
            o = o * lax.rsqrt(jnp.mean(o * o, axis=-1, keepdims=True) + NORM_EPS) * nw_ref[...]
            o_ref[r, :, hl] = (o * _silu(gate_ref[r, :, hl].astype(F32))).astype(BF16)
            yield

    _interleave_staggered([row_work(r) for r in range(q_ref.shape[0])], DN_STAGGER)


def _dn_mixer(proj, small, p, l, batch, seq, ts):
    nb = seq // ts
    w = BRANCH_W
    nh, dk = DN_HEADS, DN_HEAD_DIM
    nr = math.gcd(batch, DN_SLOTS)
    proj3 = proj.reshape(batch, seq, proj.shape[1])
    small3 = small.reshape(batch, seq, LANES)
    rows = lambda col: pl.BlockSpec((nr, ts, w), lambda b, s, col=col: (b, s, col))
    conv = lambda n: pl.BlockSpec((None, DN_CONV, w), lambda b, s, n=n: (l, 0, n))
    out = pl.pallas_call(
        _dn_kernel,
        grid=(batch // nr, nb),
        in_specs=[rows(COL_DQ), rows(COL_DK), rows(COL_DV), rows(COL_DG),
                  pl.BlockSpec((nr, ts, LANES), lambda b, s: (b, s, 0)),
                  conv(0), conv(1), conv(2),
                  _layer_spec(l, (VEC_ROWS, LANES)), _layer_spec(l, (1, dk)),
                  _const_spec((ts, ts)), _const_spec((w, w))],
        out_specs=pl.BlockSpec((nr, ts, w), lambda b, s: (b, s, 0)),
        out_shape=jax.ShapeDtypeStruct((batch, seq, w), BF16),
        scratch_shapes=[pltpu.VMEM((nr, nh, dk, dk), F32),
                        pltpu.VMEM((nr, SUBLANES, w), F32), pltpu.VMEM((nr, SUBLANES, w), F32),
                        pltpu.VMEM((nr, SUBLANES, w), F32)],
        compiler_params=_cparams(("parallel", "arbitrary")),
        name="dn_mixer",
    )(proj3, proj3, proj3, proj3, small3, p["dn_conv_w"], p["dn_conv_w"], p["dn_conv_w"], p["vec"],
      p["dn_nw"], _block_diag(ts, DN_CHUNK, True), _block_diag(w, DN_HEAD_DIM, False))
    return out.reshape(batch * seq, w)


def _sg_kernel(u_ref, v_ref, g_ref, b_ref, w_ref, bs_ref, o_ref):
    ts = u_ref.shape[0]
    q = SG_CHUNK
    gd = BRANCH_W // SG_GROUPS
    u = jax.nn.gelu(u_ref[...].astype(F32))
    v = _layer_norm(jax.nn.gelu(v_ref[...].astype(F32)), g_ref[...], b_ref[...])
    ri = lax.broadcasted_iota(jnp.int32, (q, q), 0)
    ci = lax.broadcasted_iota(jnp.int32, (q, q), 1)
    lower = ci <= ri
    for g in range(SG_GROUPS):
        gl = slice(g * gd, (g + 1) * gd)
        wg = jnp.where(lower, w_ref[g], 0.0).astype(BF16)
        for c in range(ts // q):
            sl = slice(c * q, (c + 1) * q)
            mixed = _dot(wg, v[sl, gl].astype(BF16)) + bs_ref[:, gl]
            o_ref[sl, gl] = (u[sl, gl] * mixed).astype(BF16)


def _sg_mixer(proj, p, l, t, ts):
    w = BRANCH_W
    rows = lambda col: pl.BlockSpec((ts, w), lambda i, col=col: (i, col))
    return pl.pallas_call(
        _sg_kernel,
        grid=(t // ts,),
        in_specs=[rows(COL_SU), rows(COL_SV), _layer_spec(l, (1, w)), _layer_spec(l, (1, w)),
                  _layer_spec(l, (SG_GROUPS, SG_CHUNK, SG_CHUNK)), _layer_spec(l, (SG_CHUNK, w))],
        out_specs=pl.BlockSpec((ts, w), lambda i: (i, 0)),
        out_shape=jax.ShapeDtypeStruct((t, w), BF16),
        compiler_params=_cparams(("parallel",)),
        name="sg_mixer",
    )(proj, proj, p["sg_g"], p["sg_b"], p["sg_w"], p["sg_bs"])


FOX_SPLIT = 3
FOX_CUM_CHUNK = 128
FOX_AUG = LANES // (2 * FOX_HEADS)
assert FOX_AUG >= 2 * FOX_SPLIT
LOG2E = math.log2(math.e)


def _fcum_kernel(sm_ref, k_ref, vec_ref, tri_ref, eq_ref, ek_ref, oq_ref, ok_ref, qa_ref, kt_ref, carry_ref):
    fb_ref = vec_ref.at[VEC_FOX_FB:VEC_FOX_FB + 1]
    @pl.when(pl.program_id(1) == 0)
    def _():
        carry_ref[...] = jnp.zeros_like(carry_ref)

    n = sm_ref.shape[0]
    per = LANES // FOX_HEAD_DIM
    ls = jax.nn.log_sigmoid(sm_ref[...] + fb_ref[...])
    tri = tri_ref[...]
    carry = carry_ref[...]
    pieces = []
    for r in range(n // FOX_CUM_CHUNK):
        piece = _sel_left(tri, ls[r * FOX_CUM_CHUNK:(r + 1) * FOX_CUM_CHUNK], 3) + carry
        carry = piece[FOX_CUM_CHUNK - 1:FOX_CUM_CHUNK, :]
        pieces.append(piece)
    c = jnp.concatenate(pieces, axis=0)
    carry_ref[...] = carry
    terms = _split_terms(c * LOG2E, FOX_SPLIT)
    cq = sum(_dot(t, eq_ref[i]) for i, t in enumerate(terms))
    qa_ref[...] = (oq_ref[...] + cq).astype(BF16)
    ck_t = sum(_dot_nt(ek_ref[i], t) for i, t in enumerate(terms))
    aug_t = (ok_ref[...] - ck_t).astype(BF16)
    pad = jnp.zeros((LANES - FOX_HEAD_DIM - FOX_AUG, n), BF16)
    for b in range(FOX_HEADS // per):
        k_t = k_ref[:, b * LANES:(b + 1) * LANES].astype(F32).T.astype(BF16)
        for e in range(per):
            h = b * per + e
            kt_ref[h] = jnp.concatenate([k_t[e * FOX_HEAD_DIM:(e + 1) * FOX_HEAD_DIM],
                                         aug_t[h * FOX_AUG:(h + 1) * FOX_AUG], pad], axis=0)


def _forget_cumsum(proj, small, p, l, batch, seq, tc):
    nb = seq // tc
    na = FOX_HEADS * FOX_AUG
    return pl.pallas_call(
        _fcum_kernel,
        grid=(batch, nb),
        in_specs=[pl.BlockSpec((tc, LANES), lambda b, s: (b * nb + s, 0)),
                  pl.BlockSpec((tc, BRANCH_W), lambda b, s: (b * nb + s, COL_FK)),
                  _layer_spec(l, (VEC_ROWS, LANES)), _const_spec((FOX_CUM_CHUNK, FOX_CUM_CHUNK)),
                  _const_spec((FOX_SPLIT, LANES, LANES)), _const_spec((FOX_SPLIT, na, LANES)),
                  _const_spec((1, LANES)), _const_spec((na, 1))],
        out_specs=[pl.BlockSpec((tc, LANES), lambda b, s: (b * nb + s, 0)),
                   pl.BlockSpec((None, FOX_HEADS, LANES, tc), lambda b, s: (b, 0, 0, s))],
        out_shape=[jax.ShapeDtypeStruct((batch * seq, LANES), BF16),
                   jax.ShapeDtypeStruct((batch, FOX_HEADS, LANES, seq), BF16)],
        scratch_shapes=[pltpu.VMEM((1, LANES), F32)],
        compiler_params=_cparams(("parallel", "arbitrary")),
        name="forget_cumsum",
    )(small, proj, p["vec"], p["tri_fox"], p["fox_eq"], p["fox_ek"], p["fox_oq"], p["fox_ok"])


def _roll_lanes(x, shift):
    shift %= x.shape[-1]
    return x if shift == 0 else pltpu.roll(x, shift, axis=x.ndim - 1)


def _fox_kernel(ii_ref, jj_ref, q_ref, kt_ref, v_ref, qa_ref, o_ref, qaug_ref, m_ref, acc_ref):
    t = pl.program_id(1)
    i = ii_ref[t]
    j = jj_ref[t]
    tq = q_ref.shape[0]
    per = LANES // FOX_HEAD_DIM
    lane = lax.broadcasted_iota(jnp.int32, (1, LANES), 1)

    @pl.when(j == 0)
    def _():
        m_ref[...] = jnp.full_like(m_ref, NEG_BIG)
        acc_ref[...] = jnp.zeros_like(acc_ref)
        qa = qa_ref[...].astype(F32)
        for h in range(FOX_HEADS):
            bl = slice(h // per * LANES, (h // per + 1) * LANES)
            q2 = (q_ref[:, bl].astype(F32) * (LOG2E * FOX_HEAD_DIM ** -0.5)).astype(BF16).astype(F32)
            q_own = _roll_lanes(q2, LANES - h % per * FOX_HEAD_DIM)
            extra = _roll_lanes(qa, FOX_HEAD_DIM - h * FOX_AUG)
            in_extra = (lane >= FOX_HEAD_DIM) & (lane < FOX_HEAD_DIM + FOX_AUG)
            qaug_ref[h] = jnp.where(lane < FOX_HEAD_DIM, q_own, jnp.where(in_extra, extra, 0.0)).astype(BF16)

    def logits(h):
        return _dot(qaug_ref[h], kt_ref[h])

    def softmax(h, s, masked):
        if masked:
            s = jnp.where(lax.broadcasted_iota(jnp.int32, (tq, tq), 1) <= lax.broadcasted_iota(jnp.int32, (tq, tq), 0),
                          s, NEG_BIG)
        m_old = m_ref[h]
        m_new = jnp.maximum(m_old, jnp.max(s, axis=-1, keepdims=True))
        m_ref[h] = m_new
        return jnp.exp2(s - m_new[:, :1]).astype(BF16), jnp.exp2(m_old - m_new)

    def accumulate(h, p, alpha):
        bl = slice(h // per * LANES, (h // per + 1) * LANES)
        v2 = v_ref[:, bl]
        vaug = jnp.concatenate([v2, jnp.ones_like(v2)], axis=1)
        acc_ref[h] = jnp.concatenate([alpha, alpha], axis=1) * acc_ref[h] + _dot(p, vaug)

    def step(masked):
        s_next = logits(0)
        for h in range(FOX_HEADS):
            s = s_next
            if h + 1 < FOX_HEADS:
                s_next = logits(h + 1)
            accumulate(h, *softmax(h, s, masked))

    @pl.when(j < i)
    def _():
        step(False)

    @pl.when(j == i)
    def _():
        step(True)
        for b in range(FOX_HEADS // per):
            outs = []
            for e in range(per):
                a = acc_ref[b * per + e]
                outs.append(a[:, :LANES] / a[:, LANES:])
            o_ref[:, b * LANES:(b + 1) * LANES] = jnp.where(lane // FOX_HEAD_DIM == 0, outs[0], outs[1]).astype(BF16)


def _fox_mixer(proj, qa, kt, batch, seq, tq):
    nq = seq // tq
    w = BRANCH_W
    pairs = [(i, j) for i in range(nq) for j in range(i + 1)]
    ii = jnp.asarray([p[0] for p in pairs], jnp.int32)
    jj = jnp.asarray([p[1] for p in pairs], jnp.int32)
    qrow = lambda b, t, ii, jj: b * nq + ii[t]
    krow = lambda b, t, ii, jj: b * nq + jj[t]
    grid_spec = pltpu.PrefetchScalarGridSpec(
        num_scalar_prefetch=2,
        grid=(batch, len(pairs)),
        in_specs=[pl.BlockSpec((tq, w), lambda b, t, ii, jj: (qrow(b, t, ii, jj), COL_FQ)),
                  pl.BlockSpec((None, FOX_HEADS, LANES, tq), lambda b, t, ii, jj: (b, 0, 0, jj[t])),
                  pl.BlockSpec((tq, w), lambda b, t, ii, jj: (krow(b, t, ii, jj), COL_FV)),
                  pl.BlockSpec((tq, LANES), lambda b, t, ii, jj: (qrow(b, t, ii, jj), 0))],
        out_specs=pl.BlockSpec((tq, w), lambda b, t, ii, jj: (qrow(b, t, ii, jj), 0)),
        scratch_shapes=[pltpu.VMEM((FOX_HEADS, tq, LANES), BF16),
                        pltpu.VMEM((FOX_HEADS, tq, LANES), F32),
                        pltpu.VMEM((FOX_HEADS, tq, 2 * LANES), F32)])
    return pl.pallas_call(
        _fox_kernel,
        grid_spec=grid_spec,
        out_shape=jax.ShapeDtypeStruct((batch * seq, w), BF16),
        compiler_params=_cparams(("parallel", "arbitrary")),
        name="fox_mixer",
    )(ii, jj, proj, kt, proj, qa)


def _merge_kernel(alpha, ya_ref, yb_ref, yc_ref, yd_ref, g0_ref, g1_ref, g2_ref, g3_ref, gb_ref,
                  wb_ref, wo_ref, h_ref, lg_ref, lb_ref, o32_ref, o16_ref):
    merged = None
    for i, (y_ref, gl_ref) in enumerate(((ya_ref, g0_ref), (yb_ref, g1_ref), (yc_ref, g2_ref), (yd_ref, g3_ref))):
        gate = _sigmoid(gl_ref[...].astype(F32) + gb_ref[i:i + 1, :])
        term = gate * _dot(y_ref[...], wb_ref[i])
        merged = term if merged is None else merged + term
    mix = _dot(merged.astype(BF16), wo_ref[...])
    y = _layer_norm(alpha * h_ref[...] + mix, lg_ref[...], lb_ref[...])
    o32_ref[...] = y
    o16_ref[...] = y.astype(BF16)


def _merge(ys, proj, h32, p, l, alpha, tm):
    t, d = h32.shape
    w = BRANCH_W
    gate0 = (COL_GATE * w) // d
    yspec = pl.BlockSpec((tm, w), lambda i: (i, 0))
    gspec = lambda n: pl.BlockSpec((tm, d), lambda i, n=n: (i, gate0 + n))
    row = pl.BlockSpec((tm, d), lambda i: (i, 0))
    return pl.pallas_call(
        functools.partial(_merge_kernel, alpha),
        grid=(t // tm,),
        in_specs=[yspec] * 4 + [gspec(n) for n in range(N_BRANCH)] +
                 [_layer_spec(l, (N_BRANCH, d)), _layer_spec(l, (N_BRANCH, w, d)), _layer_spec(l, (d, d)), row,
                  _layer_spec(l, (1, d)), _layer_spec(l, (1, d))],
        out_specs=[row, row],
        out_shape=[jax.ShapeDtypeStruct((t, d), F32), jax.ShapeDtypeStruct((t, d), BF16)],
        compiler_params=_cparams(("parallel",)),
        name="merge_norm",
    )(*ys, proj, proj, proj, proj, p["gate_b"], p["w_branch"], p["w_out"], h32, p["ln1_g"], p["ln1_b"])


def _ffn_kernel(alpha, h16_ref, h32_ref, wu_ref, wd_ref, lg_ref, lb_ref, o32_ref, o16_ref, acc_ref):
    f = pl.program_id(1)

    @pl.when(f == 0)
    def _():
        acc_ref[...] = jnp.zeros_like(acc_ref)

    up = jnp.maximum(_dot(h16_ref[...], wu_ref[...]), 0.0)
    acc_ref[...] += _dot((up * up).astype(BF16), wd_ref[...])

    @pl.when(f == pl.num_programs(1) - 1)
    def _():
        y = _layer_norm(alpha * h32_ref[...] + acc_ref[...], lg_ref[...], lb_ref[...])
        o32_ref[...] = y
        o16_ref[...] = y.astype(BF16)


def _ffn(h16, h32, p, l, alpha, tm, tf):
    t, d = h32.shape
    dff = p["w_up"].shape[2]
    row = pl.BlockSpec((tm, d), lambda i, f: (i, 0))
    return pl.pallas_call(
        functools.partial(_ffn_kernel, alpha),
        grid=(t // tm, dff // tf),
        in_specs=[row, row, pl.BlockSpec((None, d, tf), lambda i, f: (l, 0, f)),
                  pl.BlockSpec((None, tf, d), lambda i, f: (l, f, 0)),
                  _layer_spec(l, (1, d)), _layer_spec(l, (1, d))],
        out_specs=[row, row],
        out_shape=[jax.ShapeDtypeStruct((t, d), F32), jax.ShapeDtypeStruct((t, d), BF16)],
        scratch_shapes=[pltpu.VMEM((tm, d), F32)],
        compiler_params=_cparams(("parallel", "arbitrary")),
        name="ffn_norm",
    )(h16, h32, p["w_up"], p["w_down"], p["ln2_g"], p["ln2_b"])


VEC_SSD_DTB, VEC_SSD_ALOG, VEC_DN_ALOG, VEC_DN_DTB, VEC_FOX_FB, VEC_ROWS = 0, 1, 2, 3, 4, SUBLANES


def _lane_row(v, start):
    return jnp.pad(v.astype(F32), ((0, 0), (start, LANES - start - v.shape[1])))[:, None, :]


def _expander(start, heads, width):
    m = np.zeros((LANES, heads * width), np.float32)
    for h in range(heads):
        m[start + h, h * width:(h + 1) * width] = 1.0
    return jnp.asarray(m, BF16)


def _tri(n):
    return jnp.asarray(np.tril(np.ones((n, n), np.float32)), BF16)


def _block_diag(n, c, lower):
    blk = np.tril(np.ones((c, c), np.float32)) if lower else np.ones((c, c), np.float32)
    return jnp.asarray(np.kron(np.eye(n // c, dtype=np.float32), blk), BF16)


def _fox_aug_constants():
    na = FOX_HEADS * FOX_AUG
    eq = np.zeros((FOX_SPLIT, LANES, LANES), np.float32)
    ek = np.zeros((FOX_SPLIT, na, LANES), np.float32)
    oq = np.zeros((1, LANES), np.float32)
    ok = np.zeros((na, 1), np.float32)
    for h in range(FOX_HEADS):
        for i in range(FOX_SPLIT):
            ek[i, h * FOX_AUG + i, SM_F + h] = 1.0
            eq[i, SM_F + h, h * FOX_AUG + FOX_SPLIT + i] = 1.0
            oq[0, h * FOX_AUG + i] = 1.0
            ok[h * FOX_AUG + FOX_SPLIT + i, 0] = 1.0
    return jnp.asarray(eq, BF16), jnp.asarray(ek, BF16), jnp.asarray(oq), jnp.asarray(ok)


def _prepare_params(a):
    w = BRANCH_W
    w_in = a["w_in"]
    depth, d = w_in.shape[:2]
    o_dt = w + SSD_XBC
    o_qkv = o_dt + SSD_HEADS
    o_beta = o_qkv + 3 * w
    o_a = o_beta + DN_HEADS
    o_gate = o_a + DN_HEADS
    o_f = o_gate + w + 2 * w + 3 * w
    o_gates = o_f + FOX_HEADS
    w_big = jnp.concatenate([w_in[..., :o_dt], w_in[..., o_qkv:o_beta], w_in[..., o_gate:o_f], w_in[..., o_gates:]],
                            axis=-1).astype(BF16)
    n_small = SSD_HEADS + 2 * DN_HEADS + FOX_HEADS
    assert (SM_DT, SM_BETA, SM_A, SM_F) == (0, SSD_HEADS, SSD_HEADS + DN_HEADS, SSD_HEADS + 2 * DN_HEADS)
    w_small = jnp.concatenate([w_in[..., o_dt:o_qkv], w_in[..., o_beta:o_a], w_in[..., o_a:o_gate],
                               w_in[..., o_f:o_gates], jnp.zeros((depth, d, LANES - n_small), w_in.dtype)],
                              axis=-1).astype(BF16)
    vec = jnp.concatenate([_lane_row(a["ssd_dt_bias"], SM_DT), _lane_row(a["ssd_a_log"], SM_DT),
                           _lane_row(a["dn_a_log"], SM_A), _lane_row(a["dn_dt_bias"], SM_A),
                           _lane_row(a["fox_f_bias"], SM_F),
                           jnp.zeros((depth, VEC_ROWS - 5, LANES), F32)], axis=1)
    fox_eq, fox_ek, fox_oq, fox_ok = _fox_aug_constants()
    row = lambda v: v[:, None, :]
    return {
        "w_big": w_big, "w_small": w_small, "vec": vec,
        "ssd_conv_w": a["ssd_conv_w"], "ssd_conv_b": row(a["ssd_conv_b"]), "dn_conv_w": a["dn_conv_w"],
        "ssd_dskip": row(jnp.repeat(a["ssd_d"], SSD_HEAD_DIM, axis=1)), "ssd_nw": row(a["ssd_norm_w"]),
        "e8": _expander(SM_DT, SSD_HEADS, SSD_HEAD_DIM), "tri_ssd": _tri(SSD_CHUNK),
        "dn_nw": row(a["dn_norm_w"]),
        "sg_g": row(a["sg_ln_g"]), "sg_b": row(a["sg_ln_b"]), "sg_w": a["sg_w"],
        "sg_bs": jnp.repeat(jnp.swapaxes(a["sg_b"], 1, 2), BRANCH_W // SG_GROUPS, axis=2),
        "tri_fox": _tri(FOX_CUM_CHUNK), "fox_eq": fox_eq, "fox_ek": fox_ek, "fox_oq": fox_oq, "fox_ok": fox_ok,
        "gate_b": a["gate_b"], "w_branch": a["w_branch"].astype(BF16), "w_out": a["w_out"].astype(BF16),
        "ln1_g": row(a["ln1_g"]), "ln1_b": row(a["ln1_b"]),
        "w_up": a["w_up"].astype(BF16), "w_down": a["w_down"].astype(BF16),
        "ln2_g": row(a["ln2_g"]), "ln2_b": row(a["ln2_b"]),
    }


def _tiles(batch, seq):
    t = batch * seq
    pick = lambda n, cap: math.gcd(n, cap)
    return {
        "ln": pick(t, 1024), "proj_m": pick(t, 2048), "proj_n": 2048, "ssd": pick(seq, 512), "dn": pick(seq, 256),
        "sg": pick(t, 2048), "fcum": pick(seq, 1024), "fox": pick(seq, 512), "merge": pick(t, 1024),
        "ffn_m": pick(t, 1024), "ffn_f": 2048,
    }


def kernel(x, ln_in_g, ln_in_b, w_in, ssd_conv_w, ssd_conv_b, ssd_dt_bias, ssd_a_log, ssd_d, ssd_norm_w, dn_conv_w,
           dn_a_log, dn_dt_bias, dn_norm_w, sg_ln_g, sg_ln_b, sg_w, sg_b, fox_f_bias, gate_b, w_branch, w_out,
           ln1_g, ln1_b, w_up, w_down, ln2_g, ln2_b):
    batch, seq, d = x.shape
    depth = w_in.shape[0]
    alpha = (2 * depth) ** 0.25
    a = dict(w_in=w_in, ssd_conv_w=ssd_conv_w, ssd_conv_b=ssd_conv_b, ssd_dt_bias=ssd_dt_bias, ssd_a_log=ssd_a_log,
             ssd_d=ssd_d, ssd_norm_w=ssd_norm_w, dn_conv_w=dn_conv_w, dn_a_log=dn_a_log, dn_dt_bias=dn_dt_bias,
             dn_norm_w=dn_norm_w, sg_ln_g=sg_ln_g, sg_ln_b=sg_ln_b, sg_w=sg_w, sg_b=sg_b, fox_f_bias=fox_f_bias,
             gate_b=gate_b, w_branch=w_branch, w_out=w_out, ln1_g=ln1_g, ln1_b=ln1_b, w_up=w_up, w_down=w_down,
             ln2_g=ln2_g, ln2_b=ln2_b)
    tl = _tiles(batch, seq)
    t = batch * seq
    h32, h16 = _entry_norm(x.reshape(t, d), ln_in_g, ln_in_b, tl["ln"])
    p = _prepare_params(a)
    for l in range(depth):
        proj, small = _in_proj(h16, p, l, tl["proj_m"], tl["proj_n"])
        y_a = _ssd_mixer(proj, small, p, l, batch, seq, tl["ssd"])
        y_b = _dn_mixer(proj, small, p, l, batch, seq, tl["dn"])
        y_c = _sg_mixer(proj, p, l, t, tl["sg"])
        qa, kt = _forget_cumsum(proj, small, p, l, batch, seq, tl["fcum"])
        y_d = _fox_mixer(proj, qa, kt, batch, seq, tl["fox"])
        h32, h16 = _merge((y_a, y_b, y_c, y_d), proj, h32, p, l, alpha, tl["merge"])
        h32, h16 = _ffn(h16, h32, p, l, alpha, tl["ffn_m"], tl["ffn_f"])
    return h32.reshape(batch, seq, d)
```

```python
import functools
import math

import numpy as np
import jax
import jax.numpy as jnp
from jax import lax
from jax.experimental import pallas as pl
from jax.experimental.pallas import tpu as pltpu

F32 = jnp.float32
BF16 = jnp.bfloat16

BRANCH_W = 512
N_BRANCH = 4
SSD_HEADS, SSD_HEAD_DIM, SSD_GROUPS, SSD_STATE, SSD_CONV = 8, 64, 2, 128, 4
SSD_XBC = BRANCH_W + 2 * SSD_GROUPS * SSD_STATE
DN_HEADS, DN_HEAD_DIM, DN_CONV = 4, 128, 4
SG_GROUPS, SG_CHUNK = 4, 128
FOX_HEADS, FOX_HEAD_DIM = 8, 64
LN_EPS = 1e-5
NORM_EPS = 1e-6
NEG_BIG = -1e30

LANES = 128
SUBLANES = 8
VMEM_LIMIT_BYTES = 56 * 1024 * 1024

SM_DT, SM_BETA, SM_A, SM_F = 0, 8, 12, 16

SSD_CHUNK = 128
DN_CHUNK = 64


def _dot(a, b):
    return jnp.dot(a, b, preferred_element_type=F32)


def _dot_nt(a, b):
    return lax.dot_general(a, b, (((1,), (1,)), ((), ())), preferred_element_type=F32)


def _dot_tn(a, b):
    return lax.dot_general(a, b, (((0,), (0,)), ((), ())), preferred_element_type=F32)


def _bdot(a, b):
    return _dot(a.astype(BF16), b.astype(BF16))


def _split_terms(x, n):
    terms, r = [], x
    for i in range(n):
        p = r.astype(BF16)
        terms.append(p)
        if i + 1 < n:
            r = r - p.astype(F32)
    return terms


def _sel_right(x, m, n):
    return sum(_dot(p, m) for p in _split_terms(x, n))


def _sel_left(m, x, n):
    return sum(_dot(m, p) for p in _split_terms(x, n))


def _sigmoid(x):
    return 0.5 * jnp.tanh(0.5 * x) + 0.5


def _silu(x):
    hx = 0.5 * x
    return hx + hx * jnp.tanh(hx)


def _softplus(x):
    return jnp.maximum(x, 0.0) + jnp.log1p(jnp.exp(-jnp.abs(x)))


def _layer_norm(x, g, b):
    mu = jnp.mean(x, axis=-1, keepdims=True)
    xc = x - mu
    var = jnp.mean(xc * xc, axis=-1, keepdims=True)
    return xc * lax.rsqrt(var + LN_EPS) * g + b


def _causal_conv(x, tail, w):
    n = x.shape[0]
    k = w.shape[0]
    row8 = lax.broadcasted_iota(jnp.int32, (SUBLANES, x.shape[1]), 0)
    acc = x * w[k - 1:k, :]
    for s in range(1, k):
        xr = pltpu.roll(x, s, axis=0)
        pr = pltpu.roll(tail, s, axis=0)
        head = jnp.where(row8 < s, pr, xr[:SUBLANES])
        xs = jnp.concatenate([head, xr[SUBLANES:]], axis=0)
        acc = acc + xs * w[k - 1 - s:k - s, :]
    return acc, x[n - SUBLANES:]


def _cparams(sem):
    return pltpu.CompilerParams(dimension_semantics=sem, vmem_limit_bytes=VMEM_LIMIT_BYTES)


def _const_spec(shape):
    nd = len(shape)
    return pl.BlockSpec(shape, lambda *_: (0,) * nd)


def _ln_kernel(x_ref, g_ref, b_ref, o32_ref, o16_ref):
    y = _layer_norm(x_ref[...], g_ref[...], b_ref[...])
    o32_ref[...] = y
    o16_ref[...] = y.astype(BF16)


def _entry_norm(x2, g, b, tm):
    t, d = x2.shape
    row = pl.BlockSpec((tm, d), lambda i: (i, 0))
    return pl.pallas_call(
        _ln_kernel,
        grid=(t // tm,),
        in_specs=[row, _const_spec((1, d)), _const_spec((1, d))],
        out_specs=[row, row],
        out_shape=[jax.ShapeDtypeStruct((t, d), F32), jax.ShapeDtypeStruct((t, d), BF16)],
        compiler_params=_cparams(("parallel",)),
        name="entry_norm",
    )(x2, g.reshape(1, d), b.reshape(1, d))


COL_Z, COL_X, COL_BC, COL_DQ, COL_DK, COL_DV, COL_DG, COL_SU, COL_SV, COL_FQ, COL_FK, COL_FV, COL_GATE = range(13)


def _proj_kernel(h_ref, w_ref, ws_ref, o_ref, os_ref):
    h = h_ref[...]
    o_ref[...] = _dot(h, w_ref[...]).astype(BF16)

    @pl.when(pl.program_id(1) == 0)
    def _():
        os_ref[...] = _dot(h, ws_ref[...])


def _layer_spec(l, shape):
    nd = len(shape)
    return pl.BlockSpec((None,) + tuple(shape), lambda *_: (l,) + (0,) * nd)


def _in_proj(h16, p, l, tm, tn):
    t, d = h16.shape
    w_big, w_small = p["w_big"], p["w_small"]
    n = w_big.shape[2]
    return pl.pallas_call(
        _proj_kernel,
        grid=(t // tm, n // tn),
        in_specs=[pl.BlockSpec((tm, d), lambda i, j: (i, 0)),
                  pl.BlockSpec((None, d, tn), lambda i, j: (l, 0, j)),
                  _layer_spec(l, (d, LANES))],
        out_specs=[pl.BlockSpec((tm, tn), lambda i, j: (i, j)),
                   pl.BlockSpec((tm, LANES), lambda i, j: (i, 0))],
        out_shape=[jax.ShapeDtypeStruct((t, n), BF16), jax.ShapeDtypeStruct((t, LANES), F32)],
        compiler_params=_cparams(("parallel", "arbitrary")),
        name="in_proj",
    )(h16, w_big, w_small)


def _ssd_kernel(z_ref, x_ref, bc_ref, sm_ref, cwx_ref, cwbc_ref, cbx_ref, cbbc_ref, vec_ref,
                dskip_ref, nw_ref, e8_ref, tri_ref, o_ref, state_ref, xtail_ref, bctail_ref):
    dtb_ref = vec_ref.at[VEC_SSD_DTB:VEC_SSD_DTB + 1]
    alog_ref = vec_ref.at[VEC_SSD_ALOG:VEC_SSD_ALOG + 1]
    first = pl.program_id(1) == 0
    ts = x_ref.shape[0]
    q = SSD_CHUNK
    hp = SSD_HEADS // SSD_GROUPS * SSD_HEAD_DIM
    e8 = e8_ref[...]
    tri = tri_ref[...]

    @pl.when(first)
    def _():
        state_ref[...] = jnp.zeros_like(state_ref)
        xtail_ref[...] = jnp.zeros_like(xtail_ref)
        bctail_ref[...] = jnp.zeros_like(bctail_ref)

    xs, xtail_ref[...] = _causal_conv(x_ref[...].astype(F32), xtail_ref[...], cwx_ref[...])
    xs = _silu(xs + cbx_ref[...])
    bcm, bctail_ref[...] = _causal_conv(bc_ref[...].astype(F32), bctail_ref[...], cwbc_ref[...])
    bcm = _silu(bcm + cbbc_ref[...]).astype(BF16)
    lane = lax.broadcasted_iota(jnp.int32, (1, LANES), 1)
    head_lane = (lane >= SM_DT) & (lane < SM_DT + SSD_HEADS)
    a_neg = jnp.where(head_lane, -jnp.exp(alog_ref[...]), 0.0)
    dt = _softplus(sm_ref[...] + dtb_ref[...])
    dta = dt * a_neg
    xdt = xs * _sel_right(dt, e8, 2)
    ri = lax.broadcasted_iota(jnp.int32, (q, q), 0)
    ci = lax.broadcasted_iota(jnp.int32, (q, q), 1)
    lower = ci <= ri
    lane_g = lax.broadcasted_iota(jnp.int32, (1, hp), 1) // SSD_HEAD_DIM

    for c in range(ts // q):
        sl = slice(c * q, (c + 1) * q)
        acs = _sel_left(tri, dta[sl], 3)
        acs_t = acs.T
        eacs = _sel_right(jnp.exp(acs), e8, 2)
        dec_end = _sel_right(jnp.exp(acs[q - 1:q, :] - acs), e8, 2)
        ys = []
        for g in range(SSD_GROUPS):
            gl = slice(g * hp, (g + 1) * hp)
            bm = bcm[sl, g * SSD_STATE:(g + 1) * SSD_STATE]
            cm = bcm[sl, (SSD_GROUPS + g) * SSD_STATE:(SSD_GROUPS + g + 1) * SSD_STATE]
            cb = _dot_nt(cm, bm)
            xg = xdt[sl, gl]
            s_prev = state_ref[g]
            y = _dot(cm, s_prev.astype(BF16)) * eacs[:, gl]
            for e in range(SSD_HEADS // SSD_GROUPS):
                h = SM_DT + g * (SSD_HEADS // SSD_GROUPS) + e
                seg = jnp.exp(jnp.where(lower, acs[:, h:h + 1] - acs_t[h:h + 1, :], NEG_BIG))
                xe = jnp.where(lane_g == e, xg, 0.0).astype(BF16)
                y = y + _dot((cb * seg).astype(BF16), xe)
            state_ref[g] = s_prev * eacs[q - 1:q, gl] + _dot_tn(bm, (xg * dec_end[:, gl]).astype(BF16))
            ys.append(y)
        y = jnp.concatenate(ys, axis=1) + dskip_ref[...] * xs[sl]
        y = y * _silu(z_ref[sl, :].astype(F32))
        y = y * lax.rsqrt(jnp.mean(y * y, axis=-1, keepdims=True) + NORM_EPS) * nw_ref[...]
        o_ref[sl, :] = y.astype(BF16)


def _ssd_mixer(proj, small, p, l, batch, seq, ts):
    nb = seq // ts
    w = BRANCH_W
    rows = lambda col: pl.BlockSpec((ts, w), lambda b, s, col=col: (b * nb + s, col))
    half = lambda r, n: pl.BlockSpec((None, r, w), lambda b, s, n=n: (l, 0, n))
    hp = SSD_HEADS // SSD_GROUPS * SSD_HEAD_DIM
    return pl.pallas_call(
        _ssd_kernel,
        grid=(batch, nb),
        in_specs=[rows(COL_Z), rows(COL_X), rows(COL_BC),
                  pl.BlockSpec((ts, LANES), lambda b, s: (b * nb + s, 0)),
                  half(SSD_CONV, 0), half(SSD_CONV, 1), half(1, 0), half(1, 1),
                  _layer_spec(l, (VEC_ROWS, LANES)),
                  _layer_spec(l, (1, w)), _layer_spec(l, (1, w)),
                  _const_spec((LANES, w)), _const_spec((SSD_CHUNK, SSD_CHUNK))],
        out_specs=pl.BlockSpec((ts, w), lambda b, s: (b * nb + s, 0)),
        out_shape=jax.ShapeDtypeStruct((batch * seq, w), BF16),
        scratch_shapes=[pltpu.VMEM((SSD_GROUPS, SSD_STATE, hp), F32),
                        pltpu.VMEM((SUBLANES, w), F32), pltpu.VMEM((SUBLANES, w), F32)],
        compiler_params=_cparams(("parallel", "arbitrary")),
        name="ssd_mixer",
    )(proj, proj, proj, small, p["ssd_conv_w"], p["ssd_conv_w"], p["ssd_conv_b"], p["ssd_conv_b"], p["vec"],
      p["ssd_dskip"], p["ssd_nw"], p["e8"], p["tri_ssd"])


DN_SLOTS = 2
DN_STAGGER = 9


def _interleave_staggered(gens, stagger):
    active, waiting, rounds = [], list(gens), 0
    while active or waiting:
        if waiting and rounds % stagger == 0:
            active.append(waiting.pop(0))
        for g in list(active):
            try:
                next(g)
            except StopIteration:
                active.remove(g)
        rounds += 1


def _dn_kernel(q_ref, k_ref, v_ref, gate_ref, sm_ref, cwq_ref, cwk_ref, cwv_ref, vec_ref, nw_ref,
               tri_ref, hsum_ref, o_ref, state_ref, qtail_ref, ktail_ref, vtail_ref):
    alog_ref = vec_ref.at[VEC_DN_ALOG:VEC_DN_ALOG + 1]
    dtb_ref = vec_ref.at[VEC_DN_DTB:VEC_DN_DTB + 1]
    ts = q_ref.shape[1]
    c = DN_CHUNK
    dk = DN_HEAD_DIM
    nh = DN_HEADS

    @pl.when(pl.program_id(1) == 0)
    def _():
        state_ref[...] = jnp.zeros_like(state_ref)
        qtail_ref[...] = jnp.zeros_like(qtail_ref)
        ktail_ref[...] = jnp.zeros_like(ktail_ref)
        vtail_ref[...] = jnp.zeros_like(vtail_ref)

    def row_work(r):
        qf, qtail_ref[r] = _causal_conv(q_ref[r].astype(F32), qtail_ref[r], cwq_ref[...])
        qf = _silu(qf)
        yield
        kf, ktail_ref[r] = _causal_conv(k_ref[r].astype(F32), ktail_ref[r], cwk_ref[...])
        kf = _silu(kf)
        yield
        vf, vtail_ref[r] = _causal_conv(v_ref[r].astype(F32), vtail_ref[r], cwv_ref[...])
        vf = _silu(vf)
        yield
        qf = qf * (lax.rsqrt(_sel_right(qf * qf, hsum_ref[...], 1) + NORM_EPS) * (dk ** -0.5))
        kf = kf * lax.rsqrt(_sel_right(kf * kf, hsum_ref[...], 1) + NORM_EPS)
        yield
        sm = sm_ref[r]
        lane = lax.broadcasted_iota(jnp.int32, (1, LANES), 1)
        a_lane = (lane >= SM_A) & (lane < SM_A + nh)
        neg_a = jnp.where(a_lane, -jnp.exp(alog_ref[...]), 0.0)
        gdec = neg_a * _softplus(sm + dtb_ref[...])
        beta_s = _sigmoid(sm)
        gcs = _sel_left(tri_ref[...], gdec, 3)
        gtot = jnp.concatenate([jnp.broadcast_to(gcs[(n + 1) * c - 1:(n + 1) * c], (c, LANES))
                                for n in range(ts // c)], axis=0)
        gcs_t = gcs.T
        egcs_s = jnp.exp(gcs)
        edec_s = jnp.exp(gtot - gcs)
        elast_s = jnp.exp(gtot)
        ri = lax.broadcasted_iota(jnp.int32, (ts, ts), 0)
        ci = lax.broadcasted_iota(jnp.int32, (ts, ts), 1)
        lower = (ci >= ri // c * c) & (ci <= ri)
        offdiag = ci != ri
        yield

        def head_lanes(x, lane_idx):
            return jnp.broadcast_to(x[:, lane_idx:lane_idx + 1], (x.shape[0], dk))

        a_mats, rhs, qk, qg, kd, elast = [], [], [], [], [], []
        for h in range(nh):
            hl = slice(h * dk, (h + 1) * dk)
            gl = SM_A + h
            q, k = qf[:, hl], kf[:, hl]
            bh = head_lanes(beta_s, SM_BETA + h)
            egcs = head_lanes(egcs_s, gl)
            gamma = jnp.exp(jnp.where(lower, gcs[:, gl:gl + 1] - gcs_t[gl:gl + 1, :], NEG_BIG))
            kb = k * bh
            k16 = k.astype(BF16)
            a_mats.append(jnp.where(offdiag, _dot_nt(kb.astype(BF16), k16) * gamma, 0.0))
            yield
            rhs.append(jnp.concatenate([kb * egcs, vf[:, hl] * bh], axis=1).astype(BF16))
            qk.append((_dot_nt(q.astype(BF16), k16) * gamma).astype(BF16))
            yield
            qg.append((q * egcs).astype(BF16))
            kd.append((k * head_lanes(edec_s, gl)).astype(BF16))
            elast.append([head_lanes(elast_s[n * c:n * c + 1], gl) for n in range(ts // c)])
            yield

        eye = jnp.where(offdiag, 0.0, 1.0).astype(F32)
        pws = [(-a).astype(BF16) for a in a_mats]
        invs = [eye - a for a in a_mats]
        for _ in range(int(math.log2(c)) - 1):
            sq = [_dot(pw, pw) for pw in pws]
            yield
            pws = [x.astype(BF16) for x in sq]
            invs = [inv + _dot(inv.astype(BF16), pw) for inv, pw in zip(invs, pws)]
            yield
        wu = [_dot(invs[h].astype(BF16), rhs[h]) for h in range(nh)]
        yield

        pn = [[_dot_tn(kd[h][ch * c:(ch + 1) * c], wu[h][ch * c:(ch + 1) * c].astype(BF16))
               for ch in range(ts // c)] for h in range(nh)]
        yield
        states = [state_ref[r, h] for h in range(nh)]
        v_new = [[] for _ in range(nh)]
        o_state = [[] for _ in range(nh)]
        for ch in range(ts // c):
            sl = slice(ch * c, (ch + 1) * c)
            for h in range(nh):
                s16 = states[h].astype(BF16)
                states[h] = (states[h] * elast[h][ch] - _dot(pn[h][ch][:, :dk].astype(BF16), s16)
                             + pn[h][ch][:, dk:])
                lhs = jnp.concatenate([wu[h][sl, :dk].astype(BF16), qg[h][sl]], axis=0)
                ws = _dot(lhs, s16)
                v_new[h].append(wu[h][sl, dk:] - ws[:c])
                o_state[h].append(ws[c:])
            yield
        for h in range(nh):
            hl = slice(h * dk, (h + 1) * dk)
            state_ref[r, h] = states[h]
            vn_all = jnp.concatenate(v_new[h], axis=0).astype(BF16)
            o = jnp.concatenate(o_state[h], axis=0) + _dot(qk[h], vn_all)
            o = o * lax.rsqrt(jnp.mean(o * o, axis=-1, keepdims=True) + NORM_EPS) * nw_ref[...]
            o_ref[r, :, hl] = (o * _silu(gate_ref[r, :, hl].astype(F32))).astype(BF16)
            yield

    _interleave_staggered([row_work(r) for r in range(q_ref.shape[0])], DN_STAGGER)


def _dn_mixer(proj, small, p, l, batch, seq, ts):
    nb = seq // ts
    w = BRANCH_W
    nh, dk = DN_HEADS, DN_HEAD_DIM
    nr = math.gcd(batch, DN_SLOTS)
    proj3 = proj.reshape(batch, seq, proj.shape[1])
    small3 = small.reshape(batch, seq, LANES)
    rows = lambda col: pl.BlockSpec((nr, ts, w), lambda b, s, col=col: (b, s, col))
    conv = lambda n: pl.BlockSpec((None, DN_CONV, w), lambda b, s, n=n: (l, 0, n))
    out = pl.pallas_call(
        _dn_kernel,
        grid=(batch // nr, nb),
        in_specs=[rows(COL_DQ), rows(COL_DK), rows(COL_DV), rows(COL_DG),
                  pl.BlockSpec((nr, ts, LANES), lambda b, s: (b, s, 0)),
                  conv(0), conv(1), conv(2),
                  _layer_spec(l, (VEC_ROWS, LANES)), _layer_spec(l, (1, dk)),
                  _const_spec((ts, ts)), _const_spec((w, w))],
        out_specs=pl.BlockSpec((nr, ts, w), lambda b, s: (b, s, 0)),
        out_shape=jax.ShapeDtypeStruct((batch, seq, w), BF16),
        scratch_shapes=[pltpu.VMEM((nr, nh, dk, dk), F32),
                        pltpu.VMEM((nr, SUBLANES, w), F32), pltpu.VMEM((nr, SUBLANES, w), F32),
                        pltpu.VMEM((nr, SUBLANES, w), F32)],
        compiler_params=_cparams(("parallel", "arbitrary")),
        name="dn_mixer",
    )(proj3, proj3, proj3, proj3, small3, p["dn_conv_w"], p["dn_conv_w"], p["dn_conv_w"], p["vec"],
      p["dn_nw"], _block_diag(ts, DN_CHUNK, True), _block_diag(w, DN_HEAD_DIM, False))
    return out.reshape(batch * seq, w)


def _sg_kernel(u_ref, v_ref, g_ref, b_ref, w_ref, bs_ref, o_ref):
    ts = u_ref.shape[0]
    q = SG_CHUNK
    gd = BRANCH_W // SG_GROUPS
    u = jax.nn.gelu(u_ref[...].astype(F32))
    v = _layer_norm(jax.nn.gelu(v_ref[...].astype(F32)), g_ref[...], b_ref[...])
    ri = lax.broadcasted_iota(jnp.int32, (q, q), 0)
    ci = lax.broadcasted_iota(jnp.int32, (q, q), 1)
    lower = ci <= ri
    for g in range(SG_GROUPS):
        gl = slice(g * gd, (g + 1) * gd)
        wg = jnp.where(lower, w_ref[g], 0.0).astype(BF16)
        for c in range(ts // q):
            sl = slice(c * q, (c + 1) * q)
            mixed = _dot(wg, v[sl, gl].astype(BF16)) + bs_ref[:, gl]
            o_ref[sl, gl] = (u[sl, gl] * mixed).astype(BF16)


def _sg_mixer(proj, p, l, t, ts):
    w = BRANCH_W
    rows = lambda col: pl.BlockSpec((ts, w), lambda i, col=col: (i, col))
    return pl.pallas_call(
        _sg_kernel,
        grid=(t // ts,),
        in_specs=[rows(COL_SU), rows(COL_SV), _layer_spec(l, (1, w)), _layer_spec(l, (1, w)),
                  _layer_spec(l, (SG_GROUPS, SG_CHUNK, SG_CHUNK)), _layer_spec(l, (SG_CHUNK, w))],
        out_specs=pl.BlockSpec((ts, w), lambda i: (i, 0)),
        out_shape=jax.ShapeDtypeStruct((t, w), BF16),
        compiler_params=_cparams(("parallel",)),
        name="sg_mixer",
    )(proj, proj, p["sg_g"], p["sg_b"], p["sg_w"], p["sg_bs"])


FOX_SPLIT = 3
FOX_CUM_CHUNK = 128
FOX_AUG = LANES // (2 * FOX_HEADS)
assert FOX_AUG >= 2 * FOX_SPLIT
LOG2E = math.log2(math.e)


def _fcum_kernel(sm_ref, k_ref, vec_ref, tri_ref, eq_ref, ek_ref, oq_ref, ok_ref, qa_ref, kt_ref, carry_ref):
    fb_ref = vec_ref.at[VEC_FOX_FB:VEC_FOX_FB + 1]
    @pl.when(pl.program_id(1) == 0)
    def _():
        carry_ref[...] = jnp.zeros_like(carry_ref)

    n = sm_ref.shape[0]
    per = LANES // FOX_HEAD_DIM
    ls = jax.nn.log_sigmoid(sm_ref[...] + fb_ref[...])
    tri = tri_ref[...]
    carry = carry_ref[...]
    pieces = []
    for r in range(n // FOX_CUM_CHUNK):
        piece = _sel_left(tri, ls[r * FOX_CUM_CHUNK:(r + 1) * FOX_CUM_CHUNK], 3) + carry
        carry = piece[FOX_CUM_CHUNK - 1:FOX_CUM_CHUNK, :]
        pieces.append(piece)
    c = jnp.concatenate(pieces, axis=0)
    carry_ref[...] = carry
    terms = _split_terms(c * LOG2E, FOX_SPLIT)
    cq = sum(_dot(t, eq_ref[i]) for i, t in enumerate(terms))
    qa_ref[...] = (oq_ref[...] + cq).astype(BF16)
    ck_t = sum(_dot_nt(ek_ref[i], t) for i, t in enumerate(terms))
    aug_t = (ok_ref[...] - ck_t).astype(BF16)
    pad = jnp.zeros((LANES - FOX_HEAD_DIM - FOX_AUG, n), BF16)
    for b in range(FOX_HEADS // per):
        k_t = k_ref[:, b * LANES:(b + 1) * LANES].astype(F32).T.astype(BF16)
        for e in range(per):
            h = b * per + e
            kt_ref[h] = jnp.concatenate([k_t[e * FOX_HEAD_DIM:(e + 1) * FOX_HEAD_DIM],
                                         aug_t[h * FOX_AUG:(h + 1) * FOX_AUG], pad], axis=0)


def _forget_cumsum(proj, small, p, l, batch, seq, tc):
    nb = seq // tc
    na = FOX_HEADS * FOX_AUG
    return pl.pallas_call(
        _fcum_kernel,
        grid=(batch, nb),
        in_specs=[pl.BlockSpec((tc, LANES), lambda b, s: (b * nb + s, 0)),
                  pl.BlockSpec((tc, BRANCH_W), lambda b, s: (b * nb + s, COL_FK)),
                  _layer_spec(l, (VEC_ROWS, LANES)), _const_spec((FOX_CUM_CHUNK, FOX_CUM_CHUNK)),
                  _const_spec((FOX_SPLIT, LANES, LANES)), _const_spec((FOX_SPLIT, na, LANES)),
                  _const_spec((1, LANES)), _const_spec((na, 1))],
        out_specs=[pl.BlockSpec((tc, LANES), lambda b, s: (b * nb + s, 0)),
                   pl.BlockSpec((None, FOX_HEADS, LANES, tc), lambda b, s: (b, 0, 0, s))],
        out_shape=[jax.ShapeDtypeStruct((batch * seq, LANES), BF16),
                   jax.ShapeDtypeStruct((batch, FOX_HEADS, LANES, seq), BF16)],
        scratch_shapes=[pltpu.VMEM((1, LANES), F32)],
        compiler_params=_cparams(("parallel", "arbitrary")),
        name="forget_cumsum",
    )(small, proj, p["vec"], p["tri_fox"], p["fox_eq"], p["fox_ek"], p["fox_oq"], p["fox_ok"])


def _roll_lanes(x, shift):
    shift %= x.shape[-1]
    return x if shift == 0 else pltpu.roll(x, shift, axis=x.ndim - 1)


def _fox_kernel(ii_ref, jj_ref, q_ref, kt_ref, v_ref, qa_ref, o_ref, qaug_ref, m_ref, acc_ref):
    t = pl.program_id(1)
    i = ii_ref[t]
    j = jj_ref[t]
    tq = q_ref.shape[0]
    per = LANES // FOX_HEAD_DIM
    lane = lax.broadcasted_iota(jnp.int32, (1, LANES), 1)

    @pl.when(j == 0)
    def _():
        m_ref[...] = jnp.full_like(m_ref, NEG_BIG)
        acc_ref[...] = jnp.zeros_like(acc_ref)
        qa = qa_ref[...].astype(F32)
        for h in range(FOX_HEADS):
            bl = slice(h // per * LANES, (h // per + 1) * LANES)
            q2 = (q_ref[:, bl].astype(F32) * (LOG2E * FOX_HEAD_DIM ** -0.5)).astype(BF16).astype(F32)
            q_own = _roll_lanes(q2, LANES - h % per * FOX_HEAD_DIM)
            extra = _roll_lanes(qa, FOX_HEAD_DIM - h * FOX_AUG)
            in_extra = (lane >= FOX_HEAD_DIM) & (lane < FOX_HEAD_DIM + FOX_AUG)
            qaug_ref[h] = jnp.where(lane < FOX_HEAD_DIM, q_own, jnp.where(in_extra, extra, 0.0)).astype(BF16)

    tk = kt_ref.shape[2]
    nsub = tq // tk

    def logits(h, rows):
        return _dot(qaug_ref[h, rows, :], kt_ref[h])

    def softmax(h, rows, s, masked):
        if masked:
            s = jnp.where(lax.broadcasted_iota(jnp.int32, (tk, tk), 1) <= lax.broadcasted_iota(jnp.int32, (tk, tk), 0),
                          s, NEG_BIG)
        m_old = m_ref[h, rows, :]
        m_new = jnp.maximum(m_old, jnp.max(s, axis=-1, keepdims=True))
        m_ref[h, rows, :] = m_new
        return jnp.exp2(s - m_new[:, :1]).astype(BF16), jnp.exp2(m_old - m_new)

    def accumulate(h, rows, p, alpha):
        bl = slice(h // per * LANES, (h // per + 1) * LANES)
        v2 = v_ref[:, bl]
        vaug = jnp.concatenate([v2, jnp.ones_like(v2)], axis=1)
        acc_ref[h, rows, :] = jnp.concatenate([alpha, alpha], axis=1) * acc_ref[h, rows, :] + _dot(p, vaug)

    def step(first_sub):
        units = [(h, r) for h in range(FOX_HEADS) for r in range(max(first_sub, 0), nsub)]
        rows = lambda r: slice(r * tk, (r + 1) * tk)
        s_next = logits(units[0][0], rows(units[0][1]))
        for n, (h, r) in enumerate(units):
            s = s_next
            if n + 1 < len(units):
                s_next = logits(units[n + 1][0], rows(units[n + 1][1]))
            accumulate(h, rows(r), *softmax(h, rows(r), s, r == first_sub))

    d = j - i * nsub

    @pl.when(d < 0)
    def _():
        step(-1)

    for k in range(nsub):
        @pl.when(d == k)
        def _(k=k):
            step(k)

    @pl.when(d == nsub - 1)
    def _():
        for b in range(FOX_HEADS // per):
            outs = []
            for e in range(per):
                a = acc_ref[b * per + e]
                outs.append(a[:, :LANES] / a[:, LANES:])
            o_ref[:, b * LANES:(b + 1) * LANES] = jnp.where(lane // FOX_HEAD_DIM == 0, outs[0], outs[1]).astype(BF16)


def _fox_mixer(proj, qa, kt, batch, seq, tq, tk):
    nq, nk = seq // tq, seq // tk
    w = BRANCH_W
    pairs = [(i, j) for i in range(nq) for j in range((i + 1) * (tq // tk))]
    ii = jnp.asarray([p[0] for p in pairs], jnp.int32)
    jj = jnp.asarray([p[1] for p in pairs], jnp.int32)
    qrow = lambda b, t, ii, jj: b * nq + ii[t]
    krow = lambda b, t, ii, jj: b * nk + jj[t]
    grid_spec = pltpu.PrefetchScalarGridSpec(
        num_scalar_prefetch=2,
        grid=(batch, len(pairs)),
        in_specs=[pl.BlockSpec((tq, w), lambda b, t, ii, jj: (qrow(b, t, ii, jj), COL_FQ)),
                  pl.BlockSpec((None, FOX_HEADS, LANES, tk), lambda b, t, ii, jj: (b, 0, 0, jj[t])),
                  pl.BlockSpec((tk, w), lambda b, t, ii, jj: (krow(b, t, ii, jj), COL_FV)),
                  pl.BlockSpec((tq, LANES), lambda b, t, ii, jj: (qrow(b, t, ii, jj), 0))],
        out_specs=pl.BlockSpec((tq, w), lambda b, t, ii, jj: (qrow(b, t, ii, jj), 0)),
        scratch_shapes=[pltpu.VMEM((FOX_HEADS, tq, LANES), BF16),
                        pltpu.VMEM((FOX_HEADS, tq, LANES), F32),
                        pltpu.VMEM((FOX_HEADS, tq, 2 * LANES), F32)])
    return pl.pallas_call(
        _fox_kernel,
        grid_spec=grid_spec,
        out_shape=jax.ShapeDtypeStruct((batch * seq, w), BF16),
        compiler_params=_cparams(("parallel", "arbitrary")),
        name="fox_mixer",
    )(ii, jj, proj, kt, proj, qa)


def _merge_kernel(alpha, ya_ref, yb_ref, yc_ref, yd_ref, g0_ref, g1_ref, g2_ref, g3_ref, gb_ref,
                  wb_ref, wo_ref, h_ref, lg_ref, lb_ref, o32_ref, o16_ref):
    merged = None
    for i, (y_ref, gl_ref) in enumerate(((ya_ref, g0_ref), (yb_ref, g1_ref), (yc_ref, g2_ref), (yd_ref, g3_ref))):
        gate = _sigmoid(gl_ref[...].astype(F32) + gb_ref[i:i + 1, :])
        term = gate * _dot(y_ref[...], wb_ref[i])
        merged = term if merged is None else merged + term
    mix = _dot(merged.astype(BF16), wo_ref[...])
    y = _layer_norm(alpha * h_ref[...] + mix, lg_ref[...], lb_ref[...])
    o32_ref[...] = y
    o16_ref[...] = y.astype(BF16)


def _merge(ys, proj, h32, p, l, alpha, tm):
    t, d = h32.shape
    w = BRANCH_W
    gate0 = (COL_GATE * w) // d
    yspec = pl.BlockSpec((tm, w), lambda i: (i, 0))
    gspec = lambda n: pl.BlockSpec((tm, d), lambda i, n=n: (i, gate0 + n))
    row = pl.BlockSpec((tm, d), lambda i: (i, 0))
    return pl.pallas_call(
        functools.partial(_merge_kernel, alpha),
        grid=(t // tm,),
        in_specs=[yspec] * 4 + [gspec(n) for n in range(N_BRANCH)] +
                 [_layer_spec(l, (N_BRANCH, d)), _layer_spec(l, (N_BRANCH, w, d)), _layer_spec(l, (d, d)), row,
                  _layer_spec(l, (1, d)), _layer_spec(l, (1, d))],
        out_specs=[row, row],
        out_shape=[jax.ShapeDtypeStruct((t, d), F32), jax.ShapeDtypeStruct((t, d), BF16)],
        compiler_params=_cparams(("parallel",)),
        name="merge_norm",
    )(*ys, proj, proj, proj, proj, p["gate_b"], p["w_branch"], p["w_out"], h32, p["ln1_g"], p["ln1_b"])


def _ffn_kernel(alpha, h16_ref, h32_ref, wu_ref, wd_ref, lg_ref, lb_ref, o32_ref, o16_ref, acc_ref):
    f = pl.program_id(1)

    @pl.when(f == 0)
    def _():
        acc_ref[...] = jnp.zeros_like(acc_ref)

    up = jnp.maximum(_dot(h16_ref[...], wu_ref[...]), 0.0)
    acc_ref[...] += _dot((up * up).astype(BF16), wd_ref[...])

    @pl.when(f == pl.num_programs(1) - 1)
    def _():
        y = _layer_norm(alpha * h32_ref[...] + acc_ref[...], lg_ref[...], lb_ref[...])
        o32_ref[...] = y
        o16_ref[...] = y.astype(BF16)


def _ffn(h16, h32, p, l, alpha, tm, tf):
    t, d = h32.shape
    dff = p["w_up"].shape[2]
    row = pl.BlockSpec((tm, d), lambda i, f: (i, 0))
    return pl.pallas_call(
        functools.partial(_ffn_kernel, alpha),
        grid=(t // tm, dff // tf),
        in_specs=[row, row, pl.BlockSpec((None, d, tf), lambda i, f: (l, 0, f)),
                  pl.BlockSpec((None, tf, d), lambda i, f: (l, f, 0)),
                  _layer_spec(l, (1, d)), _layer_spec(l, (1, d))],
        out_specs=[row, row],
        out_shape=[jax.ShapeDtypeStruct((t, d), F32), jax.ShapeDtypeStruct((t, d), BF16)],
        scratch_shapes=[pltpu.VMEM((tm, d), F32)],
        compiler_params=_cparams(("parallel", "arbitrary")),
        name="ffn_norm",
    )(h16, h32, p["w_up"], p["w_down"], p["ln2_g"], p["ln2_b"])


VEC_SSD_DTB, VEC_SSD_ALOG, VEC_DN_ALOG, VEC_DN_DTB, VEC_FOX_FB, VEC_ROWS = 0, 1, 2, 3, 4, SUBLANES


def _lane_row(v, start):
    return jnp.pad(v.astype(F32), ((0, 0), (start, LANES - start - v.shape[1])))[:, None, :]


def _expander(start, heads, width):
    m = np.zeros((LANES, heads * width), np.float32)
    for h in range(heads):
        m[start + h, h * width:(h + 1) * width] = 1.0
    return jnp.asarray(m, BF16)


def _tri(n):
    return jnp.asarray(np.tril(np.ones((n, n), np.float32)), BF16)


def _block_diag(n, c, lower):
    blk = np.tril(np.ones((c, c), np.float32)) if lower else np.ones((c, c), np.float32)
    return jnp.asarray(np.kron(np.eye(n // c, dtype=np.float32), blk), BF16)


def _fox_aug_constants():
    na = FOX_HEADS * FOX_AUG
    eq = np.zeros((FOX_SPLIT, LANES, LANES), np.float32)
    ek = np.zeros((FOX_SPLIT, na, LANES), np.float32)
    oq = np.zeros((1, LANES), np.float32)
    ok = np.zeros((na, 1), np.float32)
    for h in range(FOX_HEADS):
        for i in range(FOX_SPLIT):
            ek[i, h * FOX_AUG + i, SM_F + h] = 1.0
            eq[i, SM_F + h, h * FOX_AUG + FOX_SPLIT + i] = 1.0
            oq[0, h * FOX_AUG + i] = 1.0
            ok[h * FOX_AUG + FOX_SPLIT + i, 0] = 1.0
    return jnp.asarray(eq, BF16), jnp.asarray(ek, BF16), jnp.asarray(oq), jnp.asarray(ok)


def _prepare_params(a):
    w = BRANCH_W
    w_in = a["w_in"]
    depth, d = w_in.shape[:2]
    o_dt = w + SSD_XBC
    o_qkv = o_dt + SSD_HEADS
    o_beta = o_qkv + 3 * w
    o_a = o_beta + DN_HEADS
    o_gate = o_a + DN_HEADS
    o_f = o_gate + w + 2 * w + 3 * w
    o_gates = o_f + FOX_HEADS
    w_big = jnp.concatenate([w_in[..., :o_dt], w_in[..., o_qkv:o_beta], w_in[..., o_gate:o_f], w_in[..., o_gates:]],
                            axis=-1).astype(BF16)
    n_small = SSD_HEADS + 2 * DN_HEADS + FOX_HEADS
    assert (SM_DT, SM_BETA, SM_A, SM_F) == (0, SSD_HEADS, SSD_HEADS + DN_HEADS, SSD_HEADS + 2 * DN_HEADS)
    w_small = jnp.concatenate([w_in[..., o_dt:o_qkv], w_in[..., o_beta:o_a], w_in[..., o_a:o_gate],
                               w_in[..., o_f:o_gates], jnp.zeros((depth, d, LANES - n_small), w_in.dtype)],
                              axis=-1).astype(BF16)
    vec = jnp.concatenate([_lane_row(a["ssd_dt_bias"], SM_DT), _lane_row(a["ssd_a_log"], SM_DT),
                           _lane_row(a["dn_a_log"], SM_A), _lane_row(a["dn_dt_bias"], SM_A),
                           _lane_row(a["fox_f_bias"], SM_F),
                           jnp.zeros((depth, VEC_ROWS - 5, LANES), F32)], axis=1)
    fox_eq, fox_ek, fox_oq, fox_ok = _fox_aug_constants()
    row = lambda v: v[:, None, :]
    return {
        "w_big": w_big, "w_small": w_small, "vec": vec,
        "ssd_conv_w": a["ssd_conv_w"], "ssd_conv_b": row(a["ssd_conv_b"]), "dn_conv_w": a["dn_conv_w"],
        "ssd_dskip": row(jnp.repeat(a["ssd_d"], SSD_HEAD_DIM, axis=1)), "ssd_nw": row(a["ssd_norm_w"]),
        "e8": _expander(SM_DT, SSD_HEADS, SSD_HEAD_DIM), "tri_ssd": _tri(SSD_CHUNK),
        "dn_nw": row(a["dn_norm_w"]),
        "sg_g": row(a["sg_ln_g"]), "sg_b": row(a["sg_ln_b"]), "sg_w": a["sg_w"],
        "sg_bs": jnp.repeat(jnp.swapaxes(a["sg_b"], 1, 2), BRANCH_W // SG_GROUPS, axis=2),
        "tri_fox": _tri(FOX_CUM_CHUNK), "fox_eq": fox_eq, "fox_ek": fox_ek, "fox_oq": fox_oq, "fox_ok": fox_ok,
        "gate_b": a["gate_b"], "w_branch": a["w_branch"].astype(BF16), "w_out": a["w_out"].astype(BF16),
        "ln1_g": row(a["ln1_g"]), "ln1_b": row(a["ln1_b"]),
        "w_up": a["w_up"].astype(BF16), "w_down": a["w_down"].astype(BF16),
        "ln2_g": row(a["ln2_g"]), "ln2_b": row(a["ln2_b"]),
    }


def _tiles(batch, seq):
    t = batch * seq
    pick = lambda n, cap: math.gcd(n, cap)
    return {
        "ln": pick(t, 1024), "proj_m": pick(t, 2048), "proj_n": 2048, "ssd": pick(seq, 512), "dn": pick(seq, 256),
        "sg": pick(t, 2048), "fcum": pick(seq, 1024), "fox_q": pick(seq, 1024), "fox_k": pick(seq, 512), "merge": pick(t, 1024),
        "ffn_m": pick(t, 1024), "ffn_f": 2048,
    }


def kernel(x, ln_in_g, ln_in_b, w_in, ssd_conv_w, ssd_conv_b, ssd_dt_bias, ssd_a_log, ssd_d, ssd_norm_w, dn_conv_w,
           dn_a_log, dn_dt_bias, dn_norm_w, sg_ln_g, sg_ln_b, sg_w, sg_b, fox_f_bias, gate_b, w_branch, w_out,
           ln1_g, ln1_b, w_up, w_down, ln2_g, ln2_b):
    batch, seq, d = x.shape
    depth = w_in.shape[0]
    alpha = (2 * depth) ** 0.25
    a = dict(w_in=w_in, ssd_conv_w=ssd_conv_w, ssd_conv_b=ssd_conv_b, ssd_dt_bias=ssd_dt_bias, ssd_a_log=ssd_a_log,
             ssd_d=ssd_d, ssd_norm_w=ssd_norm_w, dn_conv_w=dn_conv_w, dn_a_log=dn_a_log, dn_dt_bias=dn_dt_bias,
             dn_norm_w=dn_norm_w, sg_ln_g=sg_ln_g, sg_ln_b=sg_ln_b, sg_w=sg_w, sg_b=sg_b, fox_f_bias=fox_f_bias,
             gate_b=gate_b, w_branch=w_branch, w_out=w_out, ln1_g=ln1_g, ln1_b=ln1_b, w_up=w_up, w_down=w_down,
             ln2_g=ln2_g, ln2_b=ln2_b)
    tl = _tiles(batch, seq)
    t = batch * seq
    h32, h16 = _entry_norm(x.reshape(t, d), ln_in_g, ln_in_b, tl["ln"])
    p = _prepare_params(a)
    for l in range(depth):
        proj, small = _in_proj(h16, p, l, tl["proj_m"], tl["proj_n"])
        y_a = _ssd_mixer(proj, small, p, l, batch, seq, tl["ssd"])
        y_b = _dn_mixer(proj, small, p, l, batch, seq, tl["dn"])
        y_c = _sg_mixer(proj, p, l, t, tl["sg"])
        qa, kt = _forget_cumsum(proj, small, p, l, batch, seq, tl["fcum"])
        y_d = _fox_mixer(proj, qa, kt, batch, seq, tl["fox_q"], tl["fox_k"])
        h32, h16 = _merge((y_a, y_b, y_c, y_d), proj, h32, p, l, alpha, tl["merge"])
        h32, h16 = _ffn(h16, h32, p, l, alpha, tl["ffn_m"], tl["ffn_f"])
    return h32.reshape(batch, seq, d)
```

```python
import functools
import math

import numpy as np
import jax
import jax.numpy as jnp
from jax import lax
from jax.experimental import pallas as pl
from jax.experimental.pallas import tpu as pltpu

F32 = jnp.float32
BF16 = jnp.bfloat16

BRANCH_W = 512
N_BRANCH = 4
SSD_HEADS, SSD_HEAD_DIM, SSD_GROUPS, SSD_STATE, SSD_CONV = 8, 64, 2, 128, 4
SSD_XBC = BRANCH_W + 2 * SSD_GROUPS * SSD_STATE
DN_HEADS, DN_HEAD_DIM, DN_CONV = 4, 128, 4
SG_GROUPS, SG_CHUNK = 4, 128
FOX_HEADS, FOX_HEAD_DIM = 8, 64
LN_EPS = 1e-5
NORM_EPS = 1e-6
NEG_BIG = -1e30

LANES = 128
SUBLANES = 8
VMEM_LIMIT_BYTES = 56 * 1024 * 1024

SM_DT, SM_BETA, SM_A, SM_F = 0, 8, 12, 16

SSD_CHUNK = 128
DN_CHUNK = 64


def _dot(a, b):
    return jnp.dot(a, b, preferred_element_type=F32)


def _dot_nt(a, b):
    return lax.dot_general(a, b, (((1,), (1,)), ((), ())), preferred_element_type=F32)


def _dot_tn(a, b):
    return lax.dot_general(a, b, (((0,), (0,)), ((), ())), preferred_element_type=F32)


def _split_terms(x, n):
    terms, r = [], x
    for i in range(n):
        p = r.astype(BF16)
        terms.append(p)
        if i + 1 < n:
            r = r - p.astype(F32)
    return terms


def _sel_right(x, m, n):
    return sum(_dot(p, m) for p in _split_terms(x, n))


def _sel_left(m, x, n):
    return sum(_dot(m, p) for p in _split_terms(x, n))


def _sigmoid(x):
    return 0.5 * jnp.tanh(0.5 * x) + 0.5


def _silu(x):
    hx = 0.5 * x
    return hx + hx * jnp.tanh(hx)


def _softplus(x):
    return jnp.maximum(x, 0.0) + jnp.log1p(jnp.exp(-jnp.abs(x)))


def _layer_norm(x, g, b):
    mu = jnp.mean(x, axis=-1, keepdims=True)
    xc = x - mu
    var = jnp.mean(xc * xc, axis=-1, keepdims=True)
    return xc * lax.rsqrt(var + LN_EPS) * g + b


def _causal_conv(x, tail, w):
    n = x.shape[0]
    k = w.shape[0]
    row8 = lax.broadcasted_iota(jnp.int32, (SUBLANES, x.shape[1]), 0)
    acc = x * w[k - 1:k, :]
    for s in range(1, k):
        xr = pltpu.roll(x, s, axis=0)
        pr = pltpu.roll(tail, s, axis=0)
        head = jnp.where(row8 < s, pr, xr[:SUBLANES])
        xs = jnp.concatenate([head, xr[SUBLANES:]], axis=0)
        acc = acc + xs * w[k - 1 - s:k - s, :]
    return acc, x[n - SUBLANES:]


def _cparams(sem):
    return pltpu.CompilerParams(dimension_semantics=sem, vmem_limit_bytes=VMEM_LIMIT_BYTES)


def _const_spec(shape):
    nd = len(shape)
    return pl.BlockSpec(shape, lambda *_: (0,) * nd)


def _ln_kernel(x_ref, g_ref, b_ref, o32_ref, o16_ref):
    y = _layer_norm(x_ref[...], g_ref[...], b_ref[...])
    o32_ref[...] = y
    o16_ref[...] = y.astype(BF16)


def _entry_norm(x2, g, b, tm):
    t, d = x2.shape
    row = pl.BlockSpec((tm, d), lambda i: (i, 0))
    return pl.pallas_call(
        _ln_kernel,
        grid=(t // tm,),
        in_specs=[row, _const_spec((1, d)), _const_spec((1, d))],
        out_specs=[row, row],
        out_shape=[jax.ShapeDtypeStruct((t, d), F32), jax.ShapeDtypeStruct((t, d), BF16)],
        compiler_params=_cparams(("parallel",)),
        name="entry_norm",
    )(x2, g.reshape(1, d), b.reshape(1, d))


COL_Z, COL_X, COL_BC, COL_DQ, COL_DK, COL_DV, COL_DG, COL_SU, COL_SV, COL_FQ, COL_FK, COL_FV, COL_GATE = range(13)


def _proj_kernel(h_ref, w_ref, ws_ref, o_ref, os_ref):
    h = h_ref[...]
    o_ref[...] = _dot(h, w_ref[...]).astype(BF16)

    @pl.when(pl.program_id(1) == 0)
    def _():
        os_ref[...] = _dot(h, ws_ref[...])


def _layer_spec(l, shape):
    nd = len(shape)
    return pl.BlockSpec((None,) + tuple(shape), lambda *_: (l,) + (0,) * nd)


def _in_proj(h16, p, l, tm, tn):
    t, d = h16.shape
    w_big, w_small = p["w_big"], p["w_small"]
    n = w_big.shape[2]
    return pl.pallas_call(
        _proj_kernel,
        grid=(t // tm, n // tn),
        in_specs=[pl.BlockSpec((tm, d), lambda i, j: (i, 0)),
                  pl.BlockSpec((None, d, tn), lambda i, j: (l, 0, j)),
                  _layer_spec(l, (d, LANES))],
        out_specs=[pl.BlockSpec((tm, tn), lambda i, j: (i, j)),
                   pl.BlockSpec((tm, LANES), lambda i, j: (i, 0))],
        out_shape=[jax.ShapeDtypeStruct((t, n), BF16), jax.ShapeDtypeStruct((t, LANES), F32)],
        compiler_params=_cparams(("parallel", "arbitrary")),
        name="in_proj",
    )(h16, w_big, w_small)


def _ssd_kernel(z_ref, x_ref, bc_ref, sm_ref, cwx_ref, cwbc_ref, cbx_ref, cbbc_ref, vec_ref,
                dskip_ref, nw_ref, e8_ref, tri_ref, o_ref, state_ref, xtail_ref, bctail_ref):
    dtb_ref = vec_ref.at[VEC_SSD_DTB:VEC_SSD_DTB + 1]
    alog_ref = vec_ref.at[VEC_SSD_ALOG:VEC_SSD_ALOG + 1]
    first = pl.program_id(1) == 0
    ts = x_ref.shape[0]
    q = SSD_CHUNK
    hp = SSD_HEADS // SSD_GROUPS * SSD_HEAD_DIM
    e8 = e8_ref[...]
    tri = tri_ref[...]

    @pl.when(first)
    def _():
        state_ref[...] = jnp.zeros_like(state_ref)
        xtail_ref[...] = jnp.zeros_like(xtail_ref)
        bctail_ref[...] = jnp.zeros_like(bctail_ref)

    xs, xtail_ref[...] = _causal_conv(x_ref[...].astype(F32), xtail_ref[...], cwx_ref[...])
    xs = _silu(xs + cbx_ref[...])
    bcm, bctail_ref[...] = _causal_conv(bc_ref[...].astype(F32), bctail_ref[...], cwbc_ref[...])
    bcm = _silu(bcm + cbbc_ref[...]).astype(BF16)
    lane = lax.broadcasted_iota(jnp.int32, (1, LANES), 1)
    head_lane = (lane >= SM_DT) & (lane < SM_DT + SSD_HEADS)
    a_neg = jnp.where(head_lane, -jnp.exp(alog_ref[...]), 0.0)
    dt = _softplus(sm_ref[...] + dtb_ref[...])
    dta = dt * a_neg
    xdt = xs * _sel_right(dt, e8, 2)
    ri = lax.broadcasted_iota(jnp.int32, (q, q), 0)
    ci = lax.broadcasted_iota(jnp.int32, (q, q), 1)
    lower = ci <= ri
    lane_g = lax.broadcasted_iota(jnp.int32, (1, hp), 1) // SSD_HEAD_DIM

    for c in range(ts // q):
        sl = slice(c * q, (c + 1) * q)
        acs = _sel_left(tri, dta[sl], 3)
        acs_t = acs.T
        eacs = _sel_right(jnp.exp(acs), e8, 2)
        dec_end = _sel_right(jnp.exp(acs[q - 1:q, :] - acs), e8, 2)
        ys = []
        for g in range(SSD_GROUPS):
            gl = slice(g * hp, (g + 1) * hp)
            bm = bcm[sl, g * SSD_STATE:(g + 1) * SSD_STATE]
            cm = bcm[sl, (SSD_GROUPS + g) * SSD_STATE:(SSD_GROUPS + g + 1) * SSD_STATE]
            cb = _dot_nt(cm, bm)
            xg = xdt[sl, gl]
            s_prev = state_ref[g]
            y = _dot(cm, s_prev.astype(BF16)) * eacs[:, gl]
            for e in range(SSD_HEADS // SSD_GROUPS):
                h = SM_DT + g * (SSD_HEADS // SSD_GROUPS) + e
                seg = jnp.exp(jnp.where(lower, acs[:, h:h + 1] - acs_t[h:h + 1, :], NEG_BIG))
                xe = jnp.where(lane_g == e, xg, 0.0).astype(BF16)
                y = y + _dot((cb * seg).astype(BF16), xe)
            state_ref[g] = s_prev * eacs[q - 1:q, gl] + _dot_tn(bm, (xg * dec_end[:, gl]).astype(BF16))
            ys.append(y)
        y = jnp.concatenate(ys, axis=1) + dskip_ref[...] * xs[sl]
        y = y * _silu(z_ref[sl, :].astype(F32))
        y = y * lax.rsqrt(jnp.mean(y * y, axis=-1, keepdims=True) + NORM_EPS) * nw_ref[...]
        o_ref[sl, :] = y.astype(BF16)


def _ssd_mixer(proj, small, p, l, batch, seq, ts):
    nb = seq // ts
    w = BRANCH_W
    rows = lambda col: pl.BlockSpec((ts, w), lambda b, s, col=col: (b * nb + s, col))
    half = lambda r, n: pl.BlockSpec((None, r, w), lambda b, s, n=n: (l, 0, n))
    hp = SSD_HEADS // SSD_GROUPS * SSD_HEAD_DIM
    return pl.pallas_call(
        _ssd_kernel,
        grid=(batch, nb),
        in_specs=[rows(COL_Z), rows(COL_X), rows(COL_BC),
                  pl.BlockSpec((ts, LANES), lambda b, s: (b * nb + s, 0)),
                  half(SSD_CONV, 0), half(SSD_CONV, 1), half(1, 0), half(1, 1),
                  _layer_spec(l, (VEC_ROWS, LANES)),
                  _layer_spec(l, (1, w)), _layer_spec(l, (1, w)),
                  _const_spec((LANES, w)), _const_spec((SSD_CHUNK, SSD_CHUNK))],
        out_specs=pl.BlockSpec((ts, w), lambda b, s: (b * nb + s, 0)),
        out_shape=jax.ShapeDtypeStruct((batch * seq, w), BF16),
        scratch_shapes=[pltpu.VMEM((SSD_GROUPS, SSD_STATE, hp), F32),
                        pltpu.VMEM((SUBLANES, w), F32), pltpu.VMEM((SUBLANES, w), F32)],
        compiler_params=_cparams(("parallel", "arbitrary")),
        name="ssd_mixer",
    )(proj, proj, proj, small, p["ssd_conv_w"], p["ssd_conv_w"], p["ssd_conv_b"], p["ssd_conv_b"], p["vec"],
      p["ssd_dskip"], p["ssd_nw"], p["e8"], p["tri_ssd"])


DN_SLOTS = 2
DN_STAGGER = 9


def _interleave_staggered(gens, stagger):
    active, waiting, rounds = [], list(gens), 0
    while active or waiting:
        if waiting and rounds % stagger == 0:
            active.append(waiting.pop(0))
        for g in list(active):
            try:
                next(g)
            except StopIteration:
                active.remove(g)
        rounds += 1


def _dn_kernel(q_ref, k_ref, v_ref, gate_ref, sm_ref, cwq_ref, cwk_ref, cwv_ref, vec_ref, nw_ref,
               tri_ref, hsum_ref, o_ref, state_ref, qtail_ref, ktail_ref, vtail_ref):
    alog_ref = vec_ref.at[VEC_DN_ALOG:VEC_DN_ALOG + 1]
    dtb_ref = vec_ref.at[VEC_DN_DTB:VEC_DN_DTB + 1]
    ts = q_ref.shape[1]
    c = DN_CHUNK
    dk = DN_HEAD_DIM
    nh = DN_HEADS

    @pl.when(pl.program_id(1) == 0)
    def _():
        state_ref[...] = jnp.zeros_like(state_ref)
        qtail_ref[...] = jnp.zeros_like(qtail_ref)
        ktail_ref[...] = jnp.zeros_like(ktail_ref)
        vtail_ref[...] = jnp.zeros_like(vtail_ref)

    def row_work(r):
        qf, qtail_ref[r] = _causal_conv(q_ref[r].astype(F32), qtail_ref[r], cwq_ref[...])
        qf = _silu(qf)
        yield
        kf, ktail_ref[r] = _causal_conv(k_ref[r].astype(F32), ktail_ref[r], cwk_ref[...])
        kf = _silu(kf)
        yield
        vf, vtail_ref[r] = _causal_conv(v_ref[r].astype(F32), vtail_ref[r], cwv_ref[...])
        vf = _silu(vf)
        yield
        qf = qf * (lax.rsqrt(_sel_right(qf * qf, hsum_ref[...], 1) + NORM_EPS) * (dk ** -0.5))
        kf = kf * lax.rsqrt(_sel_right(kf * kf, hsum_ref[...], 1) + NORM_EPS)
        yield
        sm = sm_ref[r]
        lane = lax.broadcasted_iota(jnp.int32, (1, LANES), 1)
        a_lane = (lane >= SM_A) & (lane < SM_A + nh)
        neg_a = jnp.where(a_lane, -jnp.exp(alog_ref[...]), 0.0)
        gdec = neg_a * _softplus(sm + dtb_ref[...])
        beta_s = _sigmoid(sm)
        gcs = _sel_left(tri_ref[...], gdec, 3)
        gtot = jnp.concatenate([jnp.broadcast_to(gcs[(n + 1) * c - 1:(n + 1) * c], (c, LANES))
                                for n in range(ts // c)], axis=0)
        gcs_t = gcs.T
        egcs_s = jnp.exp(gcs)
        edec_s = jnp.exp(gtot - gcs)
        elast_s = jnp.exp(gtot)
        ri = lax.broadcasted_iota(jnp.int32, (ts, ts), 0)
        ci = lax.broadcasted_iota(jnp.int32, (ts, ts), 1)
        lower = (ci >= ri // c * c) & (ci <= ri)
        offdiag = ci != ri
        yield

        def head_lanes(x, lane_idx):
            return jnp.broadcast_to(x[:, lane_idx:lane_idx + 1], (x.shape[0], dk))

        a_mats, rhs, qk, qg, kd, elast = [], [], [], [], [], []
        for h in range(nh):
            hl = slice(h * dk, (h + 1) * dk)
            gl = SM_A + h
            q, k = qf[:, hl], kf[:, hl]
            bh = head_lanes(beta_s, SM_BETA + h)
            egcs = head_lanes(egcs_s, gl)
            gamma = jnp.exp(jnp.where(lower, gcs[:, gl:gl + 1] - gcs_t[gl:gl + 1, :], NEG_BIG))
            kb = k * bh
            k16 = k.astype(BF16)
            a_mats.append(jnp.where(offdiag, _dot_nt(kb.astype(BF16), k16) * gamma, 0.0))
            yield
            rhs.append(jnp.concatenate([kb * egcs, vf[:, hl] * bh], axis=1).astype(BF16))
            qk.append((_dot_nt(q.astype(BF16), k16) * gamma).astype(BF16))
            yield
            qg.append((q * egcs).astype(BF16))
            kd.append((k * head_lanes(edec_s, gl)).astype(BF16))
            elast.append([head_lanes(elast_s[n * c:n * c + 1], gl) for n in range(ts // c)])
            yield

        eye = jnp.where(offdiag, 0.0, 1.0).astype(F32)
        pws = [(-a).astype(BF16) for a in a_mats]
        invs = [eye - a for a in a_mats]
        for _ in range(int(math.log2(c)) - 1):
            sq = [_dot(pw, pw) for pw in pws]
            yield
            pws = [x.astype(BF16) for x in sq]
            invs = [inv + _dot(inv.astype(BF16), pw) for inv, pw in zip(invs, pws)]
            yield
        wu = [_dot(invs[h].astype(BF16), rhs[h]) for h in range(nh)]
        yield

        pn = [[_dot_tn(kd[h][ch * c:(ch + 1) * c], wu[h][ch * c:(ch + 1) * c].astype(BF16))
               for ch in range(ts // c)] for h in range(nh)]
        yield
        states = [state_ref[r, h] for h in range(nh)]
        v_new = [[] for _ in range(nh)]
        o_state = [[] for _ in range(nh)]
        for ch in range(ts // c):
            sl = slice(ch * c, (ch + 1) * c)
            for h in range(nh):
                s16 = states[h].astype(BF16)
                states[h] = (states[h] * elast[h][ch] - _dot(pn[h][ch][:, :dk].astype(BF16), s16)
                             + pn[h][ch][:, dk:])
                lhs = jnp.concatenate([wu[h][sl, :dk].astype(BF16), qg[h][sl]], axis=0)
                ws = _dot(lhs, s16)
                v_new[h].append(wu[h][sl, dk:] - ws[:c])
                o_state[h].append(ws[c:])
            yield
        for h in range(nh):
            hl = slice(h * dk, (h + 1) * dk)
            state_ref[r, h] = states[h]
            vn_all = jnp.concatenate(v_new[h], axis=0).astype(BF16)
            o = jnp.concatenate(o_state[h], axis=0) + _dot(qk[h], vn_all)
            o = o * lax.rsqrt(jnp.mean(o * o, axis=-1, keepdims=True) + NORM_EPS) * nw_ref[...]
            o_ref[r, :, hl] = (o * _silu(gate_ref[r, :, hl].astype(F32))).astype(BF16)
            yield

    _interleave_staggered([row_work(r) for r in range(q_ref.shape[0])], DN_STAGGER)


def _dn_mixer(proj, small, p, l, batch, seq, ts):
    nb = seq // ts
    w = BRANCH_W
    nh, dk = DN_HEADS, DN_HEAD_DIM
    nr = math.gcd(batch, DN_SLOTS)
    proj3 = proj.reshape(batch, seq, proj.shape[1])
    small3 = small.reshape(batch, seq, LANES)
    rows = lambda col: pl.BlockSpec((nr, ts, w), lambda b, s, col=col: (b, s, col))
    conv = lambda n: pl.BlockSpec((None, DN_CONV, w), lambda b, s, n=n: (l, 0, n))
    out = pl.pallas_call(
        _dn_kernel,
        grid=(batch // nr, nb),
        in_specs=[rows(COL_DQ), rows(COL_DK), rows(COL_DV), rows(COL_DG),
                  pl.BlockSpec((nr, ts, LANES), lambda b, s: (b, s, 0)),
                  conv(0), conv(1), conv(2),
                  _layer_spec(l, (VEC_ROWS, LANES)), _layer_spec(l, (1, dk)),
                  _const_spec((ts, ts)), _const_spec((w, w))],
        out_specs=pl.BlockSpec((nr, ts, w), lambda b, s: (b, s, 0)),
        out_shape=jax.ShapeDtypeStruct((batch, seq, w), BF16),
        scratch_shapes=[pltpu.VMEM((nr, nh, dk, dk), F32),
                        pltpu.VMEM((nr, SUBLANES, w), F32), pltpu.VMEM((nr, SUBLANES, w), F32),
                        pltpu.VMEM((nr, SUBLANES, w), F32)],
        compiler_params=_cparams(("parallel", "arbitrary")),
        name="dn_mixer",
    )(proj3, proj3, proj3, proj3, small3, p["dn_conv_w"], p["dn_conv_w"], p["dn_conv_w"], p["vec"],
      p["dn_nw"], _block_diag(ts, DN_CHUNK, True), _block_diag(w, DN_HEAD_DIM, False))
    return out.reshape(batch * seq, w)


def _sg_kernel(u_ref, v_ref, g_ref, b_ref, w_ref, bs_ref, o_ref):
    ts = u_ref.shape[0]
    q = SG_CHUNK
    gd = BRANCH_W // SG_GROUPS
    u = jax.nn.gelu(u_ref[...].astype(F32))
    v = _layer_norm(jax.nn.gelu(v_ref[...].astype(F32)), g_ref[...], b_ref[...])
    ri = lax.broadcasted_iota(jnp.int32, (q, q), 0)
    ci = lax.broadcasted_iota(jnp.int32, (q, q), 1)
    lower = ci <= ri
    for g in range(SG_GROUPS):
        gl = slice(g * gd, (g + 1) * gd)
        wg = jnp.where(lower, w_ref[g], 0.0).astype(BF16)
        for c in range(ts // q):
            sl = slice(c * q, (c + 1) * q)
            mixed = _dot(wg, v[sl, gl].astype(BF16)) + bs_ref[:, gl]
            o_ref[sl, gl] = (u[sl, gl] * mixed).astype(BF16)


def _sg_mixer(proj, p, l, t, ts):
    w = BRANCH_W
    rows = lambda col: pl.BlockSpec((ts, w), lambda i, col=col: (i, col))
    return pl.pallas_call(
        _sg_kernel,
        grid=(t // ts,),
        in_specs=[rows(COL_SU), rows(COL_SV), _layer_spec(l, (1, w)), _layer_spec(l, (1, w)),
                  _layer_spec(l, (SG_GROUPS, SG_CHUNK, SG_CHUNK)), _layer_spec(l, (SG_CHUNK, w))],
        out_specs=pl.BlockSpec((ts, w), lambda i: (i, 0)),
        out_shape=jax.ShapeDtypeStruct((t, w), BF16),
        compiler_params=_cparams(("parallel",)),
        name="sg_mixer",
    )(proj, proj, p["sg_g"], p["sg_b"], p["sg_w"], p["sg_bs"])


FOX_SPLIT = 3
FOX_CUM_CHUNK = 128
FOX_AUG = LANES // (2 * FOX_HEADS)
assert FOX_AUG >= 2 * FOX_SPLIT
LOG2E = math.log2(math.e)


def _fcum_kernel(sm_ref, k_ref, vec_ref, tri_ref, eq_ref, ek_ref, oq_ref, ok_ref, qa_ref, kt_ref, carry_ref):
    fb_ref = vec_ref.at[VEC_FOX_FB:VEC_FOX_FB + 1]
    @pl.when(pl.program_id(1) == 0)
    def _():
        carry_ref[...] = jnp.zeros_like(carry_ref)

    n = sm_ref.shape[0]
    per = LANES // FOX_HEAD_DIM
    ls = jax.nn.log_sigmoid(sm_ref[...] + fb_ref[...])
    tri = tri_ref[...]
    carry = carry_ref[...]
    pieces = []
    for r in range(n // FOX_CUM_CHUNK):
        piece = _sel_left(tri, ls[r * FOX_CUM_CHUNK:(r + 1) * FOX_CUM_CHUNK], 3) + carry
        carry = piece[FOX_CUM_CHUNK - 1:FOX_CUM_CHUNK, :]
        pieces.append(piece)
    c = jnp.concatenate(pieces, axis=0)
    carry_ref[...] = carry
    terms = _split_terms(c * LOG2E, FOX_SPLIT)
    cq = sum(_dot(t, eq_ref[i]) for i, t in enumerate(terms))
    qa_ref[...] = (oq_ref[...] + cq).astype(BF16)
    ck_t = sum(_dot_nt(ek_ref[i], t) for i, t in enumerate(terms))
    aug_t = (ok_ref[...] - ck_t).astype(BF16)
    pad = jnp.zeros((LANES - FOX_HEAD_DIM - FOX_AUG, n), BF16)
    for b in range(FOX_HEADS // per):
        k_t = k_ref[:, b * LANES:(b + 1) * LANES].astype(F32).T.astype(BF16)
        for e in range(per):
            h = b * per + e
            kt_ref[h] = jnp.concatenate([k_t[e * FOX_HEAD_DIM:(e + 1) * FOX_HEAD_DIM],
                                         aug_t[h * FOX_AUG:(h + 1) * FOX_AUG], pad], axis=0)


def _forget_cumsum(proj, small, p, l, batch, seq, tc):
    nb = seq // tc
    na = FOX_HEADS * FOX_AUG
    return pl.pallas_call(
        _fcum_kernel,
        grid=(batch, nb),
        in_specs=[pl.BlockSpec((tc, LANES), lambda b, s: (b * nb + s, 0)),
                  pl.BlockSpec((tc, BRANCH_W), lambda b, s: (b * nb + s, COL_FK)),
                  _layer_spec(l, (VEC_ROWS, LANES)), _const_spec((FOX_CUM_CHUNK, FOX_CUM_CHUNK)),
                  _const_spec((FOX_SPLIT, LANES, LANES)), _const_spec((FOX_SPLIT, na, LANES)),
                  _const_spec((1, LANES)), _const_spec((na, 1))],
        out_specs=[pl.BlockSpec((tc, LANES), lambda b, s: (b * nb + s, 0)),
                   pl.BlockSpec((None, FOX_HEADS, LANES, tc), lambda b, s: (b, 0, 0, s))],
        out_shape=[jax.ShapeDtypeStruct((batch * seq, LANES), BF16),
                   jax.ShapeDtypeStruct((batch, FOX_HEADS, LANES, seq), BF16)],
        scratch_shapes=[pltpu.VMEM((1, LANES), F32)],
        compiler_params=_cparams(("parallel", "arbitrary")),
        name="forget_cumsum",
    )(small, proj, p["vec"], p["tri_fox"], p["fox_eq"], p["fox_ek"], p["fox_oq"], p["fox_ok"])


def _roll_lanes(x, shift):
    shift %= x.shape[-1]
    return x if shift == 0 else pltpu.roll(x, shift, axis=x.ndim - 1)


def _fox_kernel(ii_ref, jj_ref, q_ref, kt_ref, v_ref, qa_ref, o_ref, qaug_ref, m_ref, acc_ref):
    t = pl.program_id(1)
    i = ii_ref[t]
    j = jj_ref[t]
    tq = q_ref.shape[0]
    per = LANES // FOX_HEAD_DIM
    lane = lax.broadcasted_iota(jnp.int32, (1, LANES), 1)

    @pl.when(j == 0)
    def _():
        m_ref[...] = jnp.full_like(m_ref, NEG_BIG)
        acc_ref[...] = jnp.zeros_like(acc_ref)
        qa = qa_ref[...].astype(F32)
        for h in range(FOX_HEADS):
            bl = slice(h // per * LANES, (h // per + 1) * LANES)
            q2 = (q_ref[:, bl].astype(F32) * (LOG2E * FOX_HEAD_DIM ** -0.5)).astype(BF16).astype(F32)
            q_own = _roll_lanes(q2, LANES - h % per * FOX_HEAD_DIM)
            extra = _roll_lanes(qa, FOX_HEAD_DIM - h * FOX_AUG)
            in_extra = (lane >= FOX_HEAD_DIM) & (lane < FOX_HEAD_DIM + FOX_AUG)
            qaug_ref[h] = jnp.where(lane < FOX_HEAD_DIM, q_own, jnp.where(in_extra, extra, 0.0)).astype(BF16)

    tk = kt_ref.shape[2]
    nsub = tq // tk

    def logits(h, rows):
        return _dot(qaug_ref[h, rows, :], kt_ref[h])

    def softmax(h, rows, s, masked):
        if masked:
            s = jnp.where(lax.broadcasted_iota(jnp.int32, (tk, tk), 1) <= lax.broadcasted_iota(jnp.int32, (tk, tk), 0),
                          s, NEG_BIG)
        m_old = m_ref[h, rows, :]
        m_new = jnp.maximum(m_old, jnp.max(s, axis=-1, keepdims=True))
        m_ref[h, rows, :] = m_new
        return jnp.exp2(s - m_new[:, :1]).astype(BF16), jnp.exp2(m_old - m_new)

    def accumulate(h, rows, p, alpha):
        bl = slice(h // per * LANES, (h // per + 1) * LANES)
        v2 = v_ref[:, bl]
        vaug = jnp.concatenate([v2, jnp.ones_like(v2)], axis=1)
        acc_ref[h, rows, :] = jnp.concatenate([alpha, alpha], axis=1) * acc_ref[h, rows, :] + _dot(p, vaug)

    def step(first_sub):
        units = [(h, r) for h in range(FOX_HEADS) for r in range(max(first_sub, 0), nsub)]
        rows = lambda r: slice(r * tk, (r + 1) * tk)
        s_next = logits(units[0][0], rows(units[0][1]))
        for n, (h, r) in enumerate(units):
            s = s_next
            if n + 1 < len(units):
                s_next = logits(units[n + 1][0], rows(units[n + 1][1]))
            accumulate(h, rows(r), *softmax(h, rows(r), s, r == first_sub))

    d = j - i * nsub

    @pl.when(d < 0)
    def _():
        step(-1)

    for k in range(nsub):
        @pl.when(d == k)
        def _(k=k):
            step(k)

    @pl.when(d == nsub - 1)
    def _():
        for b in range(FOX_HEADS // per):
            outs = []
            for e in range(per):
                a = acc_ref[b * per + e]
                outs.append(a[:, :LANES] / a[:, LANES:])
            o_ref[:, b * LANES:(b + 1) * LANES] = jnp.where(lane // FOX_HEAD_DIM == 0, outs[0], outs[1]).astype(BF16)


def _fox_mixer(proj, qa, kt, batch, seq, tq, tk):
    nq, nk = seq // tq, seq // tk
    w = BRANCH_W
    pairs = [(i, j) for i in range(nq) for j in range((i + 1) * (tq // tk))]
    ii = jnp.asarray([p[0] for p in pairs], jnp.int32)
    jj = jnp.asarray([p[1] for p in pairs], jnp.int32)
    qrow = lambda b, t, ii, jj: b * nq + ii[t]
    krow = lambda b, t, ii, jj: b * nk + jj[t]
    grid_spec = pltpu.PrefetchScalarGridSpec(
        num_scalar_prefetch=2,
        grid=(batch, len(pairs)),
        in_specs=[pl.BlockSpec((tq, w), lambda b, t, ii, jj: (qrow(b, t, ii, jj), COL_FQ)),
                  pl.BlockSpec((None, FOX_HEADS, LANES, tk), lambda b, t, ii, jj: (b, 0, 0, jj[t])),
                  pl.BlockSpec((tk, w), lambda b, t, ii, jj: (krow(b, t, ii, jj), COL_FV)),
                  pl.BlockSpec((tq, LANES), lambda b, t, ii, jj: (qrow(b, t, ii, jj), 0))],
        out_specs=pl.BlockSpec((tq, w), lambda b, t, ii, jj: (qrow(b, t, ii, jj), 0)),
        scratch_shapes=[pltpu.VMEM((FOX_HEADS, tq, LANES), BF16),
                        pltpu.VMEM((FOX_HEADS, tq, LANES), F32),
                        pltpu.VMEM((FOX_HEADS, tq, 2 * LANES), F32)])
    return pl.pallas_call(
        _fox_kernel,
        grid_spec=grid_spec,
        out_shape=jax.ShapeDtypeStruct((batch * seq, w), BF16),
        compiler_params=_cparams(("parallel", "arbitrary")),
        name="fox_mixer",
    )(ii, jj, proj, kt, proj, qa)


def _merge_kernel(alpha, ya_ref, yb_ref, yc_ref, yd_ref, g0_ref, g1_ref, g2_ref, g3_ref, gb_ref,
                  wb_ref, wo_ref, h_ref, lg_ref, lb_ref, o32_ref, o16_ref):
    merged = None
    for i, (y_ref, gl_ref) in enumerate(((ya_ref, g0_ref), (yb_ref, g1_ref), (yc_ref, g2_ref), (yd_ref, g3_ref))):
        gate = _sigmoid(gl_ref[...].astype(F32) + gb_ref[i:i + 1, :])
        term = gate * _dot(y_ref[...], wb_ref[i])
        merged = term if merged is None else merged + term
    mix = _dot(merged.astype(BF16), wo_ref[...])
    y = _layer_norm(alpha * h_ref[...] + mix, lg_ref[...], lb_ref[...])
    o32_ref[...] = y
    o16_ref[...] = y.astype(BF16)


def _merge(ys, proj, h32, p, l, alpha, tm):
    t, d = h32.shape
    w = BRANCH_W
    gate0 = (COL_GATE * w) // d
    yspec = pl.BlockSpec((tm, w), lambda i: (i, 0))
    gspec = lambda n: pl.BlockSpec((tm, d), lambda i, n=n: (i, gate0 + n))
    row = pl.BlockSpec((tm, d), lambda i: (i, 0))
    return pl.pallas_call(
        functools.partial(_merge_kernel, alpha),
        grid=(t // tm,),
        in_specs=[yspec] * 4 + [gspec(n) for n in range(N_BRANCH)] +
                 [_layer_spec(l, (N_BRANCH, d)), _layer_spec(l, (N_BRANCH, w, d)), _layer_spec(l, (d, d)), row,
                  _layer_spec(l, (1, d)), _layer_spec(l, (1, d))],
        out_specs=[row, row],
        out_shape=[jax.ShapeDtypeStruct((t, d), F32), jax.ShapeDtypeStruct((t, d), BF16)],
        compiler_params=_cparams(("parallel",)),
        name="merge_norm",
    )(*ys, proj, proj, proj, proj, p["gate_b"], p["w_branch"], p["w_out"], h32, p["ln1_g"], p["ln1_b"])


def _ffn_kernel(alpha, h16_ref, h32_ref, wu_ref, wd_ref, lg_ref, lb_ref, o32_ref, o16_ref, acc_ref):
    f = pl.program_id(1)

    @pl.when(f == 0)
    def _():
        acc_ref[...] = jnp.zeros_like(acc_ref)

    up = jnp.maximum(_dot(h16_ref[...], wu_ref[...]), 0.0)
    acc_ref[...] += _dot((up * up).astype(BF16), wd_ref[...])

    @pl.when(f == pl.num_programs(1) - 1)
    def _():
        y = _layer_norm(alpha * h32_ref[...] + acc_ref[...], lg_ref[...], lb_ref[...])
        o32_ref[...] = y
        o16_ref[...] = y.astype(BF16)


def _ffn(h16, h32, p, l, alpha, tm, tf):
    t, d = h32.shape
    dff = p["w_up"].shape[2]
    row = pl.BlockSpec((tm, d), lambda i, f: (i, 0))
    return pl.pallas_call(
        functools.partial(_ffn_kernel, alpha),
        grid=(t // tm, dff // tf),
        in_specs=[row, row, pl.BlockSpec((None, d, tf), lambda i, f: (l, 0, f)),
                  pl.BlockSpec((None, tf, d), lambda i, f: (l, f, 0)),
                  _layer_spec(l, (1, d)), _layer_spec(l, (1, d))],
        out_specs=[row, row],
        out_shape=[jax.ShapeDtypeStruct((t, d), F32), jax.ShapeDtypeStruct((t, d), BF16)],
        scratch_shapes=[pltpu.VMEM((tm, d), F32)],
        compiler_params=_cparams(("parallel", "arbitrary")),
        name="ffn_norm",
    )(h16, h32, p["w_up"], p["w_down"], p["ln2_g"], p["ln2_b"])


VEC_SSD_DTB, VEC_SSD_ALOG, VEC_DN_ALOG, VEC_DN_DTB, VEC_FOX_FB, VEC_ROWS = 0, 1, 2, 3, 4, SUBLANES


def _lane_row(v, start):
    return jnp.pad(v.astype(F32), ((0, 0), (start, LANES - start - v.shape[1])))[:, None, :]


def _expander(start, heads, width):
    m = np.zeros((LANES, heads * width), np.float32)
    for h in range(heads):
        m[start + h, h * width:(h + 1) * width] = 1.0
    return jnp.asarray(m, BF16)


def _tri(n):
    return jnp.asarray(np.tril(np.ones((n, n), np.float32)), BF16)


def _block_diag(n, c, lower):
    blk = np.tril(np.ones((c, c), np.float32)) if lower else np.ones((c, c), np.float32)
    return jnp.asarray(np.kron(np.eye(n // c, dtype=np.float32), blk), BF16)


def _fox_aug_constants():
    na = FOX_HEADS * FOX_AUG
    eq = np.zeros((FOX_SPLIT, LANES, LANES), np.float32)
    ek = np.zeros((FOX_SPLIT, na, LANES), np.float32)
    oq = np.zeros((1, LANES), np.float32)
    ok = np.zeros((na, 1), np.float32)
    for h in range(FOX_HEADS):
        for i in range(FOX_SPLIT):
            ek[i, h * FOX_AUG + i, SM_F + h] = 1.0
            eq[i, SM_F + h, h * FOX_AUG + FOX_SPLIT + i] = 1.0
            oq[0, h * FOX_AUG + i] = 1.0
            ok[h * FOX_AUG + FOX_SPLIT + i, 0] = 1.0
    return jnp.asarray(eq, BF16), jnp.asarray(ek, BF16), jnp.asarray(oq), jnp.asarray(ok)


def _prepare_params(a):
    w = BRANCH_W
    w_in = a["w_in"]
    depth, d = w_in.shape[:2]
    o_dt = w + SSD_XBC
    o_qkv = o_dt + SSD_HEADS
    o_beta = o_qkv + 3 * w
    o_a = o_beta + DN_HEADS
    o_gate = o_a + DN_HEADS
    o_f = o_gate + w + 2 * w + 3 * w
    o_gates = o_f + FOX_HEADS
    w_big = jnp.concatenate([w_in[..., :o_dt], w_in[..., o_qkv:o_beta], w_in[..., o_gate:o_f], w_in[..., o_gates:]],
                            axis=-1).astype(BF16)
    n_small = SSD_HEADS + 2 * DN_HEADS + FOX_HEADS
    assert (SM_DT, SM_BETA, SM_A, SM_F) == (0, SSD_HEADS, SSD_HEADS + DN_HEADS, SSD_HEADS + 2 * DN_HEADS)
    w_small = jnp.concatenate([w_in[..., o_dt:o_qkv], w_in[..., o_beta:o_a], w_in[..., o_a:o_gate],
                               w_in[..., o_f:o_gates], jnp.zeros((depth, d, LANES - n_small), w_in.dtype)],
                              axis=-1).astype(BF16)
    vec = jnp.concatenate([_lane_row(a["ssd_dt_bias"], SM_DT), _lane_row(a["ssd_a_log"], SM_DT),
                           _lane_row(a["dn_a_log"], SM_A), _lane_row(a["dn_dt_bias"], SM_A),
                           _lane_row(a["fox_f_bias"], SM_F),
                           jnp.zeros((depth, VEC_ROWS - 5, LANES), F32)], axis=1)
    fox_eq, fox_ek, fox_oq, fox_ok = _fox_aug_constants()
    row = lambda v: v[:, None, :]
    return {
        "w_big": w_big, "w_small": w_small, "vec": vec,
        "ssd_conv_w": a["ssd_conv_w"], "ssd_conv_b": row(a["ssd_conv_b"]), "dn_conv_w": a["dn_conv_w"],
        "ssd_dskip": row(jnp.repeat(a["ssd_d"], SSD_HEAD_DIM, axis=1)), "ssd_nw": row(a["ssd_norm_w"]),
        "e8": _expander(SM_DT, SSD_HEADS, SSD_HEAD_DIM), "tri_ssd": _tri(SSD_CHUNK),
        "dn_nw": row(a["dn_norm_w"]),
        "sg_g": row(a["sg_ln_g"]), "sg_b": row(a["sg_ln_b"]), "sg_w": a["sg_w"],
        "sg_bs": jnp.repeat(jnp.swapaxes(a["sg_b"], 1, 2), BRANCH_W // SG_GROUPS, axis=2),
        "tri_fox": _tri(FOX_CUM_CHUNK), "fox_eq": fox_eq, "fox_ek": fox_ek, "fox_oq": fox_oq, "fox_ok": fox_ok,
        "gate_b": a["gate_b"], "w_branch": a["w_branch"].astype(BF16), "w_out": a["w_out"].astype(BF16),
        "ln1_g": row(a["ln1_g"]), "ln1_b": row(a["ln1_b"]),
        "w_up": a["w_up"].astype(BF16), "w_down": a["w_down"].astype(BF16),
        "ln2_g": row(a["ln2_g"]), "ln2_b": row(a["ln2_b"]),
    }


def _tiles(batch, seq):
    t = batch * seq
    pick = lambda n, cap: math.gcd(n, cap)
    return {
        "ln": pick(t, 1024), "proj_m": pick(t, 2048), "proj_n": 2048, "ssd": pick(seq, 512), "dn": pick(seq, 256),
        "sg": pick(t, 2048), "fcum": pick(seq, 1024), "fox_q": pick(seq, 2048), "fox_k": pick(seq, 512), "merge": pick(t, 1024),
        "ffn_m": pick(t, 512), "ffn_f": 4096,
    }


def kernel(x, ln_in_g, ln_in_b, w_in, ssd_conv_w, ssd_conv_b, ssd_dt_bias, ssd_a_log, ssd_d, ssd_norm_w, dn_conv_w,
           dn_a_log, dn_dt_bias, dn_norm_w, sg_ln_g, sg_ln_b, sg_w, sg_b, fox_f_bias, gate_b, w_branch, w_out,
           ln1_g, ln1_b, w_up, w_down, ln2_g, ln2_b):
    batch, seq, d = x.shape
    depth = w_in.shape[0]
    alpha = (2 * depth) ** 0.25
    a = dict(w_in=w_in, ssd_conv_w=ssd_conv_w, ssd_conv_b=ssd_conv_b, ssd_dt_bias=ssd_dt_bias, ssd_a_log=ssd_a_log,
             ssd_d=ssd_d, ssd_norm_w=ssd_norm_w, dn_conv_w=dn_conv_w, dn_a_log=dn_a_log, dn_dt_bias=dn_dt_bias,
             dn_norm_w=dn_norm_w, sg_ln_g=sg_ln_g, sg_ln_b=sg_ln_b, sg_w=sg_w, sg_b=sg_b, fox_f_bias=fox_f_bias,
             gate_b=gate_b, w_branch=w_branch, w_out=w_out, ln1_g=ln1_g, ln1_b=ln1_b, w_up=w_up, w_down=w_down,
             ln2_g=ln2_g, ln2_b=ln2_b)
    tl = _tiles(batch, seq)
    t = batch * seq
    h32, h16 = _entry_norm(x.reshape(t, d), ln_in_g, ln_in_b, tl["ln"])
    p = _prepare_params(a)
    for l in range(depth):
        proj, small = _in_proj(h16, p, l, tl["proj_m"], tl["proj_n"])
        y_a = _ssd_mixer(proj, small, p, l, batch, seq, tl["ssd"])
        y_b = _dn_mixer(proj, small, p, l, batch, seq, tl["dn"])
        y_c = _sg_mixer(proj, p, l, t, tl["sg"])
        qa, kt = _forget_cumsum(proj, small, p, l, batch, seq, tl["fcum"])
        y_d = _fox_mixer(proj, qa, kt, batch, seq, tl["fox_q"], tl["fox_k"])
        h32, h16 = _merge((y_a, y_b, y_c, y_d), proj, h32, p, l, alpha, tl["merge"])
        h32, h16 = _ffn(h16, h32, p, l, alpha, tl["ffn_m"], tl["ffn_f"])
    return h32.reshape(batch, seq, d)
```

```python
import functools
import math

import numpy as np
import jax
import jax.numpy as jnp
from jax import lax
from jax.experimental import pallas as pl
from jax.experimental.pallas import tpu as pltpu

F32 = jnp.float32
BF16 = jnp.bfloat16

BRANCH_W = 512
N_BRANCH = 4
SSD_HEADS, SSD_HEAD_DIM, SSD_GROUPS, SSD_STATE, SSD_CONV = 8, 64, 2, 128, 4
SSD_XBC = BRANCH_W + 2 * SSD_GROUPS * SSD_STATE
DN_HEADS, DN_HEAD_DIM, DN_CONV = 4, 128, 4
SG_GROUPS, SG_CHUNK = 4, 128
FOX_HEADS, FOX_HEAD_DIM = 8, 64
LN_EPS = 1e-5
NORM_EPS = 1e-6
NEG_BIG = -1e30

LANES = 128
SUBLANES = 8
VMEM_LIMIT_BYTES = 56 * 1024 * 1024

SM_DT, SM_BETA, SM_A, SM_F = 0, 8, 12, 16

SSD_CHUNK = 128
DN_CHUNK = 64


def _dot(a, b):
    return jnp.dot(a, b, preferred_element_type=F32)


def _dot_nt(a, b):
    return lax.dot_general(a, b, (((1,), (1,)), ((), ())), preferred_element_type=F32)


def _dot_tn(a, b):
    return lax.dot_general(a, b, (((0,), (0,)), ((), ())), preferred_element_type=F32)


def _split_terms(x, n):
    terms, r = [], x
    for i in range(n):
        p = r.astype(BF16)
        terms.append(p)
        if i + 1 < n:
            r = r - p.astype(F32)
    return terms


def _sel_right(x, m, n):
    return sum(_dot(p, m) for p in _split_terms(x, n))


def _sel_left(m, x, n):
    return sum(_dot(m, p) for p in _split_terms(x, n))


def _sigmoid(x):
    return 0.5 * jnp.tanh(0.5 * x) + 0.5


def _silu(x):
    hx = 0.5 * x
    return hx + hx * jnp.tanh(hx)


def _softplus(x):
    return jnp.maximum(x, 0.0) + jnp.log1p(jnp.exp(-jnp.abs(x)))


def _layer_norm(x, g, b):
    mu = jnp.mean(x, axis=-1, keepdims=True)
    xc = x - mu
    var = jnp.mean(xc * xc, axis=-1, keepdims=True)
    return xc * lax.rsqrt(var + LN_EPS) * g + b


def _causal_conv(x, tail, w):
    n = x.shape[0]
    k = w.shape[0]
    row8 = lax.broadcasted_iota(jnp.int32, (SUBLANES, x.shape[1]), 0)
    acc = x * w[k - 1:k, :]
    for s in range(1, k):
        xr = pltpu.roll(x, s, axis=0)
        pr = pltpu.roll(tail, s, axis=0)
        head = jnp.where(row8 < s, pr, xr[:SUBLANES])
        xs = jnp.concatenate([head, xr[SUBLANES:]], axis=0)
        acc = acc + xs * w[k - 1 - s:k - s, :]
    return acc, x[n - SUBLANES:]


def _cparams(sem):
    return pltpu.CompilerParams(dimension_semantics=sem, vmem_limit_bytes=VMEM_LIMIT_BYTES)


def _const_spec(shape):
    nd = len(shape)
    return pl.BlockSpec(shape, lambda *_: (0,) * nd)


def _ln_kernel(x_ref, g_ref, b_ref, o32_ref, o16_ref):
    y = _layer_norm(x_ref[...], g_ref[...], b_ref[...])
    o32_ref[...] = y
    o16_ref[...] = y.astype(BF16)


def _entry_norm(x2, g, b, tm):
    t, d = x2.shape
    row = pl.BlockSpec((tm, d), lambda i: (i, 0))
    return pl.pallas_call(
        _ln_kernel,
        grid=(t // tm,),
        in_specs=[row, _const_spec((1, d)), _const_spec((1, d))],
        out_specs=[row, row],
        out_shape=[jax.ShapeDtypeStruct((t, d), F32), jax.ShapeDtypeStruct((t, d), BF16)],
        compiler_params=_cparams(("parallel",)),
        name="entry_norm",
    )(x2, g.reshape(1, d), b.reshape(1, d))


COL_Z, COL_X, COL_BC, COL_DQ, COL_DK, COL_DV, COL_DG, COL_SU, COL_SV, COL_FQ, COL_FK, COL_FV, COL_GATE = range(13)


def _proj_kernel(h_ref, w_ref, ws_ref, o_ref, os_ref):
    h = h_ref[...]
    o_ref[...] = _dot(h, w_ref[...]).astype(BF16)

    @pl.when(pl.program_id(1) == 0)
    def _():
        os_ref[...] = _dot(h, ws_ref[...])


def _layer_spec(l, shape):
    nd = len(shape)
    return pl.BlockSpec((None,) + tuple(shape), lambda *_: (l,) + (0,) * nd)


def _in_proj(h16, p, l, tm, tn):
    t, d = h16.shape
    w_big, w_small = p["w_big"], p["w_small"]
    n = w_big.shape[2]
    return pl.pallas_call(
        _proj_kernel,
        grid=(t // tm, n // tn),
        in_specs=[pl.BlockSpec((tm, d), lambda i, j: (i, 0)),
                  pl.BlockSpec((None, d, tn), lambda i, j: (l, 0, j)),
                  _layer_spec(l, (d, LANES))],
        out_specs=[pl.BlockSpec((tm, tn), lambda i, j: (i, j)),
                   pl.BlockSpec((tm, LANES), lambda i, j: (i, 0))],
        out_shape=[jax.ShapeDtypeStruct((t, n), BF16), jax.ShapeDtypeStruct((t, LANES), F32)],
        compiler_params=_cparams(("parallel", "arbitrary")),
        name="in_proj",
    )(h16, w_big, w_small)


def _ssd_kernel(z_ref, x_ref, bc_ref, sm_ref, cwx_ref, cwbc_ref, cbx_ref, cbbc_ref, vec_ref,
                dskip_ref, nw_ref, e8_ref, tri_ref, o_ref, state_ref, xtail_ref, bctail_ref):
    dtb_ref = vec_ref.at[VEC_SSD_DTB:VEC_SSD_DTB + 1]
    alog_ref = vec_ref.at[VEC_SSD_ALOG:VEC_SSD_ALOG + 1]
    first = pl.program_id(1) == 0
    ts = x_ref.shape[0]
    q = SSD_CHUNK
    hp = SSD_HEADS // SSD_GROUPS * SSD_HEAD_DIM
    e8 = e8_ref[...]
    tri = tri_ref[...]

    @pl.when(first)
    def _():
        state_ref[...] = jnp.zeros_like(state_ref)
        xtail_ref[...] = jnp.zeros_like(xtail_ref)
        bctail_ref[...] = jnp.zeros_like(bctail_ref)

    xs, xtail_ref[...] = _causal_conv(x_ref[...].astype(F32), xtail_ref[...], cwx_ref[...])
    xs = _silu(xs + cbx_ref[...])
    bcm, bctail_ref[...] = _causal_conv(bc_ref[...].astype(F32), bctail_ref[...], cwbc_ref[...])
    bcm = _silu(bcm + cbbc_ref[...]).astype(BF16)
    lane = lax.broadcasted_iota(jnp.int32, (1, LANES), 1)
    head_lane = (lane >= SM_DT) & (lane < SM_DT + SSD_HEADS)
    a_neg = jnp.where(head_lane, -jnp.exp(alog_ref[...]), 0.0)
    dt = _softplus(sm_ref[...] + dtb_ref[...])
    dta = dt * a_neg
    xdt = xs * _sel_right(dt, e8, 2)
    ri = lax.broadcasted_iota(jnp.int32, (q, q), 0)
    ci = lax.broadcasted_iota(jnp.int32, (q, q), 1)
    lower = ci <= ri
    lane_g = lax.broadcasted_iota(jnp.int32, (1, hp), 1) // SSD_HEAD_DIM

    for c in range(ts // q):
        sl = slice(c * q, (c + 1) * q)
        acs = _sel_left(tri, dta[sl], 3)
        acs_t = acs.T
        eacs = _sel_right(jnp.exp(acs), e8, 2)
        dec_end = _sel_right(jnp.exp(acs[q - 1:q, :] - acs), e8, 2)
        ys = []
        for g in range(SSD_GROUPS):
            gl = slice(g * hp, (g + 1) * hp)
            bm = bcm[sl, g * SSD_STATE:(g + 1) * SSD_STATE]
            cm = bcm[sl, (SSD_GROUPS + g) * SSD_STATE:(SSD_GROUPS + g + 1) * SSD_STATE]
            cb = _dot_nt(cm, bm)
            xg = xdt[sl, gl]
            xg16 = xg.astype(BF16)
            zero16 = jnp.zeros_like(xg16)
            s_prev = state_ref[g]
            y = _dot(cm, s_prev.astype(BF16)) * eacs[:, gl]
            for e in range(SSD_HEADS // SSD_GROUPS):
                h = SM_DT + g * (SSD_HEADS // SSD_GROUPS) + e
                seg = jnp.exp(jnp.where(lower, acs[:, h:h + 1] - acs_t[h:h + 1, :], NEG_BIG))
                xe = jnp.where(lane_g == e, xg16, zero16)
                y = y + _dot((cb * seg).astype(BF16), xe)
            state_ref[g] = s_prev * eacs[q - 1:q, gl] + _dot_tn(bm, (xg * dec_end[:, gl]).astype(BF16))
            ys.append(y)
        y = jnp.concatenate(ys, axis=1) + dskip_ref[...] * xs[sl]
        y = y * _silu(z_ref[sl, :].astype(F32))
        y = y * lax.rsqrt(jnp.mean(y * y, axis=-1, keepdims=True) + NORM_EPS) * nw_ref[...]
        o_ref[sl, :] = y.astype(BF16)


def _ssd_mixer(proj, small, p, l, batch, seq, ts):
    nb = seq // ts
    w = BRANCH_W
    rows = lambda col: pl.BlockSpec((ts, w), lambda b, s, col=col: (b * nb + s, col))
    half = lambda r, n: pl.BlockSpec((None, r, w), lambda b, s, n=n: (l, 0, n))
    hp = SSD_HEADS // SSD_GROUPS * SSD_HEAD_DIM
    return pl.pallas_call(
        _ssd_kernel,
        grid=(batch, nb),
        in_specs=[rows(COL_Z), rows(COL_X), rows(COL_BC),
                  pl.BlockSpec((ts, LANES), lambda b, s: (b * nb + s, 0)),
                  half(SSD_CONV, 0), half(SSD_CONV, 1), half(1, 0), half(1, 1),
                  _layer_spec(l, (VEC_ROWS, LANES)),
                  _layer_spec(l, (1, w)), _layer_spec(l, (1, w)),
                  _const_spec((LANES, w)), _const_spec((SSD_CHUNK, SSD_CHUNK))],
        out_specs=pl.BlockSpec((ts, w), lambda b, s: (b * nb + s, 0)),
        out_shape=jax.ShapeDtypeStruct((batch * seq, w), BF16),
        scratch_shapes=[pltpu.VMEM((SSD_GROUPS, SSD_STATE, hp), F32),
                        pltpu.VMEM((SUBLANES, w), F32), pltpu.VMEM((SUBLANES, w), F32)],
        compiler_params=_cparams(("parallel", "arbitrary")),
        name="ssd_mixer",
    )(proj, proj, proj, small, p["ssd_conv_w"], p["ssd_conv_w"], p["ssd_conv_b"], p["ssd_conv_b"], p["vec"],
      p["ssd_dskip"], p["ssd_nw"], p["e8"], p["tri_ssd"])


DN_SLOTS = 2
DN_STAGGER = 9


def _interleave_staggered(gens, stagger):
    active, waiting, rounds = [], list(gens), 0
    while active or waiting:
        if waiting and rounds % stagger == 0:
            active.append(waiting.pop(0))
        for g in list(active):
            try:
                next(g)
            except StopIteration:
                active.remove(g)
        rounds += 1


def _dn_kernel(q_ref, k_ref, v_ref, gate_ref, sm_ref, cwq_ref, cwk_ref, cwv_ref, vec_ref, nw_ref,
               tri_ref, hsum_ref, o_ref, state_ref, qtail_ref, ktail_ref, vtail_ref):
    alog_ref = vec_ref.at[VEC_DN_ALOG:VEC_DN_ALOG + 1]
    dtb_ref = vec_ref.at[VEC_DN_DTB:VEC_DN_DTB + 1]
    ts = q_ref.shape[1]
    c = DN_CHUNK
    dk = DN_HEAD_DIM
    nh = DN_HEADS

    @pl.when(pl.program_id(1) == 0)
    def _():
        state_ref[...] = jnp.zeros_like(state_ref)
        qtail_ref[...] = jnp.zeros_like(qtail_ref)
        ktail_ref[...] = jnp.zeros_like(ktail_ref)
        vtail_ref[...] = jnp.zeros_like(vtail_ref)

    def row_work(r):
        qf, qtail_ref[r] = _causal_conv(q_ref[r].astype(F32), qtail_ref[r], cwq_ref[...])
        qf = _silu(qf)
        yield
        kf, ktail_ref[r] = _causal_conv(k_ref[r].astype(F32), ktail_ref[r], cwk_ref[...])
        kf = _silu(kf)
        yield
        vf, vtail_ref[r] = _causal_conv(v_ref[r].astype(F32), vtail_ref[r], cwv_ref[...])
        vf = _silu(vf)
        yield
        qf = qf * (lax.rsqrt(_sel_right(qf * qf, hsum_ref[...], 1) + NORM_EPS) * (dk ** -0.5))
        kf = kf * lax.rsqrt(_sel_right(kf * kf, hsum_ref[...], 1) + NORM_EPS)
        yield
        sm = sm_ref[r]
        lane = lax.broadcasted_iota(jnp.int32, (1, LANES), 1)
        a_lane = (lane >= SM_A) & (lane < SM_A + nh)
        neg_a = jnp.where(a_lane, -jnp.exp(alog_ref[...]), 0.0)
        gdec = neg_a * _softplus(sm + dtb_ref[...])
        beta_s = _sigmoid(sm)
        gcs = _sel_left(tri_ref[...], gdec, 3)
        gtot = jnp.concatenate([jnp.broadcast_to(gcs[(n + 1) * c - 1:(n + 1) * c], (c, LANES))
                                for n in range(ts // c)], axis=0)
        gcs_t = gcs.T
        egcs_s = jnp.exp(gcs)
        edec_s = jnp.exp(gtot - gcs)
        elast_s = jnp.exp(gtot)
        ri = lax.broadcasted_iota(jnp.int32, (ts, ts), 0)
        ci = lax.broadcasted_iota(jnp.int32, (ts, ts), 1)
        lower = (ci >= ri // c * c) & (ci <= ri)
        offdiag = ci != ri
        yield

        def head_lanes(x, lane_idx):
            return jnp.broadcast_to(x[:, lane_idx:lane_idx + 1], (x.shape[0], dk))

        a_mats, rhs, qk, qg, kd, elast = [], [], [], [], [], []
        for h in range(nh):
            hl = slice(h * dk, (h + 1) * dk)
            gl = SM_A + h
            q, k = qf[:, hl], kf[:, hl]
            bh = head_lanes(beta_s, SM_BETA + h)
            egcs = head_lanes(egcs_s, gl)
            gamma = jnp.exp(jnp.where(lower, gcs[:, gl:gl + 1] - gcs_t[gl:gl + 1, :], NEG_BIG))
            kb = k * bh
            k16 = k.astype(BF16)
            a_mats.append(jnp.where(offdiag, _dot_nt(kb.astype(BF16), k16) * gamma, 0.0))
            yield
            rhs.append(jnp.concatenate([kb * egcs, vf[:, hl] * bh], axis=1).astype(BF16))
            qk.append((_dot_nt(q.astype(BF16), k16) * gamma).astype(BF16))
            yield
            qg.append((q * egcs).astype(BF16))
            kd.append((k * head_lanes(edec_s, gl)).astype(BF16))
            elast.append([head_lanes(elast_s[n * c:n * c + 1], gl) for n in range(ts // c)])
            yield

        eye = jnp.where(offdiag, 0.0, 1.0).astype(F32)
        pws = [(-a).astype(BF16) for a in a_mats]
        invs = [eye - a for a in a_mats]
        for _ in range(int(math.log2(c)) - 1):
            sq = [_dot(pw, pw) for pw in pws]
            yield
            pws = [x.astype(BF16) for x in sq]
            invs = [inv + _dot(inv.astype(BF16), pw) for inv, pw in zip(invs, pws)]
            yield
        wu = [_dot(invs[h].astype(BF16), rhs[h]) for h in range(nh)]
        yield

        pn = [[_dot_tn(kd[h][ch * c:(ch + 1) * c], wu[h][ch * c:(ch + 1) * c].astype(BF16))
               for ch in range(ts // c)] for h in range(nh)]
        yield
        states = [state_ref[r, h] for h in range(nh)]
        v_new = [[] for _ in range(nh)]
        o_state = [[] for _ in range(nh)]
        for ch in range(ts // c):
            sl = slice(ch * c, (ch + 1) * c)
            for h in range(nh):
                s16 = states[h].astype(BF16)
                states[h] = (states[h] * elast[h][ch] - _dot(pn[h][ch][:, :dk].astype(BF16), s16)
                             + pn[h][ch][:, dk:])
                lhs = jnp.concatenate([wu[h][sl, :dk].astype(BF16), qg[h][sl]], axis=0)
                ws = _dot(lhs, s16)
                v_new[h].append(wu[h][sl, dk:] - ws[:c])
                o_state[h].append(ws[c:])
            yield
        for h in range(nh):
            hl = slice(h * dk, (h + 1) * dk)
            state_ref[r, h] = states[h]
            vn_all = jnp.concatenate(v_new[h], axis=0).astype(BF16)
            o = jnp.concatenate(o_state[h], axis=0) + _dot(qk[h], vn_all)
            o = o * lax.rsqrt(jnp.mean(o * o, axis=-1, keepdims=True) + NORM_EPS) * nw_ref[...]
            o_ref[r, :, hl] = (o * _silu(gate_ref[r, :, hl].astype(F32))).astype(BF16)
            yield

    _interleave_staggered([row_work(r) for r in range(q_ref.shape[0])], DN_STAGGER)


def _dn_mixer(proj, small, p, l, batch, seq, ts):
    nb = seq // ts
    w = BRANCH_W
    nh, dk = DN_HEADS, DN_HEAD_DIM
    nr = math.gcd(batch, DN_SLOTS)
    proj3 = proj.reshape(batch, seq, proj.shape[1])
    small3 = small.reshape(batch, seq, LANES)
    rows = lambda col: pl.BlockSpec((nr, ts, w), lambda b, s, col=col: (b, s, col))
    conv = lambda n: pl.BlockSpec((None, DN_CONV, w), lambda b, s, n=n: (l, 0, n))
    out = pl.pallas_call(
        _dn_kernel,
        grid=(batch // nr, nb),
        in_specs=[rows(COL_DQ), rows(COL_DK), rows(COL_DV), rows(COL_DG),
                  pl.BlockSpec((nr, ts, LANES), lambda b, s: (b, s, 0)),
                  conv(0), conv(1), conv(2),
                  _layer_spec(l, (VEC_ROWS, LANES)), _layer_spec(l, (1, dk)),
                  _const_spec((ts, ts)), _const_spec((w, w))],
        out_specs=pl.BlockSpec((nr, ts, w), lambda b, s: (b, s, 0)),
        out_shape=jax.ShapeDtypeStruct((batch, seq, w), BF16),
        scratch_shapes=[pltpu.VMEM((nr, nh, dk, dk), F32),
                        pltpu.VMEM((nr, SUBLANES, w), F32), pltpu.VMEM((nr, SUBLANES, w), F32),
                        pltpu.VMEM((nr, SUBLANES, w), F32)],
        compiler_params=_cparams(("parallel", "arbitrary")),
        name="dn_mixer",
    )(proj3, proj3, proj3, proj3, small3, p["dn_conv_w"], p["dn_conv_w"], p["dn_conv_w"], p["vec"],
      p["dn_nw"], _block_diag(ts, DN_CHUNK, True), _block_diag(w, DN_HEAD_DIM, False))
    return out.reshape(batch * seq, w)


def _sg_kernel(u_ref, v_ref, g_ref, b_ref, w_ref, bs_ref, o_ref):
    ts = u_ref.shape[0]
    q = SG_CHUNK
    gd = BRANCH_W // SG_GROUPS
    u = jax.nn.gelu(u_ref[...].astype(F32))
    v = _layer_norm(jax.nn.gelu(v_ref[...].astype(F32)), g_ref[...], b_ref[...])
    ri = lax.broadcasted_iota(jnp.int32, (q, q), 0)
    ci = lax.broadcasted_iota(jnp.int32, (q, q), 1)
    lower = ci <= ri
    for g in range(SG_GROUPS):
        gl = slice(g * gd, (g + 1) * gd)
        wg = jnp.where(lower, w_ref[g], 0.0).astype(BF16)
        for c in range(ts // q):
            sl = slice(c * q, (c + 1) * q)
            mixed = _dot(wg, v[sl, gl].astype(BF16)) + bs_ref[:, gl]
            o_ref[sl, gl] = (u[sl, gl] * mixed).astype(BF16)


def _sg_mixer(proj, p, l, t, ts):
    w = BRANCH_W
    rows = lambda col: pl.BlockSpec((ts, w), lambda i, col=col: (i, col))
    return pl.pallas_call(
        _sg_kernel,
        grid=(t // ts,),
        in_specs=[rows(COL_SU), rows(COL_SV), _layer_spec(l, (1, w)), _layer_spec(l, (1, w)),
                  _layer_spec(l, (SG_GROUPS, SG_CHUNK, SG_CHUNK)), _layer_spec(l, (SG_CHUNK, w))],
        out_specs=pl.BlockSpec((ts, w), lambda i: (i, 0)),
        out_shape=jax.ShapeDtypeStruct((t, w), BF16),
        compiler_params=_cparams(("parallel",)),
        name="sg_mixer",
    )(proj, proj, p["sg_g"], p["sg_b"], p["sg_w"], p["sg_bs"])


FOX_SPLIT = 3
FOX_CUM_CHUNK = 128
FOX_AUG = LANES // (2 * FOX_HEADS)
assert FOX_AUG >= 2 * FOX_SPLIT
LOG2E = math.log2(math.e)


def _fcum_kernel(sm_ref, k_ref, vec_ref, tri_ref, eq_ref, ek_ref, oq_ref, ok_ref, qa_ref, kt_ref, carry_ref):
    fb_ref = vec_ref.at[VEC_FOX_FB:VEC_FOX_FB + 1]
    @pl.when(pl.program_id(1) == 0)
    def _():
        carry_ref[...] = jnp.zeros_like(carry_ref)

    n = sm_ref.shape[0]
    per = LANES // FOX_HEAD_DIM
    ls = jax.nn.log_sigmoid(sm_ref[...] + fb_ref[...])
    tri = tri_ref[...]
    carry = carry_ref[...]
    pieces = []
    for r in range(n // FOX_CUM_CHUNK):
        piece = _sel_left(tri, ls[r * FOX_CUM_CHUNK:(r + 1) * FOX_CUM_CHUNK], 3) + carry
        carry = piece[FOX_CUM_CHUNK - 1:FOX_CUM_CHUNK, :]
        pieces.append(piece)
    c = jnp.concatenate(pieces, axis=0)
    carry_ref[...] = carry
    terms = _split_terms(c * LOG2E, FOX_SPLIT)
    cq = sum(_dot(t, eq_ref[i]) for i, t in enumerate(terms))
    qa_ref[...] = (oq_ref[...] + cq).astype(BF16)
    ck_t = sum(_dot_nt(ek_ref[i], t) for i, t in enumerate(terms))
    aug_t = (ok_ref[...] - ck_t).astype(BF16)
    pad = jnp.zeros((LANES - FOX_HEAD_DIM - FOX_AUG, n), BF16)
    for b in range(FOX_HEADS // per):
        k_t = k_ref[:, b * LANES:(b + 1) * LANES].astype(F32).T.astype(BF16)
        for e in range(per):
            h = b * per + e
            kt_ref[h] = jnp.concatenate([k_t[e * FOX_HEAD_DIM:(e + 1) * FOX_HEAD_DIM],
                                         aug_t[h * FOX_AUG:(h + 1) * FOX_AUG], pad], axis=0)


def _forget_cumsum(proj, small, p, l, batch, seq, tc):
    nb = seq // tc
    na = FOX_HEADS * FOX_AUG
    return pl.pallas_call(
        _fcum_kernel,
        grid=(batch, nb),
        in_specs=[pl.BlockSpec((tc, LANES), lambda b, s: (b * nb + s, 0)),
                  pl.BlockSpec((tc, BRANCH_W), lambda b, s: (b * nb + s, COL_FK)),
                  _layer_spec(l, (VEC_ROWS, LANES)), _const_spec((FOX_CUM_CHUNK, FOX_CUM_CHUNK)),
                  _const_spec((FOX_SPLIT, LANES, LANES)), _const_spec((FOX_SPLIT, na, LANES)),
                  _const_spec((1, LANES)), _const_spec((na, 1))],
        out_specs=[pl.BlockSpec((tc, LANES), lambda b, s: (b * nb + s, 0)),
                   pl.BlockSpec((None, FOX_HEADS, LANES, tc), lambda b, s: (b, 0, 0, s))],
        out_shape=[jax.ShapeDtypeStruct((batch * seq, LANES), BF16),
                   jax.ShapeDtypeStruct((batch, FOX_HEADS, LANES, seq), BF16)],
        scratch_shapes=[pltpu.VMEM((1, LANES), F32)],
        compiler_params=_cparams(("parallel", "arbitrary")),
        name="forget_cumsum",
    )(small, proj, p["vec"], p["tri_fox"], p["fox_eq"], p["fox_ek"], p["fox_oq"], p["fox_ok"])


def _roll_lanes(x, shift):
    shift %= x.shape[-1]
    return x if shift == 0 else pltpu.roll(x, shift, axis=x.ndim - 1)


def _fox_kernel(ii_ref, jj_ref, q_ref, kt_ref, v_ref, qa_ref, o_ref, qaug_ref, m_ref, acc_ref):
    t = pl.program_id(1)
    i = ii_ref[t]
    j = jj_ref[t]
    tq = q_ref.shape[0]
    per = LANES // FOX_HEAD_DIM
    lane = lax.broadcasted_iota(jnp.int32, (1, LANES), 1)

    @pl.when(j == 0)
    def _():
        m_ref[...] = jnp.full_like(m_ref, NEG_BIG)
        acc_ref[...] = jnp.zeros_like(acc_ref)
        qa = qa_ref[...].astype(F32)
        for h in range(FOX_HEADS):
            bl = slice(h // per * LANES, (h // per + 1) * LANES)
            q2 = (q_ref[:, bl].astype(F32) * (LOG2E * FOX_HEAD_DIM ** -0.5)).astype(BF16).astype(F32)
            q_own = _roll_lanes(q2, LANES - h % per * FOX_HEAD_DIM)
            extra = _roll_lanes(qa, FOX_HEAD_DIM - h * FOX_AUG)
            in_extra = (lane >= FOX_HEAD_DIM) & (lane < FOX_HEAD_DIM + FOX_AUG)
            qaug_ref[h] = jnp.where(lane < FOX_HEAD_DIM, q_own, jnp.where(in_extra, extra, 0.0)).astype(BF16)

    tk = kt_ref.shape[2]
    nsub = tq // tk

    def logits(h, rows):
        return _dot(qaug_ref[h, rows, :], kt_ref[h])

    def softmax(h, rows, s, masked):
        if masked:
            s = jnp.where(lax.broadcasted_iota(jnp.int32, (tk, tk), 1) <= lax.broadcasted_iota(jnp.int32, (tk, tk), 0),
                          s, NEG_BIG)
        m_old = m_ref[h, rows, :]
        m_new = jnp.maximum(m_old, jnp.max(s, axis=-1, keepdims=True))
        m_ref[h, rows, :] = m_new
        return jnp.exp2(s - m_new[:, :1]).astype(BF16), jnp.exp2(m_old - m_new)

    def accumulate(h, rows, p, alpha):
        bl = slice(h // per * LANES, (h // per + 1) * LANES)
        v2 = v_ref[:, bl]
        vaug = jnp.concatenate([v2, jnp.ones_like(v2)], axis=1)
        acc_ref[h, rows, :] = jnp.concatenate([alpha, alpha], axis=1) * acc_ref[h, rows, :] + _dot(p, vaug)

    def step(first_sub):
        units = [(h, r) for h in range(FOX_HEADS) for r in range(max(first_sub, 0), nsub)]
        rows = lambda r: slice(r * tk, (r + 1) * tk)
        s_next = logits(units[0][0], rows(units[0][1]))
        for n, (h, r) in enumerate(units):
            s = s_next
            if n + 1 < len(units):
                s_next = logits(units[n + 1][0], rows(units[n + 1][1]))
            accumulate(h, rows(r), *softmax(h, rows(r), s, r == first_sub))

    d = j - i * nsub

    @pl.when(d < 0)
    def _():
        step(-1)

    for k in range(nsub):
        @pl.when(d == k)
        def _(k=k):
            step(k)

    @pl.when(d == nsub - 1)
    def _():
        for b in range(FOX_HEADS // per):
            outs = []
            for e in range(per):
                a = acc_ref[b * per + e]
                outs.append(a[:, :LANES] / a[:, LANES:])
            o_ref[:, b * LANES:(b + 1) * LANES] = jnp.where(lane // FOX_HEAD_DIM == 0, outs[0], outs[1]).astype(BF16)


def _fox_mixer(proj, qa, kt, batch, seq, tq, tk):
    nq, nk = seq // tq, seq // tk
    w = BRANCH_W
    pairs = [(i, j) for i in range(nq) for j in range((i + 1) * (tq // tk))]
    ii = jnp.asarray([p[0] for p in pairs], jnp.int32)
    jj = jnp.asarray([p[1] for p in pairs], jnp.int32)
    qrow = lambda b, t, ii, jj: b * nq + ii[t]
    krow = lambda b, t, ii, jj: b * nk + jj[t]
    grid_spec = pltpu.PrefetchScalarGridSpec(
        num_scalar_prefetch=2,
        grid=(batch, len(pairs)),
        in_specs=[pl.BlockSpec((tq, w), lambda b, t, ii, jj: (qrow(b, t, ii, jj), COL_FQ)),
                  pl.BlockSpec((None, FOX_HEADS, LANES, tk), lambda b, t, ii, jj: (b, 0, 0, jj[t])),
                  pl.BlockSpec((tk, w), lambda b, t, ii, jj: (krow(b, t, ii, jj), COL_FV)),
                  pl.BlockSpec((tq, LANES), lambda b, t, ii, jj: (qrow(b, t, ii, jj), 0))],
        out_specs=pl.BlockSpec((tq, w), lambda b, t, ii, jj: (qrow(b, t, ii, jj), 0)),
        scratch_shapes=[pltpu.VMEM((FOX_HEADS, tq, LANES), BF16),
                        pltpu.VMEM((FOX_HEADS, tq, LANES), F32),
                        pltpu.VMEM((FOX_HEADS, tq, 2 * LANES), F32)])
    return pl.pallas_call(
        _fox_kernel,
        grid_spec=grid_spec,
        out_shape=jax.ShapeDtypeStruct((batch * seq, w), BF16),
        compiler_params=_cparams(("parallel", "arbitrary")),
        name="fox_mixer",
    )(ii, jj, proj, kt, proj, qa)


def _merge_kernel(alpha, ya_ref, yb_ref, yc_ref, yd_ref, g0_ref, g1_ref, g2_ref, g3_ref, gb_ref,
                  wb_ref, wo_ref, h_ref, lg_ref, lb_ref, o32_ref, o16_ref):
    merged = None
    for i, (y_ref, gl_ref) in enumerate(((ya_ref, g0_ref), (yb_ref, g1_ref), (yc_ref, g2_ref), (yd_ref, g3_ref))):
        gate = _sigmoid(gl_ref[...].astype(F32) + gb_ref[i:i + 1, :])
        term = gate * _dot(y_ref[...], wb_ref[i])
        merged = term if merged is None else merged + term
    mix = _dot(merged.astype(BF16), wo_ref[...])
    y = _layer_norm(alpha * h_ref[...] + mix, lg_ref[...], lb_ref[...])
    o32_ref[...] = y
    o16_ref[...] = y.astype(BF16)


def _merge(ys, proj, h32, p, l, alpha, tm):
    t, d = h32.shape
    w = BRANCH_W
    gate0 = (COL_GATE * w) // d
    yspec = pl.BlockSpec((tm, w), lambda i: (i, 0))
    gspec = lambda n: pl.BlockSpec((tm, d), lambda i, n=n: (i, gate0 + n))
    row = pl.BlockSpec((tm, d), lambda i: (i, 0))
    return pl.pallas_call(
        functools.partial(_merge_kernel, alpha),
        grid=(t // tm,),
        in_specs=[yspec] * 4 + [gspec(n) for n in range(N_BRANCH)] +
                 [_layer_spec(l, (N_BRANCH, d)), _layer_spec(l, (N_BRANCH, w, d)), _layer_spec(l, (d, d)), row,
                  _layer_spec(l, (1, d)), _layer_spec(l, (1, d))],
        out_specs=[row, row],
        out_shape=[jax.ShapeDtypeStruct((t, d), F32), jax.ShapeDtypeStruct((t, d), BF16)],
        compiler_params=_cparams(("parallel",)),
        name="merge_norm",
    )(*ys, proj, proj, proj, proj, p["gate_b"], p["w_branch"], p["w_out"], h32, p["ln1_g"], p["ln1_b"])


def _ffn_kernel(alpha, h16_ref, h32_ref, wu_ref, wd_ref, lg_ref, lb_ref, o32_ref, o16_ref, acc_ref):
    f = pl.program_id(1)

    @pl.when(f == 0)
    def _():
        acc_ref[...] = jnp.zeros_like(acc_ref)

    up = jnp.maximum(_dot(h16_ref[...], wu_ref[...]), 0.0)
    acc_ref[...] += _dot((up * up).astype(BF16), wd_ref[...])

    @pl.when(f == pl.num_programs(1) - 1)
    def _():
        y = _layer_norm(alpha * h32_ref[...] + acc_ref[...], lg_ref[...], lb_ref[...])
        o32_ref[...] = y
        o16_ref[...] = y.astype(BF16)


def _ffn(h16, h32, p, l, alpha, tm, tf):
    t, d = h32.shape
    dff = p["w_up"].shape[2]
    row = pl.BlockSpec((tm, d), lambda i, f: (i, 0))
    return pl.pallas_call(
        functools.partial(_ffn_kernel, alpha),
        grid=(t // tm, dff // tf),
        in_specs=[row, row, pl.BlockSpec((None, d, tf), lambda i, f: (l, 0, f)),
                  pl.BlockSpec((None, tf, d), lambda i, f: (l, f, 0)),
                  _layer_spec(l, (1, d)), _layer_spec(l, (1, d))],
        out_specs=[row, row],
        out_shape=[jax.ShapeDtypeStruct((t, d), F32), jax.ShapeDtypeStruct((t, d), BF16)],
        scratch_shapes=[pltpu.VMEM((tm, d), F32)],
        compiler_params=_cparams(("parallel", "arbitrary")),
        name="ffn_norm",
    )(h16, h32, p["w_up"], p["w_down"], p["ln2_g"], p["ln2_b"])


VEC_SSD_DTB, VEC_SSD_ALOG, VEC_DN_ALOG, VEC_DN_DTB, VEC_FOX_FB, VEC_ROWS = 0, 1, 2, 3, 4, SUBLANES


def _lane_row(v, start):
    return jnp.pad(v.astype(F32), ((0, 0), (start, LANES - start - v.shape[1])))[:, None, :]


def _expander(start, heads, width):
    m = np.zeros((LANES, heads * width), np.float32)
    for h in range(heads):
        m[start + h, h * width:(h + 1) * width] = 1.0
    return jnp.asarray(m, BF16)


def _tri(n):
    return jnp.asarray(np.tril(np.ones((n, n), np.float32)), BF16)


def _block_diag(n, c, lower):
    blk = np.tril(np.ones((c, c), np.float32)) if lower else np.ones((c, c), np.float32)
    return jnp.asarray(np.kron(np.eye(n // c, dtype=np.float32), blk), BF16)


def _fox_aug_constants():
    na = FOX_HEADS * FOX_AUG
    eq = np.zeros((FOX_SPLIT, LANES, LANES), np.float32)
    ek = np.zeros((FOX_SPLIT, na, LANES), np.float32)
    oq = np.zeros((1, LANES), np.float32)
    ok = np.zeros((na, 1), np.float32)
    for h in range(FOX_HEADS):
        for i in range(FOX_SPLIT):
            ek[i, h * FOX_AUG + i, SM_F + h] = 1.0
            eq[i, SM_F + h, h * FOX_AUG + FOX_SPLIT + i] = 1.0
            oq[0, h * FOX_AUG + i] = 1.0
            ok[h * FOX_AUG + FOX_SPLIT + i, 0] = 1.0
    return jnp.asarray(eq, BF16), jnp.asarray(ek, BF16), jnp.asarray(oq), jnp.asarray(ok)


def _prepare_params(a):
    w = BRANCH_W
    w_in = a["w_in"]
    depth, d = w_in.shape[:2]
    o_dt = w + SSD_XBC
    o_qkv = o_dt + SSD_HEADS
    o_beta = o_qkv + 3 * w
    o_a = o_beta + DN_HEADS
    o_gate = o_a + DN_HEADS
    o_f = o_gate + w + 2 * w + 3 * w
    o_gates = o_f + FOX_HEADS
    w_big = jnp.concatenate([w_in[..., :o_dt], w_in[..., o_qkv:o_beta], w_in[..., o_gate:o_f], w_in[..., o_gates:]],
                            axis=-1).astype(BF16)
    n_small = SSD_HEADS + 2 * DN_HEADS + FOX_HEADS
    assert (SM_DT, SM_BETA, SM_A, SM_F) == (0, SSD_HEADS, SSD_HEADS + DN_HEADS, SSD_HEADS + 2 * DN_HEADS)
    w_small = jnp.concatenate([w_in[..., o_dt:o_qkv], w_in[..., o_beta:o_a], w_in[..., o_a:o_gate],
                               w_in[..., o_f:o_gates], jnp.zeros((depth, d, LANES - n_small), w_in.dtype)],
                              axis=-1).astype(BF16)
    vec = jnp.concatenate([_lane_row(a["ssd_dt_bias"], SM_DT), _lane_row(a["ssd_a_log"], SM_DT),
                           _lane_row(a["dn_a_log"], SM_A), _lane_row(a["dn_dt_bias"], SM_A),
                           _lane_row(a["fox_f_bias"], SM_F),
                           jnp.zeros((depth, VEC_ROWS - 5, LANES), F32)], axis=1)
    fox_eq, fox_ek, fox_oq, fox_ok = _fox_aug_constants()
    row = lambda v: v[:, None, :]
    return {
        "w_big": w_big, "w_small": w_small, "vec": vec,
        "ssd_conv_w": a["ssd_conv_w"], "ssd_conv_b": row(a["ssd_conv_b"]), "dn_conv_w": a["dn_conv_w"],
        "ssd_dskip": row(jnp.repeat(a["ssd_d"], SSD_HEAD_DIM, axis=1)), "ssd_nw": row(a["ssd_norm_w"]),
        "e8": _expander(SM_DT, SSD_HEADS, SSD_HEAD_DIM), "tri_ssd": _tri(SSD_CHUNK),
        "dn_nw": row(a["dn_norm_w"]),
        "sg_g": row(a["sg_ln_g"]), "sg_b": row(a["sg_ln_b"]), "sg_w": a["sg_w"],
        "sg_bs": jnp.repeat(jnp.swapaxes(a["sg_b"], 1, 2), BRANCH_W // SG_GROUPS, axis=2),
        "tri_fox": _tri(FOX_CUM_CHUNK), "fox_eq": fox_eq, "fox_ek": fox_ek, "fox_oq": fox_oq, "fox_ok": fox_ok,
        "gate_b": a["gate_b"], "w_branch": a["w_branch"].astype(BF16), "w_out": a["w_out"].astype(BF16),
        "ln1_g": row(a["ln1_g"]), "ln1_b": row(a["ln1_b"]),
        "w_up": a["w_up"].astype(BF16), "w_down": a["w_down"].astype(BF16),
        "ln2_g": row(a["ln2_g"]), "ln2_b": row(a["ln2_b"]),
    }


def _tiles(batch, seq):
    t = batch * seq
    pick = lambda n, cap: math.gcd(n, cap)
    return {
        "ln": pick(t, 1024), "proj_m": pick(t, 2048), "proj_n": 2048, "ssd": pick(seq, 512), "dn": pick(seq, 256),
        "sg": pick(t, 2048), "fcum": pick(seq, 1024), "fox_q": pick(seq, 1024), "fox_k": pick(seq, 512), "merge": pick(t, 1024),
        "ffn_m": pick(t, 512), "ffn_f": 4096,
    }


def kernel(x, ln_in_g, ln_in_b, w_in, ssd_conv_w, ssd_conv_b, ssd_dt_bias, ssd_a_log, ssd_d, ssd_norm_w, dn_conv_w,
           dn_a_log, dn_dt_bias, dn_norm_w, sg_ln_g, sg_ln_b, sg_w, sg_b, fox_f_bias, gate_b, w_branch, w_out,
           ln1_g, ln1_b, w_up, w_down, ln2_g, ln2_b):
    batch, seq, d = x.shape
    depth = w_in.shape[0]
    alpha = (2 * depth) ** 0.25
    a = dict(w_in=w_in, ssd_conv_w=ssd_conv_w, ssd_conv_b=ssd_conv_b, ssd_dt_bias=ssd_dt_bias, ssd_a_log=ssd_a_log,
             ssd_d=ssd_d, ssd_norm_w=ssd_norm_w, dn_conv_w=dn_conv_w, dn_a_log=dn_a_log, dn_dt_bias=dn_dt_bias,
             dn_norm_w=dn_norm_w, sg_ln_g=sg_ln_g, sg_ln_b=sg_ln_b, sg_w=sg_w, sg_b=sg_b, fox_f_bias=fox_f_bias,
             gate_b=gate_b, w_branch=w_branch, w_out=w_out, ln1_g=ln1_g, ln1_b=ln1_b, w_up=w_up, w_down=w_down,
             ln2_g=ln2_g, ln2_b=ln2_b)
    tl = _tiles(batch, seq)
    t = batch * seq
    h32, h16 = _entry_norm(x.reshape(t, d), ln_in_g, ln_in_b, tl["ln"])
    p = _prepare_params(a)
    for l in range(depth):
        proj, small = _in_proj(h16, p, l, tl["proj_m"], tl["proj_n"])
        y_a = _ssd_mixer(proj, small, p, l, batch, seq, tl["ssd"])
        y_b = _dn_mixer(proj, small, p, l, batch, seq, tl["dn"])
        y_c = _sg_mixer(proj, p, l, t, tl["sg"])
        qa, kt = _forget_cumsum(proj, small, p, l, batch, seq, tl["fcum"])
        y_d = _fox_mixer(proj, qa, kt, batch, seq, tl["fox_q"], tl["fox_k"])
        h32, h16 = _merge((y_a, y_b, y_c, y_d), proj, h32, p, l, alpha, tl["merge"])
        h32, h16 = _ffn(h16, h32, p, l, alpha, tl["ffn_m"], tl["ffn_f"])
    return h32.reshape(batch, seq, d)
```
